```python
import jax, jax.numpy as jnp
from jax import lax
import numpy as np

D_MODEL = 1024
BATCH = 4
SEQ = 4096
DEPTH = 1
DEC_BATCH = 128
DEC_SEQ = 8
PAST_LEN = 8192
PAGE_SIZE = 128

D_MIX = D_MODEL
HEAD_DIM = 64
N_HEADS_ATT = (D_MIX // 2) // HEAD_DIM
D_ATT = N_HEADS_ATT * HEAD_DIM
DILATED_PATTERNS = ((128, 1), (512, 4), (2048, 16))
WINDOW_MAX = 2048
ATT_BLOCK = 128
ROT_DIM = HEAD_DIM // 4
ROPE_THETA = 500000.0
N_HEADS_GLA = 4
D_GLA_V = D_MIX - D_ATT
D_GLA_K = D_GLA_V // 2
GLA_DK = D_GLA_K // N_HEADS_GLA
GLA_DV = D_GLA_V // N_HEADS_GLA
GLA_GATE_RANK = 16
GLA_TAU = 16.0
GLA_CHUNK = 16
EPS = 1e-6
IN_SPLITS = (D_ATT, D_ATT, D_ATT, D_ATT, D_GLA_K, D_GLA_K, D_GLA_V, D_GLA_V, GLA_GATE_RANK)
D_IN = sum(IN_SPLITS)

kernel_name = "hymba_dilated_gla_decode_step"


def rms_norm(x, g):
    xf = x.astype(jnp.float32)
    y = xf * lax.rsqrt(jnp.mean(xf * xf, axis=-1, keepdims=True) + EPS)
    return (y * g.astype(jnp.float32)).astype(x.dtype)


def rotary(x, pos):
    half = ROT_DIM // 2
    inv = 1.0 / (ROPE_THETA ** (jnp.arange(half, dtype=jnp.float32) / half))
    ang = pos.astype(jnp.float32)[:, None] * inv[None, :]
    cos = jnp.cos(ang)[None, :, None, :]
    sin = jnp.sin(ang)[None, :, None, :]
    xr = x[..., :ROT_DIM].astype(jnp.float32)
    x1, x2 = xr[..., :half], xr[..., half:]
    rot = jnp.concatenate([x1 * cos - x2 * sin, x2 * cos + x1 * sin], axis=-1)
    return jnp.concatenate([rot.astype(x.dtype), x[..., ROT_DIM:]], axis=-1)


def dilated_prompt(q, k, v, dil, steps):
    B, S, H, hd = q.shape
    L = S // dil
    nb = -(-L // ATT_BLOCK)
    Lp = nb * ATT_BLOCK

    def blocks(t):
        t = t.astype(jnp.float32).reshape(B, L, dil, H, hd)
        t = jnp.pad(t, ((0, 0), (0, Lp - L), (0, 0), (0, 0), (0, 0)))
        return t.reshape(B, nb, ATT_BLOCK, dil, H, hd)

    def with_prev(t):
        prev = jnp.pad(t, ((0, 0), (1, 0), (0, 0), (0, 0), (0, 0), (0, 0)))[:, :-1]
        return jnp.concatenate([prev, t], axis=2)

    qb = blocks(q)
    kk = with_prev(blocks(k))
    vv = with_prev(blocks(v))
    s = jnp.einsum('bnqrhd,bnkrhd->bnrhqk', qb, kk) * (hd ** -0.5)
    n_idx = jnp.arange(nb)[:, None]
    qpos = n_idx * ATT_BLOCK + jnp.arange(ATT_BLOCK)[None, :]
    kpos = n_idx * ATT_BLOCK - ATT_BLOCK + jnp.arange(2 * ATT_BLOCK)[None, :]
    dist = qpos[:, :, None] - kpos[:, None, :]
    mask = (dist >= 0) & (dist <= steps) & (kpos[:, None, :] >= 0)
    s = jnp.where(mask[None, :, None, None], s, -jnp.inf)
    lse = jax.nn.logsumexp(s, axis=-1)
    p = jnp.exp(s - lse[..., None])
    o = jnp.einsum('bnrhqk,bnkrhd->bnqrhd', p, vv)
    o = o.reshape(B, Lp, dil, H, hd)[:, :L].reshape(B, S, H, hd)
    lse = lse.transpose(0, 1, 4, 2, 3).reshape(B, Lp, dil, H)[:, :L].reshape(B, S, H)
    return o, lse


def dilated_sample(q, kall, vall, dil, steps):
    B, T, H, hd = q.shape
    W = kall.shape[1] - T
    idx = W + jnp.arange(T)[:, None] - jnp.arange(steps + 1)[None, :] * dil
    valid = idx >= 0
    idxc = jnp.maximum(idx, 0)
    kg = jnp.take(kall, idxc, axis=1).astype(jnp.float32)
    vg = jnp.take(vall, idxc, axis=1).astype(jnp.float32)
    s = jnp.einsum('bthd,btjhd->bthj', q.astype(jnp.float32), kg) * (hd ** -0.5)
    s = jnp.where(valid[None, :, None, :], s, -jnp.inf)
    lse = jax.nn.logsumexp(s, axis=-1)
    p = jnp.exp(s - lse[..., None])
    o = jnp.einsum('bthj,btjhd->bthd', p, vg)
    return o, lse


def merge_by_denominator(outs, lses):
    w = jax.nn.softmax(jnp.stack(lses, axis=0), axis=0)
    return jnp.sum(w[..., None] * jnp.stack(outs, axis=0), axis=0)


def gla_chunked(q, k, v, log_a, s0):
    B, T, H, dk = q.shape
    dv = v.shape[-1]
    C = GLA_CHUNK
    N = -(-T // C)
    Tp = N * C

    def prep(t):
        t = jnp.pad(t.astype(jnp.float32), ((0, 0), (0, Tp - T), (0, 0), (0, 0)))
        return t.reshape(B, N, C, H, -1).transpose(1, 0, 3, 2, 4)

    q, k, v, la = prep(q), prep(k), prep(v), prep(log_a)
    b = jnp.cumsum(la, axis=3)
    causal = jnp.tril(jnp.ones((C, C), dtype=bool))
    diff = b[:, :, :, :, None, :] - b[:, :, :, None, :, :]
    decay = jnp.exp(jnp.where(causal[:, :, None], diff, -jnp.inf))
    A = jnp.einsum('nbhtc,nbhsc,nbhtsc->nbhts', q, k, decay)
    o_intra = jnp.einsum('nbhts,nbhsv->nbhtv', A, v)
    b_last = b[:, :, :, -1:, :]
    q_in = q * jnp.exp(b)
    k_in = k * jnp.exp(b_last - b)
    a_last = jnp.exp(b_last[:, :, :, 0, :])

    def step(S, xs):
        qi, ki, vi, ai = xs
        o = jnp.einsum('bhtk,bhkv->bhtv', qi, S)
        S = ai[..., None] * S + jnp.einsum('bhtk,bhtv->bhkv', ki, vi)
        return S, o

    S, o_inter = lax.scan(step, s0.astype(jnp.float32), (q_in, k_in, v, a_last))
    o = (o_intra + o_inter).transpose(1, 0, 3, 2, 4).reshape(B, Tp, H, dv)[:, :T]
    return o, S


def project(x, c, w_mod, b_mod, g_pre, w_in, w_a2, b_a2):
    B, T, _ = x.shape
    mod = jax.nn.silu(c) @ w_mod + b_mod
    shift, scale, gate = jnp.split(mod, 3, axis=-1)
    h = rms_norm(x, g_pre) * (1 + scale[:, None, :]) + shift[:, None, :]
    z = h @ w_in
    offsets = np.cumsum(IN_SPLITS)[:-1].tolist()
    q_a, k_a, v_a, g_a, q_g, k_g, v_g, g_g, a_lr = jnp.split(z, offsets, axis=-1)
    ha = lambda t: t.reshape(B, T, N_HEADS_ATT, HEAD_DIM)
    hk = lambda t: t.reshape(B, T, N_HEADS_GLA, GLA_DK)
    log_a = jax.nn.log_sigmoid((a_lr @ w_a2 + b_a2).astype(jnp.float32)) / GLA_TAU
    return (gate, ha(q_a), ha(k_a), ha(v_a), g_a,
            hk(q_g), hk(k_g), v_g.reshape(B, T, N_HEADS_GLA, GLA_DV), g_g, hk(log_a))


def finish(x, gate, o_att, g_a, o_gla, g_g, g_gla, g_post, w_out):
    B, T, _ = x.shape
    o_gla = rms_norm(o_gla, g_gla)
    att = o_att.reshape(B, T, D_ATT).astype(x.dtype) * jax.nn.silu(g_a)
    gla = o_gla.reshape(B, T, D_GLA_V).astype(x.dtype) * jax.nn.silu(g_g)
    y = rms_norm(jnp.concatenate([att, gla], axis=-1) @ w_out, g_post)
    return x + gate[:, None, :] * y


def setup_inputs(seed: int = 0) -> dict:
    key = jax.random.key(seed)
    ks = jax.random.split(key, 16)
    wbuf = min(WINDOW_MAX, PAST_LEN)
    f32 = jnp.float32
    nrm = lambda k, shape, s: jax.random.normal(k, shape, f32) * s
    return {
        "x_prompt": nrm(ks[0], (BATCH, SEQ, D_MODEL), 1.0),
        "x_sample": nrm(ks[1], (DEC_BATCH, DEC_SEQ, D_MODEL), 1.0),
        "c_prompt": nrm(ks[2], (BATCH, D_MODEL), 1.0),
        "c_sample": nrm(ks[3], (DEC_BATCH, D_MODEL), 1.0),
        "cache_k_win": nrm(ks[4], (DEC_BATCH, wbuf, N_HEADS_ATT, HEAD_DIM), 1.0),
        "cache_v_win": nrm(ks[5], (DEC_BATCH, wbuf, N_HEADS_ATT, HEAD_DIM), 1.0),
        "state_gla": nrm(ks[6], (DEC_BATCH, N_HEADS_GLA, GLA_DK, GLA_DV), 0.5),
        "w_mod": nrm(ks[7], (D_MODEL, 3 * D_MODEL), 0.5 * D_MODEL ** -0.5),
        "b_mod": nrm(ks[8], (3 * D_MODEL,), 0.02),
        "g_pre": 1.0 + nrm(ks[9], (D_MODEL,), 0.02),
        "g_post": 1.0 + nrm(ks[10], (D_MODEL,), 0.02),
        "w_in": nrm(ks[11], (D_MODEL, D_IN), D_MODEL ** -0.5),
        "w_a2": nrm(ks[12], (GLA_GATE_RANK, D_GLA_K), GLA_GATE_RANK ** -0.5),
        "b_a2": nrm(ks[13], (D_GLA_K,), 0.1),
        "g_gla": 1.0 + nrm(ks[14], (GLA_DV,), 0.02),
        "w_out": nrm(ks[15], (D_MIX, D_MODEL), D_MIX ** -0.5),
    }


def reference(x_prompt, x_sample, c_prompt, c_sample, cache_k_win, cache_v_win, state_gla,
              w_mod, b_mod, g_pre, g_post, w_in, w_a2, b_a2, g_gla, w_out):
    S = x_prompt.shape[1]
    pos_p = jnp.arange(S)
    y_prompt = x_prompt
    y_sample = x_sample
    for _layer in range(DEPTH):
        gate, q_a, k_a, v_a, g_a, q_g, k_g, v_g, g_g, log_a = project(
            y_prompt, c_prompt, w_mod, b_mod, g_pre, w_in, w_a2, b_a2)
        q_a = rotary(q_a, pos_p)
        k_a = rotary(k_a, pos_p)
        res = [dilated_prompt(q_a, k_a, v_a, d, w // d) for (w, d) in DILATED_PATTERNS]
        o_att = merge_by_denominator([r[0] for r in res], [r[1] for r in res])
        s0 = jnp.zeros((y_prompt.shape[0], N_HEADS_GLA, GLA_DK, GLA_DV), jnp.float32)
        o_gla, gla_p = gla_chunked(q_g * (GLA_DK ** -0.5), k_g, v_g, log_a, s0)
        y_prompt = finish(y_prompt, gate, o_att, g_a, o_gla, g_g, g_gla, g_post, w_out)
        wp = min(WINDOW_MAX, S)
        k_win_prompt = k_a[:, S - wp:]
        v_win_prompt = v_a[:, S - wp:]
        gla_prompt = gla_p.astype(x_prompt.dtype)

        T = x_sample.shape[1]
        W = cache_k_win.shape[1]
        pos_s = PAST_LEN + jnp.arange(T)
        gate, q_a, k_a, v_a, g_a, q_g, k_g, v_g, g_g, log_a = project(
            y_sample, c_sample, w_mod, b_mod, g_pre, w_in, w_a2, b_a2)
        q_a = rotary(q_a, pos_s)
        k_a = rotary(k_a, pos_s)
        kall = jnp.concatenate([cache_k_win, k_a.astype(cache_k_win.dtype)], axis=1)
        vall = jnp.concatenate([cache_v_win, v_a.astype(cache_v_win.dtype)], axis=1)
        res = [dilated_sample(q_a, kall, vall, d, w // d) for (w, d) in DILATED_PATTERNS]
        o_att = merge_by_denominator([r[0] for r in res], [r[1] for r in res])
        o_gla, gla_s = gla_chunked(q_g * (GLA_DK ** -0.5), k_g, v_g, log_a, state_gla)
        y_sample = finish(y_sample, gate, o_att, g_a, o_gla, g_g, g_gla, g_post, w_out)
        k_win_sample = kall[:, T:T + W]
        v_win_sample = vall[:, T:T + W]
        gla_sample = gla_s.astype(state_gla.dtype)
    return (y_prompt, y_sample, k_win_prompt, v_win_prompt, gla_prompt,
            k_win_sample, v_win_sample, gla_sample)
```

```python
import functools
import math

import numpy as np
import jax
import jax.numpy as jnp
from jax import lax
from jax.experimental import pallas as pl
from jax.experimental.pallas import tpu as pltpu

F32 = jnp.float32
BF16 = jnp.bfloat16

HEAD_DIM = 64
ROT_DIM = HEAD_DIM // 4
ROPE_THETA = 500000.0
DILATIONS = (1, 4, 16)
STEPS = 128
ATT_BLOCK = 128
PAST_LEN = 8192
GLA_DK = 64
GLA_DV = 128
GLA_GATE_RANK = 16
GLA_TAU = 16.0
EPS = 1e-6
NEG = -1e30

LANES = 128
SUBLANES = 8
VMEM_LIMIT = 48 * 1024 * 1024


def _cparams(*sem):
    return pltpu.CompilerParams(dimension_semantics=sem, vmem_limit_bytes=VMEM_LIMIT)


def _nt(a, b):
    return lax.dot_general(a, b, (((1,), (1,)), ((), ())), preferred_element_type=F32)


def _split3(x):
    hi = x.astype(BF16)
    r1 = x - hi.astype(F32)
    mid = r1.astype(BF16)
    lo = (r1 - mid.astype(F32)).astype(BF16)
    return hi, mid, lo


def _mod_kernel(c_ref, w_ref, b_ref, o_ref):
    c = c_ref[...]
    a = c / (1.0 + jnp.exp(-c))
    o_ref[...] = jnp.dot(a, w_ref[...], precision=lax.Precision.HIGHEST,
                         preferred_element_type=F32) + b_ref[...]


def _mod_call(c, w_mod, b_mod):
    n, d = c.shape
    d3 = w_mod.shape[1]
    tn = 512
    return pl.pallas_call(
        _mod_kernel,
        grid=(d3 // tn,),
        in_specs=[pl.BlockSpec((n, d), lambda j: (0, 0)),
                  pl.BlockSpec((d, tn), lambda j: (0, j)),
                  pl.BlockSpec((1, tn), lambda j: (0, j))],
        out_specs=pl.BlockSpec((n, tn), lambda j: (0, j)),
        out_shape=jax.ShapeDtypeStruct((n, d3), F32),
        compiler_params=_cparams("arbitrary"),
        name="mod",
    )(c, w_mod, b_mod.reshape(1, d3))


def _rope_tables(pos):
    half = ROT_DIM // 2
    inv = 1.0 / (ROPE_THETA ** (jnp.arange(half, dtype=F32) / half))
    ang = pos.astype(F32)[:, None] * inv[None, :]
    cos, sin = jnp.cos(ang), jnp.sin(ang)
    n = pos.shape[0]
    ones = jnp.ones((n, HEAD_DIM - ROT_DIM), F32)
    zeros_h = jnp.zeros((n, half), F32)
    zeros_r = jnp.zeros((n, HEAD_DIM - ROT_DIM), F32)
    c = jnp.concatenate([cos, cos, ones], axis=1)
    s1 = jnp.concatenate([-sin, zeros_h, zeros_r], axis=1)
    s2 = jnp.concatenate([zeros_h, sin, zeros_r], axis=1)
    rep = LANES // HEAD_DIM
    return tuple(jnp.tile(t, (1, rep)) for t in (c, s1, s2))


def _proj_kernel(x_ref, sh_ref, sc_ref, gpre_ref, w_ref, wa2_ref, ba2_ref,
                 rc_ref, rs1_ref, rs2_ref,
                 q_ref, k_ref, v_ref, ga_ref, qg_ref, kg_ref, la_ref, vg_ref, gg_ref,
                 *, d_att, d_gk, d_gv):
    x = x_ref[0]
    ms = jnp.mean(x * x, axis=-1, keepdims=True)
    h = x * lax.rsqrt(ms + EPS) * gpre_ref[...]
    h = h * (1.0 + sc_ref[0]) + sh_ref[0]
    hb = h.astype(BF16)

    def seg(a, n):
        return jnp.dot(hb, w_ref[:, a:a + n], preferred_element_type=F32)

    reps = d_att // LANES
    rc = jnp.concatenate([rc_ref[...]] * reps, axis=1)
    rs1 = jnp.concatenate([rs1_ref[...]] * reps, axis=1)
    rs2 = jnp.concatenate([rs2_ref[...]] * reps, axis=1)
    half = ROT_DIM // 2

    def rot(t):
        up = pltpu.roll(t, d_att - half, axis=1)
        dn = pltpu.roll(t, half, axis=1)
        return t * rc + up * rs1 + dn * rs2

    def silu(t):
        return t / (1.0 + jnp.exp(-t))

    o = 0
    q_ref[0] = rot(seg(o, d_att)) * (HEAD_DIM ** -0.5); o += d_att
    k_ref[0] = rot(seg(o, d_att)); o += d_att
    v_ref[0] = seg(o, d_att); o += d_att
    ga_ref[0] = silu(seg(o, d_att)); o += d_att
    qg_ref[0] = seg(o, d_gk) * (GLA_DK ** -0.5); o += d_gk
    kg_ref[0] = seg(o, d_gk); o += d_gk
    vg_ref[0] = seg(o, d_gv); o += d_gv
    gg_ref[0] = silu(seg(o, d_gv)); o += d_gv
    alr = seg(o, LANES)
    a1, a2, a3 = _split3(alr)
    w1, w2, w3 = _split3(wa2_ref[...])
    dot = functools.partial(jnp.dot, preferred_element_type=F32)
    z = (dot(a1, w1) + dot(a1, w2) + dot(a2, w1)
         + dot(a1, w3) + dot(a2, w2) + dot(a3, w1)) + ba2_ref[...]
    logsig = jnp.minimum(z, 0.0) - jnp.log1p(jnp.exp(-jnp.abs(z)))
    la_ref[0] = logsig * (1.0 / GLA_TAU)


def _proj_call(x3, mod3, gpre, w_pad, wa2_pad, ba2, tables, *, d_att, d_gk, d_gv, tm):
    g, r, d = x3.shape
    rm = mod3.shape[1]
    per_row = rm != 1
    grid = (g, r // tm)
    if per_row:
        mod_spec = lambda c: pl.BlockSpec((1, tm, d), lambda i, j: (i, j, c))
    else:
        mod_spec = lambda c: pl.BlockSpec((1, 1, d), lambda i, j: (i, 0, c))
    tab_rows = tables[0].shape[0]
    if tab_rows == tm:
        tab_spec = pl.BlockSpec((tm, LANES), lambda i, j: (0, 0))
    else:
        tab_spec = pl.BlockSpec((tm, LANES), lambda i, j: (j, 0))
    full = lambda a: pl.BlockSpec(a.shape, lambda i, j: (0,) * a.ndim)
    widths = (d_att, d_att, d_att, d_att, d_gk, d_gk, d_gk, d_gv, d_gv)
    out_specs = [pl.BlockSpec((1, tm, w), lambda i, j: (i, j, 0)) for w in widths]
    out_shape = [jax.ShapeDtypeStruct((g, r, w), F32) for w in widths]
    kern = functools.partial(_proj_kernel, d_att=d_att, d_gk=d_gk, d_gv=d_gv)
    return pl.pallas_call(
        kern,
        grid=grid,
        in_specs=[pl.BlockSpec((1, tm, d), lambda i, j: (i, j, 0)),
                  mod_spec(0), mod_spec(1), full(gpre), full(w_pad), full(wa2_pad), full(ba2),
                  tab_spec, tab_spec, tab_spec],
        out_specs=out_specs,
        out_shape=out_shape,
        compiler_params=_cparams("parallel", "arbitrary"),
        name="proj",
    )(x3, mod3, mod3, gpre, w_pad, wa2_pad, ba2, *tables)


def _att_prompt_kernel(q_ref, kp_ref, kc_ref, vp_ref, vc_ref, o_ref, lse_ref, *, d_att):
    n = pl.program_id(2)
    blk = ATT_BLOCK
    row = lax.broadcasted_iota(jnp.int32, (blk, 2 * blk), 0)
    col = lax.broadcasted_iota(jnp.int32, (blk, 2 * blk), 1)
    dist = row + blk - col
    prev_bias = jnp.where(n > 0, 0.0, NEG)
    band = (dist >= 0) & (dist <= STEPS)
    bias = jnp.where(band, jnp.where(col >= blk, 0.0, prev_bias), NEG)
    lane = lax.broadcasted_iota(jnp.int32, (blk, LANES), 1)
    lo = lane < HEAD_DIM
    for j in range(d_att // LANES):
        sl = slice(j * LANES, (j + 1) * LANES)
        qp = q_ref[0, :, sl]
        kcat = jnp.concatenate([kp_ref[0, :, sl], kc_ref[0, :, sl]], axis=0).astype(BF16)
        vcat = jnp.concatenate([vp_ref[0, :, sl], vc_ref[0, :, sl]], axis=0).astype(BF16)
        outs, lses = [], []
        for sel in (lo, ~lo):
            qm = jnp.where(sel, qp, 0.0).astype(BF16)
            s = _nt(qm, kcat) + bias
            m = jnp.max(s, axis=-1, keepdims=True)
            p = jnp.exp(s - m)
            l = jnp.sum(p, axis=-1, keepdims=True)
            o = jnp.dot(p.astype(BF16), vcat, preferred_element_type=F32)
            outs.append(o / l)
            lses.append(m + jnp.log(l))
        o_ref[0, :, sl] = jnp.where(lo, outs[0], outs[1])
        lse_ref[0, :, sl] = jnp.where(lo, lses[0], lses[1])


def _att_prompt_call(q, k, v, dil):
    b, s, d_att = q.shape
    l = s // dil
    nb = l // ATT_BLOCK
    view = lambda t: t.reshape(b, l, dil * d_att)
    cur = pl.BlockSpec((1, ATT_BLOCK, d_att), lambda i, r, n: (i, n, r))
    prev = pl.BlockSpec((1, ATT_BLOCK, d_att), lambda i, r, n: (i, jnp.maximum(n - 1, 0), r))
    kern = functools.partial(_att_prompt_kernel, d_att=d_att)
    o, lse = pl.pallas_call(
        kern,
        grid=(b, dil, nb),
        in_specs=[cur, prev, cur, prev, cur],
        out_specs=[cur, cur],
        out_shape=[jax.ShapeDtypeStruct((b, l, dil * d_att), F32)] * 2,
        compiler_params=_cparams("parallel", "parallel", "arbitrary"),
        name=f"att_prompt_d{dil}",
    )(view(q), view(k), view(k), view(v), view(v))
    return o.reshape(b, s, d_att), lse.reshape(b, s, d_att)


def _gla_consts(blk):
    t = np.arange(LANES)
    same = (t[:, None] // blk) == (t[None, :] // blk)
    tri = ((t[None, :] <= t[:, None]) & same).astype(np.int64)
    mats, lvl, right_rows = [tri], [], []
    h = blk // 2
    while h >= 1:
        pair = t // (2 * h)
        right = (t // h) % 2 == 1
        mid = pair * 2 * h + h - 1
        upto_mid = ((t[None, :] <= mid[:, None]) & same).astype(np.int64)
        mats.append(tri - upto_mid)
        lvl.append(pair[:, None] == pair[None, :])
        right_rows.append(np.broadcast_to(right[:, None], (LANES, LANES)))
        h //= 2
    lvl.append(np.eye(LANES, dtype=bool))
    mats.append(same.astype(np.int64))
    cmat = jnp.asarray(np.concatenate(mats, 0), BF16)
    lvl = jnp.asarray(np.concatenate(lvl, 0), F32)
    rmask = jnp.asarray(np.concatenate(right_rows, 0), F32)
    return cmat, lvl, rmask, len(right_rows)


def _gla_tile(q, k, la, cmat_ref, lvl_ref, rmask_ref, n_lvl):
    t = LANES
    hi, mid, lo3 = _split3(la)
    cm = cmat_ref[...]
    dot = functools.partial(jnp.dot, preferred_element_type=F32)
    g = dot(cm, hi) + dot(cm, mid) + dot(cm, lo3)
    b = g[0:t]
    btot = g[(n_lvl + 1) * t:(n_lvl + 2) * t]
    lane = lax.broadcasted_iota(jnp.int32, (t, LANES), 1)
    lo = lane < GLA_DK
    pairs = []
    for j in range(q.shape[1] // LANES):
        sl = slice(j * LANES, (j + 1) * LANES)
        qp, kp, bp = q[:, sl], k[:, sl], b[:, sl]
        a0 = jnp.zeros((t, t), F32)
        a1 = jnp.zeros((t, t), F32)
        for lv in range(n_lvl + 1):
            m = lvl_ref[lv * t:(lv + 1) * t, :]
            if lv < n_lvl:
                dl = g[(lv + 1) * t:(lv + 2) * t, sl]
                e = jnp.exp(-jnp.abs(dl))
                rm = rmask_ref[lv * t:(lv + 1) * t, :]
                x = qp * e * rm
                y = (kp * e * (1.0 - rm)).astype(BF16)
            else:
                x = qp
                y = kp.astype(BF16)
            x0 = jnp.where(lo, x, 0.0).astype(BF16)
            x1 = jnp.where(lo, 0.0, x).astype(BF16)
            a0 = a0 + m * _nt(x0, y)
            a1 = a1 + m * _nt(x1, y)
        q_in = qp * jnp.exp(bp)
        k_in = kp * jnp.exp(btot[:, sl] - bp)
        pairs.append((a0, a1, q_in, k_in, btot[:, sl], lo))
    return pairs


def _gla_prompt_kernel(q_ref, k_ref, la_ref, v_ref, cmat_ref, lvl_ref, rmask_ref,
                       o_ref, s_out_ref, s_ref, *, n_lvl):
    n = pl.program_id(1)

    @pl.when(n == 0)
    def _():
        s_ref[...] = jnp.zeros_like(s_ref)

    pairs = _gla_tile(q_ref[0], k_ref[0], la_ref[0], cmat_ref, lvl_ref, rmask_ref, n_lvl)
    t = LANES
    row = lax.broadcasted_iota(jnp.int32, (t, t), 0)
    top = row < GLA_DK
    for j, (a0, a1, q_in, k_in, btot, lo) in enumerate(pairs):
        s_prev = s_ref[j]
        sb = s_prev.astype(BF16)
        kt = k_in.T.astype(BF16)
        us = []
        for hh, (a, sel) in enumerate(((a0, lo), (a1, ~lo))):
            vs = slice((2 * j + hh) * GLA_DV, (2 * j + hh + 1) * GLA_DV)
            vb = v_ref[0, :, vs].astype(BF16)
            qm = jnp.where(sel, q_in, 0.0).astype(BF16)
            o = (jnp.dot(a.astype(BF16), vb, preferred_element_type=F32)
                 + jnp.dot(qm, sb, preferred_element_type=F32))
            o_ref[0, :, vs] = o
            us.append(jnp.dot(kt, vb, preferred_element_type=F32))
        a_col = jnp.exp(btot.T)
        s_ref[j] = a_col * s_prev + jnp.where(top, us[0], us[1])

    @pl.when(n == pl.num_programs(1) - 1)
    def _():
        s_out_ref[0] = s_ref[...]


def _gla_prompt_call(qg, kg, la, vg):
    b, s, d_gk = qg.shape
    d_gv = vg.shape[2]
    n_pair = d_gk // LANES
    cmat, lvl, rmask, n_lvl = _gla_consts(LANES)
    t = LANES
    tok = lambda w: pl.BlockSpec((1, t, w), lambda i, n: (i, n, 0))
    full = lambda a: pl.BlockSpec(a.shape, lambda i, n: (0,) * a.ndim)
    kern = functools.partial(_gla_prompt_kernel, n_lvl=n_lvl)
    o, s_out = pl.pallas_call(
        kern,
        grid=(b, s // t),
        in_specs=[tok(d_gk), tok(d_gk), tok(d_gk), tok(d_gv), full(cmat), full(lvl), full(rmask)],
        out_specs=[tok(d_gv), pl.BlockSpec((1, n_pair, t, GLA_DV), lambda i, n: (i, 0, 0, 0))],
        out_shape=[jax.ShapeDtypeStruct((b, s, d_gv), F32),
                   jax.ShapeDtypeStruct((b, n_pair, t, GLA_DV), F32)],
        scratch_shapes=[pltpu.VMEM((n_pair, t, GLA_DV), F32)],
        compiler_params=_cparams("parallel", "arbitrary"),
        name="gla_prompt",
    )(qg, kg, la, vg, cmat, lvl, rmask)
    return o, s_out.reshape(b, 2 * n_pair, GLA_DK, GLA_DV)


def _gla_sample_kernel(q_ref, k_ref, la_ref, v_ref, s_in_ref, cmat_ref, lvl_ref, rmask_ref,
                       o_ref, s_out_ref, *, n_lvl, t_dec):
    pairs = _gla_tile(q_ref[...], k_ref[...], la_ref[...], cmat_ref, lvl_ref, rmask_ref, n_lvl)
    t = LANES
    nb = t // t_dec
    row = lax.broadcasted_iota(jnp.int32, (t, t), 0)
    col = lax.broadcasted_iota(jnp.int32, (t, t), 1)
    top = row < GLA_DK
    for j, (a0, a1, q_in, k_in, btot, lo) in enumerate(pairs):
        s_prev = s_in_ref[:, j]
        sb = s_prev.astype(BF16)
        kt = k_in.T
        bt = btot.T
        vbs = []
        for hh, (a, sel) in enumerate(((a0, lo), (a1, ~lo))):
            vs = slice((2 * j + hh) * GLA_DV, (2 * j + hh + 1) * GLA_DV)
            vb = v_ref[:, vs].astype(BF16)
            vbs.append(vb)
            qm = jnp.where(sel, q_in, 0.0)
            o_inter = jnp.concatenate(
                [jnp.dot(qm[i * t_dec:(i + 1) * t_dec].astype(BF16), sb[i],
                         preferred_element_type=F32) for i in range(nb)], axis=0)
            o_ref[:, vs] = jnp.dot(a.astype(BF16), vb, preferred_element_type=F32) + o_inter
        for i in range(nb):
            mine = (col >= i * t_dec) & (col < (i + 1) * t_dec)
            kti = (kt * mine.astype(F32)).astype(BF16)
            u0 = jnp.dot(kti, vbs[0], preferred_element_type=F32)
            u1 = jnp.dot(kti, vbs[1], preferred_element_type=F32)
            a_col = jnp.exp(jnp.broadcast_to(bt[:, i * t_dec:i * t_dec + 1], (t, GLA_DV)))
            s_out_ref[i, j] = a_col * s_prev[i] + jnp.where(top, u0, u1)


def _gla_sample_call(qg, kg, la, vg, state, t_dec):
    rows, d_gk = qg.shape
    d_gv = vg.shape[1]
    n_pair = d_gk // LANES
    nseq = rows // t_dec
    t = LANES
    nb = t // t_dec
    cmat, lvl, rmask, n_lvl = _gla_consts(t_dec)
    s_in = state.reshape(nseq, n_pair, 2 * GLA_DK, GLA_DV)
    tok = lambda w: pl.BlockSpec((t, w), lambda i: (i, 0))
    full = lambda a: pl.BlockSpec(a.shape, lambda i: (0,) * a.ndim)
    st = pl.BlockSpec((nb, n_pair, t, GLA_DV), lambda i: (i, 0, 0, 0))
    kern = functools.partial(_gla_sample_kernel, n_lvl=n_lvl, t_dec=t_dec)
    o, s_out = pl.pallas_call(
        kern,
        grid=(rows // t,),
        in_specs=[tok(d_gk), tok(d_gk), tok(d_gk), tok(d_gv), st, full(cmat), full(lvl), full(rmask)],
        out_specs=[tok(d_gv), st],
        out_shape=[jax.ShapeDtypeStruct((rows, d_gv), F32),
                   jax.ShapeDtypeStruct(s_in.shape, F32)],
        compiler_params=_cparams("parallel"),
        name="gla_sample",
    )(qg, kg, la, vg, s_in, cmat, lvl, rmask)
    return o, s_out.reshape(state.shape)


def _sample_att_tables(w, t_dec, d_att):
    n_heads = d_att // HEAD_DIM
    wpad = -(-(w + t_dec) // LANES) * LANES
    idx = np.arange(wpad)[None, :]
    tq = np.arange(t_dec)[:, None]
    dist = w + tq - idx
    mult = np.zeros((t_dec, wpad), np.int64)
    for d in DILATIONS:
        mult += (dist >= 0) & (dist % d == 0) & (dist // d <= STEPS) & (idx < w + t_dec)
    logw = np.where(mult > 0, np.log(np.maximum(mult, 1)), NEG).astype(np.float32)
    logw = np.tile(logw, (n_heads, 1))
    head_of_row = np.repeat(np.arange(n_heads), t_dec)[:, None]
    head_of_lane = (np.arange(d_att) // HEAD_DIM)[None, :]
    hmask = (head_of_row == head_of_lane).astype(np.float32)
    return jnp.asarray(logw), jnp.asarray(hmask), wpad


def _att_sample_kernel(q_ref, kn_ref, vn_ref, ck_ref, cv_ref, logw_ref, hm_ref,
                       o_ref, ko_ref, vo_ref, kb_ref, vb_ref, *, w, t_dec, n_heads):
    chunk = LANES
    n_chunks = w // chunk

    def body(c, carry):
        r0 = pl.multiple_of(c * chunk, chunk)
        kc = ck_ref[0, pl.ds(r0, chunk), :]
        vc = cv_ref[0, pl.ds(r0, chunk), :]
        kb_ref[pl.ds(r0, chunk), :] = kc.astype(BF16)
        vb_ref[pl.ds(r0, chunk), :] = vc.astype(BF16)
        return carry

    lax.fori_loop(0, n_chunks, body, 0)
    ko_ref[0, 0:w - t_dec, :] = ck_ref[0, t_dec:w, :]
    vo_ref[0, 0:w - t_dec, :] = cv_ref[0, t_dec:w, :]
    kn = kn_ref[...]
    vn = vn_ref[...]
    ko_ref[0, w - t_dec:w, :] = kn
    vo_ref[0, w - t_dec:w, :] = vn
    pad = jnp.zeros((kb_ref.shape[0] - w - t_dec, kn.shape[1]), F32)
    kb_ref[w:, :] = jnp.concatenate([kn, pad], axis=0).astype(BF16)
    vb_ref[w:, :] = jnp.concatenate([vn, pad], axis=0).astype(BF16)

    hm = hm_ref[...]
    qbd = (jnp.concatenate([q_ref[...]] * n_heads, axis=0) * hm).astype(BF16)
    s = _nt(qbd, kb_ref[...]) + logw_ref[...]
    m = jnp.max(s, axis=-1, keepdims=True)
    p = jnp.exp(s - m)
    l = jnp.sum(p, axis=-1, keepdims=True)
    res = jnp.dot(p.astype(BF16), vb_ref[...], preferred_element_type=F32) / l * hm
    acc = res[0:t_dec]
    for h in range(1, n_heads):
        acc = acc + res[h * t_dec:(h + 1) * t_dec]
    o_ref[...] = acc


def _att_sample_call(q, kn, vn, cache_k, cache_v, t_dec):
    nseq, w, n_heads, hd = cache_k.shape
    d_att = n_heads * hd
    ck = cache_k.reshape(nseq, w, d_att)
    cv = cache_v.reshape(nseq, w, d_att)
    logw, hmask, wpad = _sample_att_tables(w, t_dec, d_att)
    tok = pl.BlockSpec((t_dec, d_att), lambda i: (i, 0))
    win = pl.BlockSpec((1, w, d_att), lambda i: (i, 0, 0))
    full = lambda a: pl.BlockSpec(a.shape, lambda i: (0,) * a.ndim)
    kern = functools.partial(_att_sample_kernel, w=w, t_dec=t_dec, n_heads=n_heads)
    o, ko, vo = pl.pallas_call(
        kern,
        grid=(nseq,),
        in_specs=[tok, tok, tok, win, win, full(logw), full(hmask)],
        out_specs=[tok, win, win],
        out_shape=[jax.ShapeDtypeStruct((nseq * t_dec, d_att), F32),
                   jax.ShapeDtypeStruct((nseq, w, d_att), F32),
                   jax.ShapeDtypeStruct((nseq, w, d_att), F32)],
        scratch_shapes=[pltpu.VMEM((wpad, d_att), BF16), pltpu.VMEM((wpad, d_att), BF16)],
        compiler_params=_cparams("parallel"),
        name="att_sample",
    )(q, kn, vn, ck, cv, logw, hmask)
    return o, ko.reshape(cache_k.shape), vo.reshape(cache_v.shape)


def _finish_kernel(*refs, n_att, d_att, d_gv):
    x_ref, gate_ref = refs[0], refs[1]
    n_in = 1 if n_att == 1 else 2 * n_att
    att_refs = refs[2:2 + n_in]
    ga_ref, og_ref, gg_ref, ggla_ref, gpost_ref, wo_ref, y_ref = refs[2 + n_in:]
    if n_att == 1:
        att = att_refs[0][0]
    else:
        lses = [att_refs[2 * i + 1][0] for i in range(n_att)]
        m = functools.reduce(jnp.maximum, lses)
        es = [jnp.exp(l - m) for l in lses]
        num = sum(e * att_refs[2 * i][0] for i, e in enumerate(es))
        att = num / sum(es)
    att = att * ga_ref[0]
    og = og_ref[0]
    gg = gg_ref[0]
    y = jnp.dot(att.astype(BF16), wo_ref[0:d_att, :], preferred_element_type=F32)
    for h in range(d_gv // GLA_DV):
        sl = slice(h * GLA_DV, (h + 1) * GLA_DV)
        oh = og[:, sl]
        nh = oh * lax.rsqrt(jnp.mean(oh * oh, axis=-1, keepdims=True) + EPS) * ggla_ref[...]
        gl = (nh * gg[:, sl]).astype(BF16)
        y = y + jnp.dot(gl, wo_ref[d_att + h * GLA_DV:d_att + (h + 1) * GLA_DV, :],
                        preferred_element_type=F32)
    yn = y * lax.rsqrt(jnp.mean(y * y, axis=-1, keepdims=True) + EPS) * gpost_ref[...]
    y_ref[0] = x_ref[0] + gate_ref[0] * yn


def _finish_call(x3, mod3, att_list, ga, og, gg, g_gla, g_post, w_out, *, tm):
    g, r, d = x3.shape
    d_att = ga.shape[2]
    d_gv = og.shape[2]
    n_att = len(att_list)
    per_row = mod3.shape[1] != 1
    if per_row:
        gate_spec = pl.BlockSpec((1, tm, d), lambda i, j: (i, j, 2))
    else:
        gate_spec = pl.BlockSpec((1, 1, d), lambda i, j: (i, 0, 2))
    tok = lambda w: pl.BlockSpec((1, tm, w), lambda i, j: (i, j, 0))
    full = lambda a: pl.BlockSpec(a.shape, lambda i, j: (0,) * a.ndim)
    att_flat = []
    for o, lse in att_list:
        att_flat += [o] if n_att == 1 else [o, lse]
    kern = functools.partial(_finish_kernel, n_att=n_att, d_att=d_att, d_gv=d_gv)
    return pl.pallas_call(
        kern,
        grid=(g, r // tm),
        in_specs=[tok(d), gate_spec] + [tok(d_att)] * len(att_flat)
                 + [tok(d_att), tok(d_gv), tok(d_gv), full(g_gla), full(g_post), full(w_out)],
        out_specs=tok(d),
        out_shape=jax.ShapeDtypeStruct((g, r, d), F32),
        compiler_params=_cparams("parallel", "arbitrary"),
        name="finish",
    )(x3, mod3, *att_flat, ga, og, gg, g_gla, g_post, w_out)


def kernel(x_prompt, x_sample, c_prompt, c_sample, cache_k_win, cache_v_win, state_gla,
           w_mod, b_mod, g_pre, g_post, w_in, w_a2, b_a2, g_gla, w_out):
    b, s, d = x_prompt.shape
    nseq, t_dec, _ = x_sample.shape
    w = cache_k_win.shape[1]
    d_att = cache_k_win.shape[2] * cache_k_win.shape[3]
    d_gk = w_a2.shape[1]
    d_gv = state_gla.shape[1] * state_gla.shape[3]
    d_main = 4 * d_att + 2 * d_gk + 2 * d_gv
    assert w_in.shape[1] == d_main + GLA_GATE_RANK and cache_k_win.shape[3] == HEAD_DIM
    assert state_gla.shape[2] == GLA_DK and state_gla.shape[3] == GLA_DV
    assert w >= STEPS * max(DILATIONS) and s % (ATT_BLOCK * max(DILATIONS)) == 0 and s >= w

    w_pad = jnp.pad(w_in, ((0, 0), (0, LANES - GLA_GATE_RANK))).astype(BF16)
    wa2_pad = jnp.pad(w_a2, ((0, LANES - GLA_GATE_RANK), (0, 0)))
    wo = w_out.astype(BF16)
    gpre = g_pre.reshape(1, d)
    gpost = g_post.reshape(1, d)
    ggla = g_gla.reshape(1, GLA_DV)
    ba2 = b_a2.reshape(1, d_gk)

    n_c = b + nseq
    n_c_pad = -(-n_c // SUBLANES) * SUBLANES
    c_all = jnp.pad(jnp.concatenate([c_prompt, c_sample], axis=0), ((0, n_c_pad - n_c), (0, 0)))
    mod = _mod_call(c_all, w_mod, b_mod)
    mod_p = mod[:b].reshape(b, 1, 3 * d)

    proj = functools.partial(_proj_call, d_att=d_att, d_gk=d_gk, d_gv=d_gv)

    tm = 512
    tabs_p = _rope_tables(jnp.arange(s))
    q, k, v, ga, qg, kg, la, vg, gg = proj(x_prompt, mod_p, gpre, w_pad, wa2_pad, ba2, tabs_p, tm=tm)
    att = [_att_prompt_call(q, k, v, dil) for dil in DILATIONS]
    og, gla_p = _gla_prompt_call(qg, kg, la, vg)
    y_prompt = _finish_call(x_prompt, mod_p, att, ga, og, gg, ggla, gpost, wo, tm=tm)
    n_heads = d_att // HEAD_DIM
    k_win_prompt = k[:, s - w:].reshape(b, w, n_heads, HEAD_DIM)
    v_win_prompt = v[:, s - w:].reshape(b, w, n_heads, HEAD_DIM)

    rows = nseq * t_dec
    tms = min(512, rows)
    xs = x_sample.reshape(rows // tms, tms, d)
    mod_s = jnp.repeat(mod[b:b + nseq], t_dec, axis=0).reshape(rows // tms, tms, 3 * d)
    tabs_s = _rope_tables(jnp.tile(PAST_LEN + jnp.arange(t_dec), tms // t_dec))
    outs = proj(xs, mod_s, gpre, w_pad, wa2_pad, ba2, tabs_s, tm=tms)
    q, k, v, ga, qg, kg, la, vg, gg = [o.reshape(rows, o.shape[2]) for o in outs]
    o_att, k_win_sample, v_win_sample = _att_sample_call(q, k, v, cache_k_win, cache_v_win, t_dec)
    og, gla_s = _gla_sample_call(qg, kg, la, vg, state_gla, t_dec)
    r3 = lambda a: a.reshape(rows // tms, tms, a.shape[1])
    y_sample = _finish_call(xs, mod_s, [(r3(o_att), None)], r3(ga), r3(og), r3(gg),
                            ggla, gpost, wo, tm=tms).reshape(x_sample.shape)

    return (y_prompt, y_sample, k_win_prompt, v_win_prompt, gla_p,
            k_win_sample, v_win_sample, gla_s)
```

```python
import functools
import math

import numpy as np
import jax
import jax.numpy as jnp
from jax import lax
from jax.experimental import pallas as pl
from jax.experimental.pallas import tpu as pltpu

F32 = jnp.float32
BF16 = jnp.bfloat16

HEAD_DIM = 64
ROT_DIM = HEAD_DIM // 4
ROPE_THETA = 500000.0
DILATIONS = (1, 4, 16)
STEPS = 128
ATT_BLOCK = 128
PAST_LEN = 8192
GLA_DK = 64
GLA_DV = 128
GLA_GATE_RANK = 16
GLA_TAU = 16.0
EPS = 1e-6
NEG = -1e30

LANES = 128
SUBLANES = 8
VMEM_LIMIT = 48 * 1024 * 1024


def _cparams(*sem):
    return pltpu.CompilerParams(dimension_semantics=sem, vmem_limit_bytes=VMEM_LIMIT)


def _nt(a, b):
    return lax.dot_general(a, b, (((1,), (1,)), ((), ())), preferred_element_type=F32)


def _split3(x):
    hi = x.astype(BF16)
    r1 = x - hi.astype(F32)
    mid = r1.astype(BF16)
    lo = (r1 - mid.astype(F32)).astype(BF16)
    return hi, mid, lo


def _mod_kernel(c_ref, w_ref, b_ref, o_ref):
    c = c_ref[...]
    a = c / (1.0 + jnp.exp(-c))
    o_ref[...] = jnp.dot(a, w_ref[...], precision=lax.Precision.HIGHEST,
                         preferred_element_type=F32) + b_ref[...]


def _mod_call(c, w_mod, b_mod):
    n, d = c.shape
    d3 = w_mod.shape[1]
    tn = 512
    return pl.pallas_call(
        _mod_kernel,
        grid=(d3 // tn,),
        in_specs=[pl.BlockSpec((n, d), lambda j: (0, 0)),
                  pl.BlockSpec((d, tn), lambda j: (0, j)),
                  pl.BlockSpec((1, tn), lambda j: (0, j))],
        out_specs=pl.BlockSpec((n, tn), lambda j: (0, j)),
        out_shape=jax.ShapeDtypeStruct((n, d3), F32),
        compiler_params=_cparams("arbitrary"),
        name="mod",
    )(c, w_mod, b_mod.reshape(1, d3))


def _rope_tables(pos):
    half = ROT_DIM // 2
    inv = 1.0 / (ROPE_THETA ** (jnp.arange(half, dtype=F32) / half))
    ang = pos.astype(F32)[:, None] * inv[None, :]
    cos, sin = jnp.cos(ang), jnp.sin(ang)
    n = pos.shape[0]
    ones = jnp.ones((n, HEAD_DIM - ROT_DIM), F32)
    zeros_h = jnp.zeros((n, half), F32)
    zeros_r = jnp.zeros((n, HEAD_DIM - ROT_DIM), F32)
    c = jnp.concatenate([cos, cos, ones], axis=1)
    s1 = jnp.concatenate([-sin, zeros_h, zeros_r], axis=1)
    s2 = jnp.concatenate([zeros_h, sin, zeros_r], axis=1)
    rep = LANES // HEAD_DIM
    return tuple(jnp.tile(t, (1, rep)) for t in (c, s1, s2))


def _proj_kernel(x_ref, sh_ref, sc_ref, gpre_ref, w_ref, wa2_ref, ba2_ref,
                 rc_ref, rs1_ref, rs2_ref,
                 q_ref, k_ref, v_ref, ga_ref, qg_ref, kg_ref, la_ref, vg_ref, gg_ref,
                 *win_refs, d_att, d_gk, d_gv, first_win_tile):
    x = x_ref[0]
    ms = jnp.mean(x * x, axis=-1, keepdims=True)
    h = x * lax.rsqrt(ms + EPS) * gpre_ref[...]
    h = h * (1.0 + sc_ref[0]) + sh_ref[0]
    hb = h.astype(BF16)

    def seg(a, n):
        return jnp.dot(hb, w_ref[:, a:a + n], preferred_element_type=F32)

    reps = d_att // LANES
    rc = jnp.concatenate([rc_ref[...]] * reps, axis=1)
    rs1 = jnp.concatenate([rs1_ref[...]] * reps, axis=1)
    rs2 = jnp.concatenate([rs2_ref[...]] * reps, axis=1)
    half = ROT_DIM // 2

    def rot(t):
        up = pltpu.roll(t, d_att - half, axis=1)
        dn = pltpu.roll(t, half, axis=1)
        return t * rc + up * rs1 + dn * rs2

    def silu(t):
        return t / (1.0 + jnp.exp(-t))

    o = 0
    q_ref[0] = rot(seg(o, d_att)) * (HEAD_DIM ** -0.5); o += d_att
    k = rot(seg(o, d_att)); o += d_att
    v = seg(o, d_att); o += d_att
    k_ref[0] = k
    v_ref[0] = v
    if win_refs:
        kt_ref, vt_ref = win_refs

        @pl.when(pl.program_id(1) >= first_win_tile)
        def _():
            kt_ref[0] = k.T
            vt_ref[0] = v.T
    ga_ref[0] = silu(seg(o, d_att)); o += d_att
    qg_ref[0] = seg(o, d_gk) * (GLA_DK ** -0.5); o += d_gk
    kg_ref[0] = seg(o, d_gk); o += d_gk
    vg_ref[0] = seg(o, d_gv); o += d_gv
    gg_ref[0] = silu(seg(o, d_gv)); o += d_gv
    alr = seg(o, LANES)
    a1, a2, a3 = _split3(alr)
    w1, w2, w3 = _split3(wa2_ref[...])
    dot = functools.partial(jnp.dot, preferred_element_type=F32)
    z = (dot(a1, w1) + dot(a1, w2) + dot(a2, w1)
         + dot(a1, w3) + dot(a2, w2) + dot(a3, w1)) + ba2_ref[...]
    logsig = jnp.minimum(z, 0.0) - jnp.log1p(jnp.exp(-jnp.abs(z)))
    la_ref[0] = logsig * (1.0 / GLA_TAU)


def _proj_call(x3, mod3, gpre, w_pad, wa2_pad, ba2, tables, *, d_att, d_gk, d_gv, tm, win=0):
    g, r, d = x3.shape
    rm = mod3.shape[1]
    per_row = rm != 1
    grid = (g, r // tm)
    if per_row:
        mod_spec = lambda c: pl.BlockSpec((1, tm, d), lambda i, j: (i, j, c))
    else:
        mod_spec = lambda c: pl.BlockSpec((1, 1, d), lambda i, j: (i, 0, c))
    tab_rows = tables[0].shape[0]
    if tab_rows == tm:
        tab_spec = pl.BlockSpec((tm, LANES), lambda i, j: (0, 0))
    else:
        tab_spec = pl.BlockSpec((tm, LANES), lambda i, j: (j, 0))
    full = lambda a: pl.BlockSpec(a.shape, lambda i, j: (0,) * a.ndim)
    widths = (d_att, d_att, d_att, d_att, d_gk, d_gk, d_gk, d_gv, d_gv)
    out_specs = [pl.BlockSpec((1, tm, w), lambda i, j: (i, j, 0)) for w in widths]
    out_shape = [jax.ShapeDtypeStruct((g, r, w), F32) for w in widths]
    first_win_tile = (r - win) // tm
    if win:
        wspec = pl.BlockSpec((1, d_att, tm), lambda i, j: (i, 0, jnp.maximum(j - first_win_tile, 0)))
        out_specs += [wspec, wspec]
        out_shape += [jax.ShapeDtypeStruct((g, d_att, win), F32)] * 2
    kern = functools.partial(_proj_kernel, d_att=d_att, d_gk=d_gk, d_gv=d_gv,
                             first_win_tile=first_win_tile)
    return pl.pallas_call(
        kern,
        grid=grid,
        in_specs=[pl.BlockSpec((1, tm, d), lambda i, j: (i, j, 0)),
                  mod_spec(0), mod_spec(1), full(gpre), full(w_pad), full(wa2_pad), full(ba2),
                  tab_spec, tab_spec, tab_spec],
        out_specs=out_specs,
        out_shape=out_shape,
        compiler_params=_cparams("parallel", "arbitrary"),
        name="proj",
    )(x3, mod3, mod3, gpre, w_pad, wa2_pad, ba2, *tables)


def _band_bias():
    row = np.arange(ATT_BLOCK)[:, None]
    col = np.arange(2 * ATT_BLOCK)[None, :]
    dist = row + ATT_BLOCK - col
    return jnp.asarray(np.where((dist >= 0) & (dist <= STEPS), 0.0, NEG), F32)


def _att_prompt_kernel(q_ref, k_ref, v_ref, bias_ref, o_ref, acc_ref, m_ref, l_ref, *, seq):
    blk = ATT_BLOCK
    lane = lax.broadcasted_iota(jnp.int32, (blk, LANES), 1)
    lo = lane < HEAD_DIM
    col = lax.broadcasted_iota(jnp.int32, (blk, 2 * blk), 1)
    is_prev = col < blk

    for dil in DILATIONS:
        def rows(start, dil=dil):
            return pl.ds(start, blk) if dil == 1 else pl.ds(start, blk, stride=dil)

        def unit(u, carry, dil=dil, rows=rows):
            r = lax.rem(u, dil)
            n = lax.div(u, dil)
            q0 = r + dil * blk * n
            p0 = r + dil * blk * jnp.maximum(n - 1, 0)
            qp = q_ref[0, rows(q0), :]
            kcat = jnp.concatenate([k_ref[0, rows(p0), :], k_ref[0, rows(q0), :]], axis=0).astype(BF16)
            vcat = jnp.concatenate([v_ref[0, rows(p0), :], v_ref[0, rows(q0), :]], axis=0).astype(BF16)
            no_prev = jnp.where(n > 0, 0.0, NEG)
            bias = bias_ref[...] + jnp.where(is_prev, no_prev, 0.0)
            os, ms, ls = [], [], []
            for sel in (lo, ~lo):
                qm = jnp.where(sel, qp, 0.0).astype(BF16)
                s = _nt(qm, kcat) + bias
                m = jnp.max(s, axis=-1, keepdims=True)
                p = jnp.exp(s - m)
                ls.append(jnp.sum(p, axis=-1, keepdims=True))
                ms.append(m)
                os.append(jnp.dot(p.astype(BF16), vcat, preferred_element_type=F32))
            o_new = jnp.where(lo, os[0], os[1])
            m_new = jnp.where(lo, ms[0], ms[1])
            l_new = jnp.where(lo, ls[0], ls[1])
            if dil == DILATIONS[0]:
                acc_ref[rows(q0), :] = o_new
                m_ref[rows(q0), :] = m_new
                l_ref[rows(q0), :] = l_new
            else:
                m_old = m_ref[rows(q0), :]
                m_tot = jnp.maximum(m_old, m_new)
                a_old = jnp.exp(m_old - m_tot)
                a_new = jnp.exp(m_new - m_tot)
                acc_ref[rows(q0), :] = a_old * acc_ref[rows(q0), :] + a_new * o_new
                l_ref[rows(q0), :] = a_old * l_ref[rows(q0), :] + a_new * l_new
                m_ref[rows(q0), :] = m_tot
            return carry

        lax.fori_loop(0, seq // blk, unit, 0)

    def norm(c, carry):
        r0 = pl.multiple_of(c * blk, blk)
        o_ref[0, pl.ds(r0, blk), :] = acc_ref[pl.ds(r0, blk), :] / l_ref[pl.ds(r0, blk), :]
        return carry

    lax.fori_loop(0, seq // blk, norm, 0)


def _att_prompt_call(q, k, v):
    b, s, d_att = q.shape
    bias = _band_bias()
    slab = pl.BlockSpec((1, s, LANES), lambda i, j: (i, 0, j))
    kern = functools.partial(_att_prompt_kernel, seq=s)
    return pl.pallas_call(
        kern,
        grid=(b, d_att // LANES),
        in_specs=[slab, slab, slab, pl.BlockSpec(bias.shape, lambda i, j: (0, 0))],
        out_specs=slab,
        out_shape=jax.ShapeDtypeStruct((b, s, d_att), F32),
        scratch_shapes=[pltpu.VMEM((s, LANES), F32)] * 3,
        compiler_params=_cparams("parallel", "parallel"),
        name="att_prompt",
    )(q, k, v, bias)


def _gla_consts(blk):
    t = np.arange(LANES)
    same = (t[:, None] // blk) == (t[None, :] // blk)
    tri = ((t[None, :] <= t[:, None]) & same).astype(np.int64)
    mats, lvl, right_rows = [tri], [], []
    h = blk // 2
    while h >= 1:
        pair = t // (2 * h)
        right = (t // h) % 2 == 1
        mid = pair * 2 * h + h - 1
        upto_mid = ((t[None, :] <= mid[:, None]) & same).astype(np.int64)
        mats.append(tri - upto_mid)
        lvl.append(pair[:, None] == pair[None, :])
        right_rows.append(np.broadcast_to(right[:, None], (LANES, LANES)))
        h //= 2
    lvl.append(np.eye(LANES, dtype=bool))
    mats.append(same.astype(np.int64))
    cmat = jnp.asarray(np.concatenate(mats, 0), BF16)
    lvl = jnp.asarray(np.concatenate(lvl, 0), F32)
    rmask = jnp.asarray(np.concatenate(right_rows, 0), F32)
    return cmat, lvl, rmask, len(right_rows)


def _gla_tile(q, k, la, cmat_ref, lvl_ref, rmask_ref, n_lvl):
    t = LANES
    hi, mid, lo3 = _split3(la)
    cm = cmat_ref[...]
    dot = functools.partial(jnp.dot, preferred_element_type=F32)
    g = dot(cm, hi) + dot(cm, mid) + dot(cm, lo3)
    b = g[0:t]
    btot = g[(n_lvl + 1) * t:(n_lvl + 2) * t]
    lane = lax.broadcasted_iota(jnp.int32, (t, LANES), 1)
    lo = lane < GLA_DK
    pairs = []
    for j in range(q.shape[1] // LANES):
        sl = slice(j * LANES, (j + 1) * LANES)
        qp, kp, bp = q[:, sl], k[:, sl], b[:, sl]
        a0 = jnp.zeros((t, t), F32)
        a1 = jnp.zeros((t, t), F32)
        for lv in range(n_lvl + 1):
            m = lvl_ref[lv * t:(lv + 1) * t, :]
            if lv < n_lvl:
                dl = g[(lv + 1) * t:(lv + 2) * t, sl]
                e = jnp.exp(-jnp.abs(dl))
                rm = rmask_ref[lv * t:(lv + 1) * t, :]
                x = qp * e * rm
                y = (kp * e * (1.0 - rm)).astype(BF16)
            else:
                x = qp
                y = kp.astype(BF16)
            x0 = jnp.where(lo, x, 0.0).astype(BF16)
            x1 = jnp.where(lo, 0.0, x).astype(BF16)
            a0 = a0 + m * _nt(x0, y)
            a1 = a1 + m * _nt(x1, y)
        q_in = qp * jnp.exp(bp)
        k_in = kp * jnp.exp(btot[:, sl] - bp)
        pairs.append((a0, a1, q_in, k_in, btot[:, sl], lo))
    return pairs


def _gla_prompt_kernel(q_ref, k_ref, la_ref, v_ref, cmat_ref, lvl_ref, rmask_ref,
                       o_ref, s_out_ref, s_ref, *, n_lvl):
    n = pl.program_id(1)

    @pl.when(n == 0)
    def _():
        s_ref[...] = jnp.zeros_like(s_ref)

    pairs = _gla_tile(q_ref[0], k_ref[0], la_ref[0], cmat_ref, lvl_ref, rmask_ref, n_lvl)
    t = LANES
    row = lax.broadcasted_iota(jnp.int32, (t, t), 0)
    top = row < GLA_DK
    for j, (a0, a1, q_in, k_in, btot, lo) in enumerate(pairs):
        s_prev = s_ref[j]
        sb = s_prev.astype(BF16)
        kt = k_in.T.astype(BF16)
        us = []
        for hh, (a, sel) in enumerate(((a0, lo), (a1, ~lo))):
            vs = slice((2 * j + hh) * GLA_DV, (2 * j + hh + 1) * GLA_DV)
            vb = v_ref[0, :, vs].astype(BF16)
            qm = jnp.where(sel, q_in, 0.0).astype(BF16)
            o = (jnp.dot(a.astype(BF16), vb, preferred_element_type=F32)
                 + jnp.dot(qm, sb, preferred_element_type=F32))
            o_ref[0, :, vs] = o
            us.append(jnp.dot(kt, vb, preferred_element_type=F32))
        a_col = jnp.exp(btot.T)
        s_ref[j] = a_col * s_prev + jnp.where(top, us[0], us[1])

    @pl.when(n == pl.num_programs(1) - 1)
    def _():
        s_out_ref[0] = s_ref[...]


def _gla_prompt_call(qg, kg, la, vg):
    b, s, d_gk = qg.shape
    d_gv = vg.shape[2]
    n_pair = d_gk // LANES
    cmat, lvl, rmask, n_lvl = _gla_consts(LANES)
    t = LANES
    tok = lambda w: pl.BlockSpec((1, t, w), lambda i, n: (i, n, 0))
    full = lambda a: pl.BlockSpec(a.shape, lambda i, n: (0,) * a.ndim)
    kern = functools.partial(_gla_prompt_kernel, n_lvl=n_lvl)
    o, s_out = pl.pallas_call(
        kern,
        grid=(b, s // t),
        in_specs=[tok(d_gk), tok(d_gk), tok(d_gk), tok(d_gv), full(cmat), full(lvl), full(rmask)],
        out_specs=[tok(d_gv), pl.BlockSpec((1, n_pair, t, GLA_DV), lambda i, n: (i, 0, 0, 0))],
        out_shape=[jax.ShapeDtypeStruct((b, s, d_gv), F32),
                   jax.ShapeDtypeStruct((b, n_pair, t, GLA_DV), F32)],
        scratch_shapes=[pltpu.VMEM((n_pair, t, GLA_DV), F32)],
        compiler_params=_cparams("parallel", "arbitrary"),
        name="gla_prompt",
    )(qg, kg, la, vg, cmat, lvl, rmask)
    return o, s_out.reshape(b, 2 * n_pair, GLA_DK, GLA_DV)


def _gla_sample_kernel(q_ref, k_ref, la_ref, v_ref, s_in_ref, cmat_ref, lvl_ref, rmask_ref,
                       o_ref, s_out_ref, *, n_lvl, t_dec):
    pairs = _gla_tile(q_ref[...], k_ref[...], la_ref[...], cmat_ref, lvl_ref, rmask_ref, n_lvl)
    t = LANES
    nb = t // t_dec
    row = lax.broadcasted_iota(jnp.int32, (t, t), 0)
    col = lax.broadcasted_iota(jnp.int32, (t, t), 1)
    top = row < GLA_DK
    for j, (a0, a1, q_in, k_in, btot, lo) in enumerate(pairs):
        s_prev = s_in_ref[:, j]
        sb = s_prev.astype(BF16)
        kt = k_in.T
        bt = btot.T
        vbs = []
        for hh, (a, sel) in enumerate(((a0, lo), (a1, ~lo))):
            vs = slice((2 * j + hh) * GLA_DV, (2 * j + hh + 1) * GLA_DV)
            vb = v_ref[:, vs].astype(BF16)
            vbs.append(vb)
            qm = jnp.where(sel, q_in, 0.0)
            o_inter = jnp.concatenate(
                [jnp.dot(qm[i * t_dec:(i + 1) * t_dec].astype(BF16), sb[i],
                         preferred_element_type=F32) for i in range(nb)], axis=0)
            o_ref[:, vs] = jnp.dot(a.astype(BF16), vb, preferred_element_type=F32) + o_inter
        for i in range(nb):
            mine = (col >= i * t_dec) & (col < (i + 1) * t_dec)
            kti = (kt * mine.astype(F32)).astype(BF16)
            u0 = jnp.dot(kti, vbs[0], preferred_element_type=F32)
            u1 = jnp.dot(kti, vbs[1], preferred_element_type=F32)
            a_col = jnp.exp(jnp.broadcast_to(bt[:, i * t_dec:i * t_dec + 1], (t, GLA_DV)))
            s_out_ref[i, j] = a_col * s_prev[i] + jnp.where(top, u0, u1)


def _gla_sample_call(qg, kg, la, vg, state, t_dec):
    rows, d_gk = qg.shape
    d_gv = vg.shape[1]
    n_pair = d_gk // LANES
    nseq = rows // t_dec
    t = LANES
    nb = t // t_dec
    cmat, lvl, rmask, n_lvl = _gla_consts(t_dec)
    s_in = state.reshape(nseq, n_pair, 2 * GLA_DK, GLA_DV)
    tok = lambda w: pl.BlockSpec((t, w), lambda i: (i, 0))
    full = lambda a: pl.BlockSpec(a.shape, lambda i: (0,) * a.ndim)
    st = pl.BlockSpec((nb, n_pair, t, GLA_DV), lambda i: (i, 0, 0, 0))
    kern = functools.partial(_gla_sample_kernel, n_lvl=n_lvl, t_dec=t_dec)
    o, s_out = pl.pallas_call(
        kern,
        grid=(rows // t,),
        in_specs=[tok(d_gk), tok(d_gk), tok(d_gk), tok(d_gv), st, full(cmat), full(lvl), full(rmask)],
        out_specs=[tok(d_gv), st],
        out_shape=[jax.ShapeDtypeStruct((rows, d_gv), F32),
                   jax.ShapeDtypeStruct(s_in.shape, F32)],
        compiler_params=_cparams("parallel"),
        name="gla_sample",
    )(qg, kg, la, vg, s_in, cmat, lvl, rmask)
    return o, s_out.reshape(state.shape)


def _sample_att_table(w, t_dec):
    idx = np.arange(w + LANES)[None, :]
    tq = np.arange(t_dec)[:, None]
    dist = w + tq - idx
    mult = np.zeros((t_dec, w + LANES), np.int64)
    for d in DILATIONS:
        mult += (dist >= 0) & (dist % d == 0) & (dist // d <= STEPS) & (idx < w + t_dec)
    logw = np.where(mult > 0, np.log(np.maximum(mult, 1)), NEG).astype(np.float32)
    return jnp.asarray(logw)


def _att_sample_kernel(q_ref, kn_ref, vn_ref, ck_ref, cv_ref, logw_ref,
                       o_ref, ko_ref, vo_ref, *, w, t_dec, n_heads):
    lane = lax.broadcasted_iota(jnp.int32, (HEAD_DIM, LANES), 1)
    keep = lane < LANES - t_dec
    zpad = jnp.zeros((LANES - t_dec, n_heads * HEAD_DIM), F32)
    knt = jnp.concatenate([kn_ref[...], zpad], axis=0).T
    vnt = jnp.concatenate([vn_ref[...], zpad], axis=0).T
    q = q_ref[...]
    logw_c = logw_ref[:, 0:w]
    logw_n = logw_ref[:, w:w + LANES]
    outs = []
    for h in range(n_heads):
        hs = slice(h * HEAD_DIM, (h + 1) * HEAD_DIM)
        kt = ck_ref[0, h]
        vt = cv_ref[0, h]
        knh = knt[hs, :]
        vnh = vnt[hs, :]
        qh = q[:, hs].astype(BF16)
        s_c = jnp.dot(qh, kt.astype(BF16), preferred_element_type=F32) + logw_c
        s_n = jnp.dot(qh, knh.astype(BF16), preferred_element_type=F32) + logw_n
        m = jnp.maximum(jnp.max(s_c, axis=-1, keepdims=True), jnp.max(s_n, axis=-1, keepdims=True))
        p_c = jnp.exp(s_c - m)
        p_n = jnp.exp(s_n - m)
        l = jnp.sum(p_c, axis=-1, keepdims=True) + jnp.sum(p_n, axis=-1, keepdims=True)
        o = _nt(p_c.astype(BF16), vt.astype(BF16)) + _nt(p_n.astype(BF16), vnh.astype(BF16))
        outs.append(o / l)
        for src, new, dst in ((kt, knh, ko_ref), (vt, vnh, vo_ref)):
            sh = pltpu.roll(src, w - t_dec, axis=1)
            tail = jnp.where(keep, sh[:, w - LANES:], pltpu.roll(new, LANES - t_dec, axis=1))
            dst[0, h, :, 0:w - LANES] = sh[:, 0:w - LANES]
            dst[0, h, :, w - LANES:w] = tail
    o_ref[...] = jnp.concatenate(outs, axis=1)


def _att_sample_call(q, kn, vn, cache_k, cache_v, t_dec):
    nseq, w, n_heads, hd = cache_k.shape
    d_att = n_heads * hd
    ck = jnp.transpose(cache_k, (0, 2, 3, 1))
    cv = jnp.transpose(cache_v, (0, 2, 3, 1))
    logw = _sample_att_table(w, t_dec)
    tok = pl.BlockSpec((t_dec, d_att), lambda i: (i, 0))
    win = pl.BlockSpec((1, n_heads, hd, w), lambda i: (i, 0, 0, 0))
    kern = functools.partial(_att_sample_kernel, w=w, t_dec=t_dec, n_heads=n_heads)
    o, ko, vo = pl.pallas_call(
        kern,
        grid=(nseq,),
        in_specs=[tok, tok, tok, win, win, pl.BlockSpec(logw.shape, lambda i: (0, 0))],
        out_specs=[tok, win, win],
        out_shape=[jax.ShapeDtypeStruct((nseq * t_dec, d_att), F32),
                   jax.ShapeDtypeStruct(ck.shape, F32),
                   jax.ShapeDtypeStruct(cv.shape, F32)],
        compiler_params=_cparams("parallel"),
        name="att_sample",
    )(q, kn, vn, ck, cv, logw)
    return o, jnp.transpose(ko, (0, 3, 1, 2)), jnp.transpose(vo, (0, 3, 1, 2))


def _finish_kernel(x_ref, gate_ref, att_ref, ga_ref, og_ref, gg_ref, ggla_ref, gpost_ref, wo_ref,
                   y_ref, *, d_att, d_gv):
    att = att_ref[0] * ga_ref[0]
    og = og_ref[0]
    gg = gg_ref[0]
    y = jnp.dot(att.astype(BF16), wo_ref[0:d_att, :], preferred_element_type=F32)
    for h in range(d_gv // GLA_DV):
        sl = slice(h * GLA_DV, (h + 1) * GLA_DV)
        oh = og[:, sl]
        nh = oh * lax.rsqrt(jnp.mean(oh * oh, axis=-1, keepdims=True) + EPS) * ggla_ref[...]
        gl = (nh * gg[:, sl]).astype(BF16)
        y = y + jnp.dot(gl, wo_ref[d_att + h * GLA_DV:d_att + (h + 1) * GLA_DV, :],
                        preferred_element_type=F32)
    yn = y * lax.rsqrt(jnp.mean(y * y, axis=-1, keepdims=True) + EPS) * gpost_ref[...]
    y_ref[0] = x_ref[0] + gate_ref[0] * yn


def _finish_call(x3, mod3, att, ga, og, gg, g_gla, g_post, w_out, *, tm):
    g, r, d = x3.shape
    d_att = ga.shape[2]
    d_gv = og.shape[2]
    per_row = mod3.shape[1] != 1
    if per_row:
        gate_spec = pl.BlockSpec((1, tm, d), lambda i, j: (i, j, 2))
    else:
        gate_spec = pl.BlockSpec((1, 1, d), lambda i, j: (i, 0, 2))
    tok = lambda w: pl.BlockSpec((1, tm, w), lambda i, j: (i, j, 0))
    full = lambda a: pl.BlockSpec(a.shape, lambda i, j: (0,) * a.ndim)
    kern = functools.partial(_finish_kernel, d_att=d_att, d_gv=d_gv)
    return pl.pallas_call(
        kern,
        grid=(g, r // tm),
        in_specs=[tok(d), gate_spec, tok(d_att), tok(d_att), tok(d_gv), tok(d_gv),
                  full(g_gla), full(g_post), full(w_out)],
        out_specs=tok(d),
        out_shape=jax.ShapeDtypeStruct((g, r, d), F32),
        compiler_params=_cparams("parallel", "arbitrary"),
        name="finish",
    )(x3, mod3, att, ga, og, gg, g_gla, g_post, w_out)


def kernel(x_prompt, x_sample, c_prompt, c_sample, cache_k_win, cache_v_win, state_gla,
           w_mod, b_mod, g_pre, g_post, w_in, w_a2, b_a2, g_gla, w_out):
    b, s, d = x_prompt.shape
    nseq, t_dec, _ = x_sample.shape
    w = cache_k_win.shape[1]
    d_att = cache_k_win.shape[2] * cache_k_win.shape[3]
    d_gk = w_a2.shape[1]
    d_gv = state_gla.shape[1] * state_gla.shape[3]
    d_main = 4 * d_att + 2 * d_gk + 2 * d_gv
    assert w_in.shape[1] == d_main + GLA_GATE_RANK and cache_k_win.shape[3] == HEAD_DIM
    assert state_gla.shape[2] == GLA_DK and state_gla.shape[3] == GLA_DV
    assert w >= STEPS * max(DILATIONS) and s % (ATT_BLOCK * max(DILATIONS)) == 0 and s >= w

    w_pad = jnp.pad(w_in, ((0, 0), (0, LANES - GLA_GATE_RANK))).astype(BF16)
    wa2_pad = jnp.pad(w_a2, ((0, LANES - GLA_GATE_RANK), (0, 0)))
    wo = w_out.astype(BF16)
    gpre = g_pre.reshape(1, d)
    gpost = g_post.reshape(1, d)
    ggla = g_gla.reshape(1, GLA_DV)
    ba2 = b_a2.reshape(1, d_gk)

    n_c = b + nseq
    n_c_pad = -(-n_c // SUBLANES) * SUBLANES
    c_all = jnp.pad(jnp.concatenate([c_prompt, c_sample], axis=0), ((0, n_c_pad - n_c), (0, 0)))
    mod = _mod_call(c_all, w_mod, b_mod)
    mod_p = mod[:b].reshape(b, 1, 3 * d)

    proj = functools.partial(_proj_call, d_att=d_att, d_gk=d_gk, d_gv=d_gv)

    tm = 512
    tabs_p = _rope_tables(jnp.arange(s))
    q, k, v, ga, qg, kg, la, vg, gg, kt, vt = proj(x_prompt, mod_p, gpre, w_pad, wa2_pad, ba2, tabs_p,
                                                   tm=tm, win=w)
    att = _att_prompt_call(q, k, v)
    og, gla_p = _gla_prompt_call(qg, kg, la, vg)
    y_prompt = _finish_call(x_prompt, mod_p, att, ga, og, gg, ggla, gpost, wo, tm=tm)
    n_heads = d_att // HEAD_DIM
    k_win_prompt = jnp.transpose(kt.reshape(b, n_heads, HEAD_DIM, w), (0, 3, 1, 2))
    v_win_prompt = jnp.transpose(vt.reshape(b, n_heads, HEAD_DIM, w), (0, 3, 1, 2))

    rows = nseq * t_dec
    tms = min(512, rows)
    xs = x_sample.reshape(rows // tms, tms, d)
    mod_s = jnp.repeat(mod[b:b + nseq], t_dec, axis=0).reshape(rows // tms, tms, 3 * d)
    tabs_s = _rope_tables(jnp.tile(PAST_LEN + jnp.arange(t_dec), tms // t_dec))
    outs = proj(xs, mod_s, gpre, w_pad, wa2_pad, ba2, tabs_s, tm=tms)
    q, k, v, ga, qg, kg, la, vg, gg = [o.reshape(rows, o.shape[2]) for o in outs]
    o_att, k_win_sample, v_win_sample = _att_sample_call(q, k, v, cache_k_win, cache_v_win, t_dec)
    og, gla_s = _gla_sample_call(qg, kg, la, vg, state_gla, t_dec)
    r3 = lambda a: a.reshape(rows // tms, tms, a.shape[1])
    y_sample = _finish_call(xs, mod_s, r3(o_att), r3(ga), r3(og), r3(gg),
                            ggla, gpost, wo, tm=tms).reshape(x_sample.shape)

    return (y_prompt, y_sample, k_win_prompt, v_win_prompt, gla_p,
            k_win_sample, v_win_sample, gla_s)
```

```python
import functools
import math

import numpy as np
import jax
import jax.numpy as jnp
from jax import lax
from jax.experimental import pallas as pl
from jax.experimental.pallas import tpu as pltpu

F32 = jnp.float32
BF16 = jnp.bfloat16

HEAD_DIM = 64
ROT_DIM = HEAD_DIM // 4
ROPE_THETA = 500000.0
DILATIONS = (1, 4, 16)
STEPS = 128
ATT_BLOCK = 128
UNITS_PER_ITER = 4
PAST_LEN = 8192
GLA_DK = 64
GLA_DV = 128
GLA_GATE_RANK = 16
GLA_TAU = 16.0
EPS = 1e-6
NEG = -1e30

LANES = 128
SUBLANES = 8
VMEM_LIMIT = 48 * 1024 * 1024


def _cparams(*sem):
    return pltpu.CompilerParams(dimension_semantics=sem, vmem_limit_bytes=VMEM_LIMIT)


def _nt(a, b):
    return lax.dot_general(a, b, (((1,), (1,)), ((), ())), preferred_element_type=F32)


def _split3(x):
    hi = x.astype(BF16)
    r1 = x - hi.astype(F32)
    mid = r1.astype(BF16)
    lo = (r1 - mid.astype(F32)).astype(BF16)
    return hi, mid, lo


def _mod_kernel(c_ref, w_ref, b_ref, o_ref):
    c = c_ref[...]
    a = c / (1.0 + jnp.exp(-c))
    o_ref[...] = jnp.dot(a, w_ref[...], precision=lax.Precision.HIGHEST,
                         preferred_element_type=F32) + b_ref[...]


def _mod_call(c, w_mod, b_mod):
    n, d = c.shape
    d3 = w_mod.shape[1]
    tn = 512
    return pl.pallas_call(
        _mod_kernel,
        grid=(d3 // tn,),
        in_specs=[pl.BlockSpec((n, d), lambda j: (0, 0)),
                  pl.BlockSpec((d, tn), lambda j: (0, j)),
                  pl.BlockSpec((1, tn), lambda j: (0, j))],
        out_specs=pl.BlockSpec((n, tn), lambda j: (0, j)),
        out_shape=jax.ShapeDtypeStruct((n, d3), F32),
        compiler_params=_cparams("arbitrary"),
        name="mod",
    )(c, w_mod, b_mod.reshape(1, d3))


def _rope_tables(pos):
    half = ROT_DIM // 2
    inv = 1.0 / (ROPE_THETA ** (jnp.arange(half, dtype=F32) / half))
    ang = pos.astype(F32)[:, None] * inv[None, :]
    cos, sin = jnp.cos(ang), jnp.sin(ang)
    n = pos.shape[0]
    ones = jnp.ones((n, HEAD_DIM - ROT_DIM), F32)
    zeros_h = jnp.zeros((n, half), F32)
    zeros_r = jnp.zeros((n, HEAD_DIM - ROT_DIM), F32)
    c = jnp.concatenate([cos, cos, ones], axis=1)
    s1 = jnp.concatenate([-sin, zeros_h, zeros_r], axis=1)
    s2 = jnp.concatenate([zeros_h, sin, zeros_r], axis=1)
    rep = LANES // HEAD_DIM
    return tuple(jnp.tile(t, (1, rep)) for t in (c, s1, s2))


def _proj_kernel(x_ref, sh_ref, sc_ref, gpre_ref, w_ref, wa2_ref, ba2_ref,
                 rc_ref, rs1_ref, rs2_ref,
                 q_ref, k_ref, v_ref, ga_ref, qg_ref, kg_ref, la_ref, vg_ref, gg_ref,
                 *win_refs, d_att, d_gk, d_gv, first_win_tile):
    x = x_ref[0]
    ms = jnp.mean(x * x, axis=-1, keepdims=True)
    h = x * lax.rsqrt(ms + EPS) * gpre_ref[...]
    h = h * (1.0 + sc_ref[0]) + sh_ref[0]
    hb = h.astype(BF16)

    def seg(a, n):
        return jnp.dot(hb, w_ref[:, a:a + n], preferred_element_type=F32)

    reps = d_att // LANES
    rc = jnp.concatenate([rc_ref[...]] * reps, axis=1)
    rs1 = jnp.concatenate([rs1_ref[...]] * reps, axis=1)
    rs2 = jnp.concatenate([rs2_ref[...]] * reps, axis=1)
    half = ROT_DIM // 2

    def rot(t):
        up = pltpu.roll(t, d_att - half, axis=1)
        dn = pltpu.roll(t, half, axis=1)
        return t * rc + up * rs1 + dn * rs2

    def silu(t):
        return t / (1.0 + jnp.exp(-t))

    o = 0
    q_ref[0] = rot(seg(o, d_att)) * (HEAD_DIM ** -0.5); o += d_att
    k = rot(seg(o, d_att)); o += d_att
    v = seg(o, d_att); o += d_att
    k_ref[0] = k
    v_ref[0] = v
    if win_refs:
        kt_ref, vt_ref = win_refs

        @pl.when(pl.program_id(1) >= first_win_tile)
        def _():
            kt_ref[0] = k.T
            vt_ref[0] = v.T
    ga_ref[0] = silu(seg(o, d_att)); o += d_att
    qg_ref[0] = seg(o, d_gk) * (GLA_DK ** -0.5); o += d_gk
    kg_ref[0] = seg(o, d_gk); o += d_gk
    vg_ref[0] = seg(o, d_gv); o += d_gv
    gg_ref[0] = silu(seg(o, d_gv)); o += d_gv
    alr = seg(o, LANES)
    a1, a2, a3 = _split3(alr)
    w1, w2, w3 = _split3(wa2_ref[...])
    dot = functools.partial(jnp.dot, preferred_element_type=F32)
    z = (dot(a1, w1) + dot(a1, w2) + dot(a2, w1)
         + dot(a1, w3) + dot(a2, w2) + dot(a3, w1)) + ba2_ref[...]
    logsig = jnp.minimum(z, 0.0) - jnp.log1p(jnp.exp(-jnp.abs(z)))
    la_ref[0] = logsig * (1.0 / GLA_TAU)


def _proj_call(x3, mod3, gpre, w_pad, wa2_pad, ba2, tables, *, d_att, d_gk, d_gv, tm, win=0):
    g, r, d = x3.shape
    rm = mod3.shape[1]
    per_row = rm != 1
    grid = (g, r // tm)
    if per_row:
        mod_spec = lambda c: pl.BlockSpec((1, tm, d), lambda i, j: (i, j, c))
    else:
        mod_spec = lambda c: pl.BlockSpec((1, 1, d), lambda i, j: (i, 0, c))
    tab_rows = tables[0].shape[0]
    if tab_rows == tm:
        tab_spec = pl.BlockSpec((tm, LANES), lambda i, j: (0, 0))
    else:
        tab_spec = pl.BlockSpec((tm, LANES), lambda i, j: (j, 0))
    full = lambda a: pl.BlockSpec(a.shape, lambda i, j: (0,) * a.ndim)
    widths = (d_att, d_att, d_att, d_att, d_gk, d_gk, d_gk, d_gv, d_gv)
    out_specs = [pl.BlockSpec((1, tm, w), lambda i, j: (i, j, 0)) for w in widths]
    out_shape = [jax.ShapeDtypeStruct((g, r, w), F32) for w in widths]
    first_win_tile = (r - win) // tm
    if win:
        wspec = pl.BlockSpec((1, d_att, tm), lambda i, j: (i, 0, jnp.maximum(j - first_win_tile, 0)))
        out_specs += [wspec, wspec]
        out_shape += [jax.ShapeDtypeStruct((g, d_att, win), F32)] * 2
    kern = functools.partial(_proj_kernel, d_att=d_att, d_gk=d_gk, d_gv=d_gv,
                             first_win_tile=first_win_tile)
    return pl.pallas_call(
        kern,
        grid=grid,
        in_specs=[pl.BlockSpec((1, tm, d), lambda i, j: (i, j, 0)),
                  mod_spec(0), mod_spec(1), full(gpre), full(w_pad), full(wa2_pad), full(ba2),
                  tab_spec, tab_spec, tab_spec],
        out_specs=out_specs,
        out_shape=out_shape,
        compiler_params=_cparams("parallel", "arbitrary"),
        name="proj",
    )(x3, mod3, mod3, gpre, w_pad, wa2_pad, ba2, *tables)


def _band_bias():
    row = np.arange(ATT_BLOCK)[:, None]
    col = np.arange(2 * ATT_BLOCK)[None, :]
    dist = row + ATT_BLOCK - col
    band = (dist >= 0) & (dist <= STEPS)
    first = band & (col >= ATT_BLOCK)
    return jnp.asarray(np.where(np.stack([first, band]), 0.0, NEG), F32)


def _att_prompt_kernel(q_ref, k_ref, v_ref, bias_ref, o_ref, acc_ref, m_ref, l_ref, *, seq):
    blk = ATT_BLOCK
    lane = lax.broadcasted_iota(jnp.int32, (blk, LANES), 1)
    lo = lane < HEAD_DIM

    for dil in DILATIONS:
        def rows(start, dil=dil):
            return pl.ds(start, blk) if dil == 1 else pl.ds(start, blk, stride=dil)

        def unit(u, carry, dil=dil, rows=rows):
            r = lax.rem(u, dil)
            n = lax.div(u, dil)
            q0 = r + dil * blk * n
            p0 = r + dil * blk * jnp.maximum(n - 1, 0)
            qp = q_ref[0, rows(q0), :]
            kcat = jnp.concatenate([k_ref[0, rows(p0), :], k_ref[0, rows(q0), :]], axis=0).astype(BF16)
            vcat = jnp.concatenate([v_ref[0, rows(p0), :], v_ref[0, rows(q0), :]], axis=0).astype(BF16)
            bias = bias_ref[jnp.minimum(n, 1)]
            os, ms, ls = [], [], []
            for sel in (lo, ~lo):
                qm = jnp.where(sel, qp, 0.0).astype(BF16)
                s = _nt(qm, kcat) + bias
                m = jnp.max(s, axis=-1, keepdims=True)
                p = jnp.exp(s - m)
                ls.append(jnp.sum(p, axis=-1, keepdims=True))
                ms.append(m)
                os.append(jnp.dot(p.astype(BF16), vcat, preferred_element_type=F32))
            o_new = jnp.where(lo, os[0], os[1])
            m_new = jnp.where(lo, ms[0], ms[1])
            l_new = jnp.where(lo, ls[0], ls[1])
            if dil == DILATIONS[0]:
                acc_ref[rows(q0), :] = o_new
                m_ref[rows(q0), :] = m_new
                l_ref[rows(q0), :] = l_new
            else:
                m_old = m_ref[rows(q0), :]
                m_tot = jnp.maximum(m_old, m_new)
                a_old = jnp.exp(m_old - m_tot)
                a_new = jnp.exp(m_new - m_tot)
                acc_ref[rows(q0), :] = a_old * acc_ref[rows(q0), :] + a_new * o_new
                l_ref[rows(q0), :] = a_old * l_ref[rows(q0), :] + a_new * l_new
                m_ref[rows(q0), :] = m_tot
            return carry

        lax.fori_loop(0, seq // blk, unit, 0, unroll=UNITS_PER_ITER)

    def norm(c, carry):
        r0 = pl.multiple_of(c * blk, blk)
        o_ref[0, pl.ds(r0, blk), :] = acc_ref[pl.ds(r0, blk), :] / l_ref[pl.ds(r0, blk), :]
        return carry

    lax.fori_loop(0, seq // blk, norm, 0)


def _att_prompt_call(q, k, v):
    b, s, d_att = q.shape
    bias = _band_bias()
    slab = pl.BlockSpec((1, s, LANES), lambda i, j: (i, 0, j))
    kern = functools.partial(_att_prompt_kernel, seq=s)
    return pl.pallas_call(
        kern,
        grid=(b, d_att // LANES),
        in_specs=[slab, slab, slab, pl.BlockSpec(bias.shape, lambda i, j: (0, 0, 0))],
        out_specs=slab,
        out_shape=jax.ShapeDtypeStruct((b, s, d_att), F32),
        scratch_shapes=[pltpu.VMEM((s, LANES), F32)] * 3,
        compiler_params=_cparams("parallel", "parallel"),
        name="att_prompt",
    )(q, k, v, bias)


def _gla_consts(blk):
    t = np.arange(LANES)
    same = (t[:, None] // blk) == (t[None, :] // blk)
    tri = ((t[None, :] <= t[:, None]) & same).astype(np.int64)
    mats, lvl, right_rows = [tri], [], []
    h = blk // 2
    while h >= 1:
        pair = t // (2 * h)
        right = (t // h) % 2 == 1
        mid = pair * 2 * h + h - 1
        upto_mid = ((t[None, :] <= mid[:, None]) & same).astype(np.int64)
        mats.append(tri - upto_mid)
        lvl.append(pair[:, None] == pair[None, :])
        right_rows.append(np.broadcast_to(right[:, None], (LANES, LANES)))
        h //= 2
    lvl.append(np.eye(LANES, dtype=bool))
    mats.append(same.astype(np.int64))
    cmat = jnp.asarray(np.concatenate(mats, 0), BF16)
    lvl = jnp.asarray(np.concatenate(lvl, 0), F32)
    rmask = jnp.asarray(np.concatenate(right_rows, 0), F32)
    return cmat, lvl, rmask, len(right_rows)


def _gla_tile(q, k, la, cmat_ref, lvl_ref, rmask_ref, n_lvl):
    t = LANES
    hi, mid, lo3 = _split3(la)
    cm = cmat_ref[...]
    dot = functools.partial(jnp.dot, preferred_element_type=F32)
    g = dot(cm, hi) + dot(cm, mid) + dot(cm, lo3)
    b = g[0:t]
    btot = g[(n_lvl + 1) * t:(n_lvl + 2) * t]
    lane = lax.broadcasted_iota(jnp.int32, (t, LANES), 1)
    lo = lane < GLA_DK
    pairs = []
    for j in range(q.shape[1] // LANES):
        sl = slice(j * LANES, (j + 1) * LANES)
        qp, kp, bp = q[:, sl], k[:, sl], b[:, sl]
        a0 = jnp.zeros((t, t), F32)
        a1 = jnp.zeros((t, t), F32)
        for lv in range(n_lvl + 1):
            m = lvl_ref[lv * t:(lv + 1) * t, :]
            if lv < n_lvl:
                dl = g[(lv + 1) * t:(lv + 2) * t, sl]
                e = jnp.exp(-jnp.abs(dl))
                rm = rmask_ref[lv * t:(lv + 1) * t, :]
                x = qp * e * rm
                y = (kp * e * (1.0 - rm)).astype(BF16)
            else:
                x = qp
                y = kp.astype(BF16)
            x0 = jnp.where(lo, x, 0.0).astype(BF16)
            x1 = jnp.where(lo, 0.0, x).astype(BF16)
            a0 = a0 + m * _nt(x0, y)
            a1 = a1 + m * _nt(x1, y)
        q_in = qp * jnp.exp(bp)
        k_in = kp * jnp.exp(btot[:, sl] - bp)
        pairs.append((a0, a1, q_in, k_in, btot[:, sl], lo))
    return pairs


def _gla_prompt_body(q_ref, k_ref, la_ref, v_ref, cmat_ref, lvl_ref, rmask_ref, o_ref, s_ref,
                     *, n_lvl):
    pairs = _gla_tile(q_ref[0], k_ref[0], la_ref[0], cmat_ref, lvl_ref, rmask_ref, n_lvl)
    t = LANES
    row = lax.broadcasted_iota(jnp.int32, (t, t), 0)
    top = row < GLA_DK
    for j, (a0, a1, q_in, k_in, btot, lo) in enumerate(pairs):
        s_prev = s_ref[j]
        sb = s_prev.astype(BF16)
        kt = k_in.T.astype(BF16)
        us = []
        for hh, (a, sel) in enumerate(((a0, lo), (a1, ~lo))):
            vs = slice((2 * j + hh) * GLA_DV, (2 * j + hh + 1) * GLA_DV)
            vb = v_ref[0, :, vs].astype(BF16)
            qm = jnp.where(sel, q_in, 0.0).astype(BF16)
            o = (jnp.dot(a.astype(BF16), vb, preferred_element_type=F32)
                 + jnp.dot(qm, sb, preferred_element_type=F32))
            o_ref[0, :, vs] = o
            us.append(jnp.dot(kt, vb, preferred_element_type=F32))
        a_col = jnp.exp(btot.T)
        s_ref[j] = a_col * s_prev + jnp.where(top, us[0], us[1])


def _gla_sample_kernel(q_ref, k_ref, la_ref, v_ref, s_in_ref, cmat_ref, lvl_ref, rmask_ref,
                       o_ref, s_out_ref, *, n_lvl, t_dec):
    pairs = _gla_tile(q_ref[...], k_ref[...], la_ref[...], cmat_ref, lvl_ref, rmask_ref, n_lvl)
    t = LANES
    nb = t // t_dec
    row = lax.broadcasted_iota(jnp.int32, (t, t), 0)
    col = lax.broadcasted_iota(jnp.int32, (t, t), 1)
    top = row < GLA_DK
    for j, (a0, a1, q_in, k_in, btot, lo) in enumerate(pairs):
        s_prev = s_in_ref[:, j]
        sb = s_prev.astype(BF16)
        kt = k_in.T
        bt = btot.T
        vbs = []
        for hh, (a, sel) in enumerate(((a0, lo), (a1, ~lo))):
            vs = slice((2 * j + hh) * GLA_DV, (2 * j + hh + 1) * GLA_DV)
            vb = v_ref[:, vs].astype(BF16)
            vbs.append(vb)
            qm = jnp.where(sel, q_in, 0.0)
            o_inter = jnp.concatenate(
                [jnp.dot(qm[i * t_dec:(i + 1) * t_dec].astype(BF16), sb[i],
                         preferred_element_type=F32) for i in range(nb)], axis=0)
            o_ref[:, vs] = jnp.dot(a.astype(BF16), vb, preferred_element_type=F32) + o_inter
        for i in range(nb):
            mine = (col >= i * t_dec) & (col < (i + 1) * t_dec)
            kti = (kt * mine.astype(F32)).astype(BF16)
            u0 = jnp.dot(kti, vbs[0], preferred_element_type=F32)
            u1 = jnp.dot(kti, vbs[1], preferred_element_type=F32)
            a_col = jnp.exp(jnp.broadcast_to(bt[:, i * t_dec:i * t_dec + 1], (t, GLA_DV)))
            s_out_ref[i, j] = a_col * s_prev[i] + jnp.where(top, u0, u1)


def _gla_sample_call(qg, kg, la, vg, state, t_dec):
    rows, d_gk = qg.shape
    d_gv = vg.shape[1]
    n_pair = d_gk // LANES
    nseq = rows // t_dec
    t = LANES
    nb = t // t_dec
    cmat, lvl, rmask, n_lvl = _gla_consts(t_dec)
    s_in = state.reshape(nseq, n_pair, 2 * GLA_DK, GLA_DV)
    tok = lambda w: pl.BlockSpec((t, w), lambda i: (i, 0))
    full = lambda a: pl.BlockSpec(a.shape, lambda i: (0,) * a.ndim)
    st = pl.BlockSpec((nb, n_pair, t, GLA_DV), lambda i: (i, 0, 0, 0))
    kern = functools.partial(_gla_sample_kernel, n_lvl=n_lvl, t_dec=t_dec)
    o, s_out = pl.pallas_call(
        kern,
        grid=(rows // t,),
        in_specs=[tok(d_gk), tok(d_gk), tok(d_gk), tok(d_gv), st, full(cmat), full(lvl), full(rmask)],
        out_specs=[tok(d_gv), st],
        out_shape=[jax.ShapeDtypeStruct((rows, d_gv), F32),
                   jax.ShapeDtypeStruct(s_in.shape, F32)],
        compiler_params=_cparams("parallel"),
        name="gla_sample",
    )(qg, kg, la, vg, s_in, cmat, lvl, rmask)
    return o, s_out.reshape(state.shape)


def _sample_att_table(w, t_dec):
    idx = np.arange(w + LANES)[None, :]
    tq = np.arange(t_dec)[:, None]
    dist = w + tq - idx
    mult = np.zeros((t_dec, w + LANES), np.int64)
    for d in DILATIONS:
        mult += (dist >= 0) & (dist % d == 0) & (dist // d <= STEPS) & (idx < w + t_dec)
    logw = np.where(mult > 0, np.log(np.maximum(mult, 1)), NEG).astype(np.float32)
    return jnp.asarray(logw)


def _att_sample_body(q_ref, kn_ref, vn_ref, ck_ref, cv_ref, logw_ref,
                     o_ref, ko_ref, vo_ref, *, w, t_dec, n_heads):
    lane = lax.broadcasted_iota(jnp.int32, (HEAD_DIM, LANES), 1)
    keep = lane < LANES - t_dec
    zpad = jnp.zeros((LANES - t_dec, n_heads * HEAD_DIM), F32)
    knt = jnp.concatenate([kn_ref[...], zpad], axis=0).T
    vnt = jnp.concatenate([vn_ref[...], zpad], axis=0).T
    q = q_ref[...]
    logw_c = logw_ref[:, 0:w]
    logw_n = logw_ref[:, w:w + LANES]
    outs = []
    for h in range(n_heads):
        hs = slice(h * HEAD_DIM, (h + 1) * HEAD_DIM)
        kt = ck_ref[0, h]
        vt = cv_ref[0, h]
        knh = knt[hs, :]
        vnh = vnt[hs, :]
        qh = q[:, hs].astype(BF16)
        s_c = jnp.dot(qh, kt.astype(BF16), preferred_element_type=F32) + logw_c
        s_n = jnp.dot(qh, knh.astype(BF16), preferred_element_type=F32) + logw_n
        m = jnp.maximum(jnp.max(s_c, axis=-1, keepdims=True), jnp.max(s_n, axis=-1, keepdims=True))
        p_c = jnp.exp(s_c - m)
        p_n = jnp.exp(s_n - m)
        l = jnp.sum(p_c, axis=-1, keepdims=True) + jnp.sum(p_n, axis=-1, keepdims=True)
        o = _nt(p_c.astype(BF16), vt.astype(BF16)) + _nt(p_n.astype(BF16), vnh.astype(BF16))
        outs.append(o / l)
        for src, new, dst in ((kt, knh, ko_ref), (vt, vnh, vo_ref)):
            sh = pltpu.roll(src, w - t_dec, axis=1)
            tail = jnp.where(keep, sh[:, w - LANES:], pltpu.roll(new, LANES - t_dec, axis=1))
            dst[0, h, :, 0:w - LANES] = sh[:, 0:w - LANES]
            dst[0, h, :, w - LANES:w] = tail
    o_ref[...] = jnp.concatenate(outs, axis=1)


def _stream_kernel(q_ref, kn_ref, vn_ref, ck_ref, cv_ref, logw_ref,
                   gq_ref, gk_ref, gla_ref, gv_ref, cmat_ref, lvl_ref, rmask_ref,
                   o_ref, ko_ref, vo_ref, og_ref, s_out_ref, s_ref,
                   *, w, t_dec, n_heads, n_lvl, n_tiles):
    n = lax.rem(pl.program_id(0), n_tiles)

    @pl.when(n == 0)
    def _():
        s_ref[...] = jnp.zeros_like(s_ref)

    _att_sample_body(q_ref, kn_ref, vn_ref, ck_ref, cv_ref, logw_ref, o_ref, ko_ref, vo_ref,
                     w=w, t_dec=t_dec, n_heads=n_heads)
    _gla_prompt_body(gq_ref, gk_ref, gla_ref, gv_ref, cmat_ref, lvl_ref, rmask_ref, og_ref, s_ref,
                     n_lvl=n_lvl)

    @pl.when(n == n_tiles - 1)
    def _():
        s_out_ref[0] = s_ref[...]


def _stream_call(q, kn, vn, cache_k, cache_v, t_dec, qg, kg, la, vg):
    nseq, w, n_heads, hd = cache_k.shape
    d_att = n_heads * hd
    b, s, d_gk = qg.shape
    d_gv = vg.shape[2]
    t = LANES
    n_tiles = s // t
    n_pair = d_gk // LANES
    assert nseq == b * n_tiles, "one prompt GLA tile is paired with each sample sequence"
    ck = jnp.transpose(cache_k, (0, 2, 3, 1))
    cv = jnp.transpose(cache_v, (0, 2, 3, 1))
    logw = _sample_att_table(w, t_dec)
    cmat, lvl, rmask, n_lvl = _gla_consts(LANES)
    tok = pl.BlockSpec((t_dec, d_att), lambda i: (i, 0))
    win = pl.BlockSpec((1, n_heads, hd, w), lambda i: (i, 0, 0, 0))
    full = lambda a: pl.BlockSpec(a.shape, lambda i: (0,) * a.ndim)
    gtok = lambda wd: pl.BlockSpec((1, t, wd), lambda i: (i // n_tiles, i % n_tiles, 0))
    kern = functools.partial(_stream_kernel, w=w, t_dec=t_dec, n_heads=n_heads,
                             n_lvl=n_lvl, n_tiles=n_tiles)
    o, ko, vo, og, s_out = pl.pallas_call(
        kern,
        grid=(nseq,),
        in_specs=[tok, tok, tok, win, win, full(logw),
                  gtok(d_gk), gtok(d_gk), gtok(d_gk), gtok(d_gv), full(cmat), full(lvl), full(rmask)],
        out_specs=[tok, win, win, gtok(d_gv),
                   pl.BlockSpec((1, n_pair, t, GLA_DV), lambda i: (i // n_tiles, 0, 0, 0))],
        out_shape=[jax.ShapeDtypeStruct((nseq * t_dec, d_att), F32),
                   jax.ShapeDtypeStruct(ck.shape, F32),
                   jax.ShapeDtypeStruct(cv.shape, F32),
                   jax.ShapeDtypeStruct((b, s, d_gv), F32),
                   jax.ShapeDtypeStruct((b, n_pair, t, GLA_DV), F32)],
        scratch_shapes=[pltpu.VMEM((n_pair, t, GLA_DV), F32)],
        compiler_params=_cparams("arbitrary"),
        name="stream",
    )(q, kn, vn, ck, cv, logw, qg, kg, la, vg, cmat, lvl, rmask)
    return (o, jnp.transpose(ko, (0, 3, 1, 2)), jnp.transpose(vo, (0, 3, 1, 2)),
            og, s_out.reshape(b, 2 * n_pair, GLA_DK, GLA_DV))


def _finish_kernel(x_ref, gate_ref, att_ref, ga_ref, og_ref, gg_ref, ggla_ref, gpost_ref, wo_ref,
                   y_ref, *, d_att, d_gv):
    att = att_ref[0] * ga_ref[0]
    og = og_ref[0]
    gg = gg_ref[0]
    y = jnp.dot(att.astype(BF16), wo_ref[0:d_att, :], preferred_element_type=F32)
    for h in range(d_gv // GLA_DV):
        sl = slice(h * GLA_DV, (h + 1) * GLA_DV)
        oh = og[:, sl]
        nh = oh * lax.rsqrt(jnp.mean(oh * oh, axis=-1, keepdims=True) + EPS) * ggla_ref[...]
        gl = (nh * gg[:, sl]).astype(BF16)
        y = y + jnp.dot(gl, wo_ref[d_att + h * GLA_DV:d_att + (h + 1) * GLA_DV, :],
                        preferred_element_type=F32)
    yn = y * lax.rsqrt(jnp.mean(y * y, axis=-1, keepdims=True) + EPS) * gpost_ref[...]
    y_ref[0] = x_ref[0] + gate_ref[0] * yn


def _finish_call(x3, mod3, att, ga, og, gg, g_gla, g_post, w_out, *, tm):
    g, r, d = x3.shape
    d_att = ga.shape[2]
    d_gv = og.shape[2]
    per_row = mod3.shape[1] != 1
    if per_row:
        gate_spec = pl.BlockSpec((1, tm, d), lambda i, j: (i, j, 2))
    else:
        gate_spec = pl.BlockSpec((1, 1, d), lambda i, j: (i, 0, 2))
    tok = lambda w: pl.BlockSpec((1, tm, w), lambda i, j: (i, j, 0))
    full = lambda a: pl.BlockSpec(a.shape, lambda i, j: (0,) * a.ndim)
    kern = functools.partial(_finish_kernel, d_att=d_att, d_gv=d_gv)
    return pl.pallas_call(
        kern,
        grid=(g, r // tm),
        in_specs=[tok(d), gate_spec, tok(d_att), tok(d_att), tok(d_gv), tok(d_gv),
                  full(g_gla), full(g_post), full(w_out)],
        out_specs=tok(d),
        out_shape=jax.ShapeDtypeStruct((g, r, d), F32),
        compiler_params=_cparams("parallel", "arbitrary"),
        name="finish",
    )(x3, mod3, att, ga, og, gg, g_gla, g_post, w_out)


def kernel(x_prompt, x_sample, c_prompt, c_sample, cache_k_win, cache_v_win, state_gla,
           w_mod, b_mod, g_pre, g_post, w_in, w_a2, b_a2, g_gla, w_out):
    b, s, d = x_prompt.shape
    nseq, t_dec, _ = x_sample.shape
    w = cache_k_win.shape[1]
    d_att = cache_k_win.shape[2] * cache_k_win.shape[3]
    d_gk = w_a2.shape[1]
    d_gv = state_gla.shape[1] * state_gla.shape[3]
    d_main = 4 * d_att + 2 * d_gk + 2 * d_gv
    assert w_in.shape[1] == d_main + GLA_GATE_RANK and cache_k_win.shape[3] == HEAD_DIM
    assert state_gla.shape[2] == GLA_DK and state_gla.shape[3] == GLA_DV
    assert w >= STEPS * max(DILATIONS) and s % (ATT_BLOCK * max(DILATIONS)) == 0 and s >= w

    w_pad = jnp.pad(w_in, ((0, 0), (0, LANES - GLA_GATE_RANK))).astype(BF16)
    wa2_pad = jnp.pad(w_a2, ((0, LANES - GLA_GATE_RANK), (0, 0)))
    wo = w_out.astype(BF16)
    gpre = g_pre.reshape(1, d)
    gpost = g_post.reshape(1, d)
    ggla = g_gla.reshape(1, GLA_DV)
    ba2 = b_a2.reshape(1, d_gk)

    n_c = b + nseq
    n_c_pad = -(-n_c // SUBLANES) * SUBLANES
    c_all = jnp.pad(jnp.concatenate([c_prompt, c_sample], axis=0), ((0, n_c_pad - n_c), (0, 0)))
    mod = _mod_call(c_all, w_mod, b_mod)
    mod_p = mod[:b].reshape(b, 1, 3 * d)

    proj = functools.partial(_proj_call, d_att=d_att, d_gk=d_gk, d_gv=d_gv)

    tm = 512
    tabs_p = _rope_tables(jnp.arange(s))
    q, k, v, ga, qg, kg, la, vg, gg, kt, vt = proj(x_prompt, mod_p, gpre, w_pad, wa2_pad, ba2, tabs_p,
                                                   tm=tm, win=w)
    rows = nseq * t_dec
    tms = min(512, rows)
    xs = x_sample.reshape(rows // tms, tms, d)
    mod_s = jnp.repeat(mod[b:b + nseq], t_dec, axis=0).reshape(rows // tms, tms, 3 * d)
    tabs_s = _rope_tables(jnp.tile(PAST_LEN + jnp.arange(t_dec), tms // t_dec))
    outs = proj(xs, mod_s, gpre, w_pad, wa2_pad, ba2, tabs_s, tm=tms)
    q_s, k_s, v_s, ga_s, qg_s, kg_s, la_s, vg_s, gg_s = [o.reshape(rows, o.shape[2]) for o in outs]

    o_att_s, k_win_sample, v_win_sample, og, gla_p = _stream_call(
        q_s, k_s, v_s, cache_k_win, cache_v_win, t_dec, qg, kg, la, vg)
    att = _att_prompt_call(q, k, v)
    og_s, gla_s = _gla_sample_call(qg_s, kg_s, la_s, vg_s, state_gla, t_dec)

    y_prompt = _finish_call(x_prompt, mod_p, att, ga, og, gg, ggla, gpost, wo, tm=tm)
    r3 = lambda a: a.reshape(rows // tms, tms, a.shape[1])
    y_sample = _finish_call(xs, mod_s, r3(o_att_s), r3(ga_s), r3(og_s), r3(gg_s),
                            ggla, gpost, wo, tm=tms).reshape(x_sample.shape)
    n_heads = d_att // HEAD_DIM
    k_win_prompt = jnp.transpose(kt.reshape(b, n_heads, HEAD_DIM, w), (0, 3, 1, 2))
    v_win_prompt = jnp.transpose(vt.reshape(b, n_heads, HEAD_DIM, w), (0, 3, 1, 2))

    return (y_prompt, y_sample, k_win_prompt, v_win_prompt, gla_p,
            k_win_sample, v_win_sample, gla_s)
```

```python
import functools
import math

import numpy as np
import jax
import jax.numpy as jnp
from jax import lax
from jax.experimental import pallas as pl
from jax.experimental.pallas import tpu as pltpu

F32 = jnp.float32
BF16 = jnp.bfloat16

HEAD_DIM = 64
ROT_DIM = HEAD_DIM // 4
ROPE_THETA = 500000.0
DILATIONS = (1, 4, 16)
STEPS = 128
ATT_BLOCK = 128
UNITS_PER_ITER = 16
GATHERS_PER_ITER = 4
PAST_LEN = 8192
GLA_DK = 64
GLA_DV = 128
GLA_GATE_RANK = 16
GLA_TAU = 16.0
EPS = 1e-6
NEG = -1e30
LOG2E = math.log2(math.e)

LANES = 128
SUBLANES = 8
VMEM_LIMIT = 48 * 1024 * 1024


def _cparams(*sem):
    return pltpu.CompilerParams(dimension_semantics=sem, vmem_limit_bytes=VMEM_LIMIT)


def _nt(a, b):
    return lax.dot_general(a, b, (((1,), (1,)), ((), ())), preferred_element_type=F32)


def _split3(x):
    hi = x.astype(BF16)
    r1 = x - hi.astype(F32)
    mid = r1.astype(BF16)
    lo = (r1 - mid.astype(F32)).astype(BF16)
    return hi, mid, lo


def _mod_kernel(c_ref, w_ref, b_ref, o_ref):
    c = c_ref[...]
    a = c / (1.0 + jnp.exp(-c))
    o_ref[...] = jnp.dot(a, w_ref[...], precision=lax.Precision.HIGHEST,
                         preferred_element_type=F32) + b_ref[...]


def _mod_call(c, w_mod, b_mod):
    n, d = c.shape
    d3 = w_mod.shape[1]
    tn = 512
    return pl.pallas_call(
        _mod_kernel,
        grid=(d3 // tn,),
        in_specs=[pl.BlockSpec((n, d), lambda j: (0, 0)),
                  pl.BlockSpec((d, tn), lambda j: (0, j)),
                  pl.BlockSpec((1, tn), lambda j: (0, j))],
        out_specs=pl.BlockSpec((n, tn), lambda j: (0, j)),
        out_shape=jax.ShapeDtypeStruct((n, d3), F32),
        compiler_params=_cparams("arbitrary"),
        name="mod",
    )(c, w_mod, b_mod.reshape(1, d3))


def _rope_tables(pos):
    half = ROT_DIM // 2
    inv = 1.0 / (ROPE_THETA ** (jnp.arange(half, dtype=F32) / half))
    ang = pos.astype(F32)[:, None] * inv[None, :]
    cos, sin = jnp.cos(ang), jnp.sin(ang)
    n = pos.shape[0]
    ones = jnp.ones((n, HEAD_DIM - ROT_DIM), F32)
    zeros_h = jnp.zeros((n, half), F32)
    zeros_r = jnp.zeros((n, HEAD_DIM - ROT_DIM), F32)
    c = jnp.concatenate([cos, cos, ones], axis=1)
    s1 = jnp.concatenate([-sin, zeros_h, zeros_r], axis=1)
    s2 = jnp.concatenate([zeros_h, sin, zeros_r], axis=1)
    rep = LANES // HEAD_DIM
    return tuple(jnp.tile(t, (1, rep)) for t in (c, s1, s2))


def _proj_kernel(x_ref, sh_ref, sc_ref, gpre_ref, w_ref, wa2_ref, ba2_ref,
                 rc_ref, rs1_ref, rs2_ref,
                 q_ref, k_ref, v_ref, ga_ref, qg_ref, kg_ref, la_ref, vg_ref, gg_ref,
                 *win_refs, d_att, d_gk, d_gv):
    x = x_ref[0]
    ms = jnp.mean(x * x, axis=-1, keepdims=True)
    h = x * lax.rsqrt(ms + EPS) * gpre_ref[...]
    h = h * (1.0 + sc_ref[0]) + sh_ref[0]
    hb = h.astype(BF16)

    def seg(a, n):
        return jnp.dot(hb, w_ref[:, a:a + n], preferred_element_type=F32)

    reps = d_att // LANES
    rc = jnp.concatenate([rc_ref[...]] * reps, axis=1)
    rs1 = jnp.concatenate([rs1_ref[...]] * reps, axis=1)
    rs2 = jnp.concatenate([rs2_ref[...]] * reps, axis=1)
    half = ROT_DIM // 2

    def rot(t):
        up = pltpu.roll(t, d_att - half, axis=1)
        dn = pltpu.roll(t, half, axis=1)
        return t * rc + up * rs1 + dn * rs2

    def silu(t):
        return t / (1.0 + jnp.exp(-t))

    o = 0
    q_ref[0] = rot(seg(o, d_att)) * (HEAD_DIM ** -0.5); o += d_att
    k = rot(seg(o, d_att)); o += d_att
    v = seg(o, d_att); o += d_att
    k_ref[0] = k
    v_ref[0] = v
    if win_refs:
        kt_ref, vt_ref = win_refs
        kt_ref[0] = k.T
        vt_ref[0] = v.T
    ga_ref[0] = silu(seg(o, d_att)); o += d_att
    qg_ref[0] = seg(o, d_gk) * (GLA_DK ** -0.5); o += d_gk
    kg_ref[0] = seg(o, d_gk); o += d_gk
    vg_ref[0] = seg(o, d_gv); o += d_gv
    gg_ref[0] = silu(seg(o, d_gv)); o += d_gv
    alr = seg(o, LANES)
    a1, a2, a3 = _split3(alr)
    w1, w2, w3 = _split3(wa2_ref[...])
    dot = functools.partial(jnp.dot, preferred_element_type=F32)
    z = (dot(a1, w1) + dot(a1, w2) + dot(a2, w1)
         + dot(a1, w3) + dot(a2, w2) + dot(a3, w1)) + ba2_ref[...]
    logsig = jnp.minimum(z, 0.0) - jnp.log1p(jnp.exp(-jnp.abs(z)))
    la_ref[0] = logsig * (1.0 / GLA_TAU)


def _proj_call(x3, mod3, gpre, w_pad, wa2_pad, ba2, tables, *, d_att, d_gk, d_gv, tm, win=0):
    g, r, d = x3.shape
    rm = mod3.shape[1]
    per_row = rm != 1
    grid = (g, r // tm)
    if per_row:
        mod_spec = lambda c: pl.BlockSpec((1, tm, d), lambda i, j: (i, j, c))
    else:
        mod_spec = lambda c: pl.BlockSpec((1, 1, d), lambda i, j: (i, 0, c))
    tab_rows = tables[0].shape[0]
    if tab_rows == tm:
        tab_spec = pl.BlockSpec((tm, LANES), lambda i, j: (0, 0))
    else:
        tab_spec = pl.BlockSpec((tm, LANES), lambda i, j: (j, 0))
    full = lambda a: pl.BlockSpec(a.shape, lambda i, j: (0,) * a.ndim)
    widths = (d_att, d_att, d_att, d_att, d_gk, d_gk, d_gk, d_gv, d_gv)
    out_specs = [pl.BlockSpec((1, tm, w), lambda i, j: (i, j, 0)) for w in widths]
    out_shape = [jax.ShapeDtypeStruct((g, r, w), F32) for w in widths]
    first_win_tile = (r - win) // tm
    if win:
        wspec = pl.BlockSpec((1, d_att, tm), lambda i, j: (i, 0, jnp.maximum(j - first_win_tile, 0)))
        out_specs += [wspec, wspec]
        out_shape += [jax.ShapeDtypeStruct((g, d_att, win), F32)] * 2
    kern = functools.partial(_proj_kernel, d_att=d_att, d_gk=d_gk, d_gv=d_gv)
    return pl.pallas_call(
        kern,
        grid=grid,
        in_specs=[pl.BlockSpec((1, tm, d), lambda i, j: (i, j, 0)),
                  mod_spec(0), mod_spec(1), full(gpre), full(w_pad), full(wa2_pad), full(ba2),
                  tab_spec, tab_spec, tab_spec],
        out_specs=out_specs,
        out_shape=out_shape,
        compiler_params=_cparams("parallel", "arbitrary"),
        name="proj",
    )(x3, mod3, mod3, gpre, w_pad, wa2_pad, ba2, *tables)


def _band_bias():
    row = np.arange(ATT_BLOCK)[:, None]
    col = np.arange(2 * ATT_BLOCK)[None, :]
    dist = row + ATT_BLOCK - col
    band = (dist >= 0) & (dist <= STEPS)
    first = band & (col >= ATT_BLOCK)
    return jnp.asarray(np.where(np.stack([first, band]), 0.0, NEG), F32)


def _att_buf_rows(seq):
    offs, total = {}, 0
    for dil in DILATIONS:
        offs[dil] = total
        total += seq + dil * ATT_BLOCK
    return offs, total


def _att_prompt_kernel(q_ref, k_ref, v_ref, bias_ref, o_ref,
                       qb_ref, kb_ref, vb_ref, acc_ref, m_ref, l_ref, *, seq):
    blk = ATT_BLOCK
    n_units = seq // blk
    offs, _ = _att_buf_rows(seq)
    lane = lax.broadcasted_iota(jnp.int32, (blk, LANES), 1)
    lo = lane < HEAD_DIM
    head_masks = (jnp.where(lo, 1.0, 0.0).astype(BF16), jnp.where(lo, 0.0, 1.0).astype(BF16))
    ones = jnp.ones((2 * blk, LANES), BF16)
    zeros = jnp.zeros((blk, LANES), BF16)

    def place(u, dil):
        r = lax.rem(u, dil)
        n = lax.div(u, dil)
        first = r + dil * blk * n
        base = offs[dil] + r * (seq // dil + blk) + blk * n
        return n, first, pl.multiple_of(base, blk)

    def rows(first, dil):
        return pl.ds(first, blk) if dil == 1 else pl.ds(first, blk, stride=dil)

    for dil in DILATIONS:
        for r in range(dil):
            z0 = offs[dil] + r * (seq // dil + blk)
            kb_ref[z0:z0 + blk, :] = zeros
            vb_ref[z0:z0 + blk, :] = zeros

        def gather(u, carry, dil=dil):
            _, first, base = place(u, dil)
            dst = pl.ds(base + blk, blk)
            qb_ref[dst, :] = (q_ref[0, rows(first, dil), :] * LOG2E).astype(BF16)
            kb_ref[dst, :] = k_ref[0, rows(first, dil), :].astype(BF16)
            vb_ref[dst, :] = v_ref[0, rows(first, dil), :].astype(BF16)
            return carry

        lax.fori_loop(0, n_units, gather, 0, unroll=GATHERS_PER_ITER)

    order = tuple(reversed(DILATIONS))
    for dil in order:
        def unit(u, carry, dil=dil):
            n, first, base = place(u, dil)
            qb = qb_ref[pl.ds(base + blk, blk), :]
            kcat = kb_ref[pl.ds(base, 2 * blk), :]
            vext = jnp.concatenate([vb_ref[pl.ds(base, 2 * blk), :], ones], axis=1)
            bias = bias_ref[jnp.minimum(n, 1)]
            os, ms, ls = [], [], []
            for hm in head_masks:
                s = _nt(qb * hm, kcat) + bias
                m = jnp.max(s, axis=-1, keepdims=True)
                p = jnp.exp2(s - m)
                res = jnp.dot(p.astype(BF16), vext, preferred_element_type=F32)
                os.append(res[:, :LANES])
                ls.append(res[:, LANES:])
                ms.append(m)
            o_new = jnp.where(lo, os[0], os[1])
            m_new = jnp.where(lo, ms[0], ms[1])
            l_new = jnp.where(lo, ls[0], ls[1])
            dst = rows(first, dil)
            if dil == order[0]:
                acc_ref[dst, :] = o_new
                m_ref[dst, :] = m_new
                l_ref[dst, :] = l_new
                return carry
            m_old = m_ref[dst, :]
            m_tot = jnp.maximum(m_old, m_new)
            a_old = jnp.exp2(m_old - m_tot)
            a_new = jnp.exp2(m_new - m_tot)
            acc = a_old * acc_ref[dst, :] + a_new * o_new
            den = a_old * l_ref[dst, :] + a_new * l_new
            if dil == order[-1]:
                o_ref[0, dst, :] = acc / den
            else:
                acc_ref[dst, :] = acc
                l_ref[dst, :] = den
                m_ref[dst, :] = m_tot
            return carry

        lax.fori_loop(0, n_units, unit, 0, unroll=UNITS_PER_ITER)


def _att_prompt_call(q, k, v):
    b, s, d_att = q.shape
    bias = _band_bias()
    _, buf_rows = _att_buf_rows(s)
    slab = pl.BlockSpec((1, s, LANES), lambda i, j: (i, 0, j))
    kern = functools.partial(_att_prompt_kernel, seq=s)
    return pl.pallas_call(
        kern,
        grid=(b, d_att // LANES),
        in_specs=[slab, slab, slab, pl.BlockSpec(bias.shape, lambda i, j: (0, 0, 0))],
        out_specs=slab,
        out_shape=jax.ShapeDtypeStruct((b, s, d_att), F32),
        scratch_shapes=[pltpu.VMEM((buf_rows, LANES), BF16)] * 3 + [pltpu.VMEM((s, LANES), F32)] * 3,
        compiler_params=_cparams("parallel", "parallel"),
        name="att_prompt",
    )(q, k, v, bias)


def _gla_consts(blk):
    t = np.arange(LANES)
    same = (t[:, None] // blk) == (t[None, :] // blk)
    tri = ((t[None, :] <= t[:, None]) & same).astype(np.int64)
    mats, lvl, right_rows = [tri], [], []
    h = blk // 2
    while h >= 1:
        pair = t // (2 * h)
        right = (t // h) % 2 == 1
        mid = pair * 2 * h + h - 1
        upto_mid = ((t[None, :] <= mid[:, None]) & same).astype(np.int64)
        mats.append(tri - upto_mid)
        lvl.append(pair[:, None] == pair[None, :])
        right_rows.append(np.broadcast_to(right[:, None], (LANES, LANES)))
        h //= 2
    lvl.append(np.eye(LANES, dtype=bool))
    if blk < LANES:
        mats.append(same.astype(np.int64))
    cmat = jnp.asarray(np.concatenate(mats, 0), BF16)
    lvl = jnp.asarray(np.concatenate(lvl, 0), F32)
    rmask = jnp.asarray(np.concatenate(right_rows, 0), F32)
    return cmat, lvl, rmask, len(right_rows)


def _gla_tile(q, k, la, cmat_ref, lvl_ref, rmask_ref, n_lvl):
    t = LANES
    hi, mid, lo3 = _split3(la)
    cm = cmat_ref[...]
    dot = functools.partial(jnp.dot, preferred_element_type=F32)
    g = dot(cm, hi) + dot(cm, mid) + dot(cm, lo3)
    b = g[0:t]
    if cm.shape[0] > (n_lvl + 1) * t:
        btot = g[(n_lvl + 1) * t:(n_lvl + 2) * t]
    else:
        btot = jnp.broadcast_to(b[t - 1:t, :], b.shape)
    lane = lax.broadcasted_iota(jnp.int32, (t, LANES), 1)
    lo = lane < GLA_DK
    pairs = []
    for j in range(q.shape[1] // LANES):
        sl = slice(j * LANES, (j + 1) * LANES)
        qp, kp, bp = q[:, sl], k[:, sl], b[:, sl]
        a0 = jnp.zeros((t, t), F32)
        a1 = jnp.zeros((t, t), F32)
        for lv in range(n_lvl + 1):
            m = lvl_ref[lv * t:(lv + 1) * t, :]
            if lv < n_lvl:
                dl = g[(lv + 1) * t:(lv + 2) * t, sl]
                e = jnp.exp(-jnp.abs(dl))
                rm = rmask_ref[lv * t:(lv + 1) * t, :]
                x = qp * e * rm
                y = (kp * e * (1.0 - rm)).astype(BF16)
            else:
                x = qp
                y = kp.astype(BF16)
            x0 = jnp.where(lo, x, 0.0).astype(BF16)
            x1 = jnp.where(lo, 0.0, x).astype(BF16)
            a0 = a0 + m * _nt(x0, y)
            a1 = a1 + m * _nt(x1, y)
        q_in = qp * jnp.exp(bp)
        k_in = kp * jnp.exp(btot[:, sl] - bp)
        pairs.append((a0, a1, q_in, k_in, btot[:, sl], lo))
    return pairs


def _gla_prompt_body(q_ref, k_ref, la_ref, v_ref, cmat_ref, lvl_ref, rmask_ref, o_ref, s_ref,
                     *, n_lvl):
    pairs = _gla_tile(q_ref[0], k_ref[0], la_ref[0], cmat_ref, lvl_ref, rmask_ref, n_lvl)
    t = LANES
    row = lax.broadcasted_iota(jnp.int32, (t, t), 0)
    top = row < GLA_DK
    for j, (a0, a1, q_in, k_in, btot, lo) in enumerate(pairs):
        s_prev = s_ref[j]
        sb = s_prev.astype(BF16)
        kt = k_in.T.astype(BF16)
        us = []
        for hh, (a, sel) in enumerate(((a0, lo), (a1, ~lo))):
            vs = slice((2 * j + hh) * GLA_DV, (2 * j + hh + 1) * GLA_DV)
            vb = v_ref[0, :, vs].astype(BF16)
            qm = jnp.where(sel, q_in, 0.0).astype(BF16)
            o = (jnp.dot(a.astype(BF16), vb, preferred_element_type=F32)
                 + jnp.dot(qm, sb, preferred_element_type=F32))
            o_ref[0, :, vs] = o
            us.append(jnp.dot(kt, vb, preferred_element_type=F32))
        a_col = jnp.exp(btot.T)
        s_ref[j] = a_col * s_prev + jnp.where(top, us[0], us[1])


def _gla_sample_kernel(q_ref, k_ref, la_ref, v_ref, s_in_ref, cmat_ref, lvl_ref, rmask_ref,
                       o_ref, s_out_ref, *, n_lvl, t_dec):
    pairs = _gla_tile(q_ref[...], k_ref[...], la_ref[...], cmat_ref, lvl_ref, rmask_ref, n_lvl)
    t = LANES
    nb = t // t_dec
    row = lax.broadcasted_iota(jnp.int32, (t, t), 0)
    col = lax.broadcasted_iota(jnp.int32, (t, t), 1)
    top = row < GLA_DK
    for j, (a0, a1, q_in, k_in, btot, lo) in enumerate(pairs):
        s_prev = s_in_ref[:, j]
        sb = s_prev.astype(BF16)
        kt = k_in.T
        bt = btot.T
        vbs = []
        for hh, (a, sel) in enumerate(((a0, lo), (a1, ~lo))):
            vs = slice((2 * j + hh) * GLA_DV, (2 * j + hh + 1) * GLA_DV)
            vb = v_ref[:, vs].astype(BF16)
            vbs.append(vb)
            qm = jnp.where(sel, q_in, 0.0)
            o_inter = jnp.concatenate(
                [jnp.dot(qm[i * t_dec:(i + 1) * t_dec].astype(BF16), sb[i],
                         preferred_element_type=F32) for i in range(nb)], axis=0)
            o_ref[:, vs] = jnp.dot(a.astype(BF16), vb, preferred_element_type=F32) + o_inter
        for i in range(nb):
            mine = (col >= i * t_dec) & (col < (i + 1) * t_dec)
            kti = (kt * mine.astype(F32)).astype(BF16)
            u0 = jnp.dot(kti, vbs[0], preferred_element_type=F32)
            u1 = jnp.dot(kti, vbs[1], preferred_element_type=F32)
            a_col = jnp.exp(jnp.broadcast_to(bt[:, i * t_dec:i * t_dec + 1], (t, GLA_DV)))
            s_out_ref[i, j] = a_col * s_prev[i] + jnp.where(top, u0, u1)


def _gla_sample_call(qg, kg, la, vg, state, t_dec):
    rows, d_gk = qg.shape
    d_gv = vg.shape[1]
    n_pair = d_gk // LANES
    nseq = rows // t_dec
    t = LANES
    nb = t // t_dec
    cmat, lvl, rmask, n_lvl = _gla_consts(t_dec)
    s_in = state.reshape(nseq, n_pair, 2 * GLA_DK, GLA_DV)
    tok = lambda w: pl.BlockSpec((t, w), lambda i: (i, 0))
    full = lambda a: pl.BlockSpec(a.shape, lambda i: (0,) * a.ndim)
    st = pl.BlockSpec((nb, n_pair, t, GLA_DV), lambda i: (i, 0, 0, 0))
    kern = functools.partial(_gla_sample_kernel, n_lvl=n_lvl, t_dec=t_dec)
    o, s_out = pl.pallas_call(
        kern,
        grid=(rows // t,),
        in_specs=[tok(d_gk), tok(d_gk), tok(d_gk), tok(d_gv), st, full(cmat), full(lvl), full(rmask)],
        out_specs=[tok(d_gv), st],
        out_shape=[jax.ShapeDtypeStruct((rows, d_gv), F32),
                   jax.ShapeDtypeStruct(s_in.shape, F32)],
        compiler_params=_cparams("parallel"),
        name="gla_sample",
    )(qg, kg, la, vg, s_in, cmat, lvl, rmask)
    return o, s_out.reshape(state.shape)


def _sample_att_table(w, t_dec):
    idx = np.arange(w + LANES)[None, :]
    tq = np.arange(t_dec)[:, None]
    dist = w + tq - idx
    mult = np.zeros((t_dec, w + LANES), np.int64)
    for d in DILATIONS:
        mult += (dist >= 0) & (dist % d == 0) & (dist // d <= STEPS) & (idx < w + t_dec)
    logw = np.where(mult > 0, np.log(np.maximum(mult, 1)), NEG).astype(np.float32)
    return jnp.asarray(logw)


def _att_sample_body(q_ref, kn_ref, vn_ref, ck_ref, cv_ref, logw_ref,
                     o_ref, ko_ref, vo_ref, *, w, t_dec, n_heads):
    lane = lax.broadcasted_iota(jnp.int32, (HEAD_DIM, LANES), 1)
    keep = lane < LANES - t_dec
    zpad = jnp.zeros((LANES - t_dec, n_heads * HEAD_DIM), F32)
    knt = jnp.concatenate([kn_ref[...], zpad], axis=0).T
    vnt = jnp.concatenate([vn_ref[...], zpad], axis=0).T
    q = q_ref[...]
    logw_c = logw_ref[:, 0:w]
    logw_n = logw_ref[:, w:w + LANES]
    outs = []
    for h in range(n_heads):
        hs = slice(h * HEAD_DIM, (h + 1) * HEAD_DIM)
        kt = ck_ref[0, h]
        vt = cv_ref[0, h]
        knh = knt[hs, :]
        vnh = vnt[hs, :]
        qh = q[:, hs].astype(BF16)
        s_c = jnp.dot(qh, kt.astype(BF16), preferred_element_type=F32) + logw_c
        s_n = jnp.dot(qh, knh.astype(BF16), preferred_element_type=F32) + logw_n
        m = jnp.maximum(jnp.max(s_c, axis=-1, keepdims=True), jnp.max(s_n, axis=-1, keepdims=True))
        p_c = jnp.exp(s_c - m)
        p_n = jnp.exp(s_n - m)
        l = jnp.sum(p_c, axis=-1, keepdims=True) + jnp.sum(p_n, axis=-1, keepdims=True)
        o = _nt(p_c.astype(BF16), vt.astype(BF16)) + _nt(p_n.astype(BF16), vnh.astype(BF16))
        outs.append(o / l)
        for src, new, dst in ((kt, knh, ko_ref), (vt, vnh, vo_ref)):
            sh = pltpu.roll(src, w - t_dec, axis=1)
            tail = jnp.where(keep, sh[:, w - LANES:], pltpu.roll(new, LANES - t_dec, axis=1))
            dst[0, h, :, 0:w - LANES] = sh[:, 0:w - LANES]
            dst[0, h, :, w - LANES:w] = tail
    o_ref[...] = jnp.concatenate(outs, axis=1)


def _stream_kernel(q_ref, kn_ref, vn_ref, ck_ref, cv_ref, logw_ref,
                   gq_ref, gk_ref, gla_ref, gv_ref, cmat_ref, lvl_ref, rmask_ref,
                   o_ref, ko_ref, vo_ref, og_ref, s_out_ref, s_ref,
                   *, w, t_dec, n_heads, n_lvl, n_tiles):
    n = lax.rem(pl.program_id(0), n_tiles)

    @pl.when(n == 0)
    def _():
        s_ref[...] = jnp.zeros_like(s_ref)

    _att_sample_body(q_ref, kn_ref, vn_ref, ck_ref, cv_ref, logw_ref, o_ref, ko_ref, vo_ref,
                     w=w, t_dec=t_dec, n_heads=n_heads)
    _gla_prompt_body(gq_ref, gk_ref, gla_ref, gv_ref, cmat_ref, lvl_ref, rmask_ref, og_ref, s_ref,
                     n_lvl=n_lvl)

    @pl.when(n == n_tiles - 1)
    def _():
        s_out_ref[0] = s_ref[...]


def _stream_call(q, kn, vn, cache_k, cache_v, t_dec, qg, kg, la, vg):
    nseq, w, n_heads, hd = cache_k.shape
    d_att = n_heads * hd
    b, s, d_gk = qg.shape
    d_gv = vg.shape[2]
    t = LANES
    n_tiles = s // t
    n_pair = d_gk // LANES
    assert nseq == b * n_tiles, "one prompt GLA tile is paired with each sample sequence"
    ck = jnp.transpose(cache_k, (0, 2, 3, 1))
    cv = jnp.transpose(cache_v, (0, 2, 3, 1))
    logw = _sample_att_table(w, t_dec)
    cmat, lvl, rmask, n_lvl = _gla_consts(LANES)
    tok = pl.BlockSpec((t_dec, d_att), lambda i: (i, 0))
    win = pl.BlockSpec((1, n_heads, hd, w), lambda i: (i, 0, 0, 0))
    full = lambda a: pl.BlockSpec(a.shape, lambda i: (0,) * a.ndim)
    gtok = lambda wd: pl.BlockSpec((1, t, wd), lambda i: (i // n_tiles, i % n_tiles, 0))
    kern = functools.partial(_stream_kernel, w=w, t_dec=t_dec, n_heads=n_heads,
                             n_lvl=n_lvl, n_tiles=n_tiles)
    o, ko, vo, og, s_out = pl.pallas_call(
        kern,
        grid=(nseq,),
        in_specs=[tok, tok, tok, win, win, full(logw),
                  gtok(d_gk), gtok(d_gk), gtok(d_gk), gtok(d_gv), full(cmat), full(lvl), full(rmask)],
        out_specs=[tok, win, win, gtok(d_gv),
                   pl.BlockSpec((1, n_pair, t, GLA_DV), lambda i: (i // n_tiles, 0, 0, 0))],
        out_shape=[jax.ShapeDtypeStruct((nseq * t_dec, d_att), F32),
                   jax.ShapeDtypeStruct(ck.shape, F32),
                   jax.ShapeDtypeStruct(cv.shape, F32),
                   jax.ShapeDtypeStruct((b, s, d_gv), F32),
                   jax.ShapeDtypeStruct((b, n_pair, t, GLA_DV), F32)],
        scratch_shapes=[pltpu.VMEM((n_pair, t, GLA_DV), F32)],
        compiler_params=_cparams("arbitrary"),
        name="stream",
    )(q, kn, vn, ck, cv, logw, qg, kg, la, vg, cmat, lvl, rmask)
    return (o, jnp.transpose(ko, (0, 3, 1, 2)), jnp.transpose(vo, (0, 3, 1, 2)),
            og, s_out.reshape(b, 2 * n_pair, GLA_DK, GLA_DV))


def _finish_kernel(x_ref, gate_ref, att_ref, ga_ref, og_ref, gg_ref, ggla_ref, gpost_ref, wo_ref,
                   y_ref, *, d_att, d_gv):
    att = att_ref[0] * ga_ref[0]
    og = og_ref[0]
    gg = gg_ref[0]
    y = jnp.dot(att.astype(BF16), wo_ref[0:d_att, :], preferred_element_type=F32)
    for h in range(d_gv // GLA_DV):
        sl = slice(h * GLA_DV, (h + 1) * GLA_DV)
        oh = og[:, sl]
        nh = oh * lax.rsqrt(jnp.mean(oh * oh, axis=-1, keepdims=True) + EPS) * ggla_ref[...]
        gl = (nh * gg[:, sl]).astype(BF16)
        y = y + jnp.dot(gl, wo_ref[d_att + h * GLA_DV:d_att + (h + 1) * GLA_DV, :],
                        preferred_element_type=F32)
    yn = y * lax.rsqrt(jnp.mean(y * y, axis=-1, keepdims=True) + EPS) * gpost_ref[...]
    y_ref[0] = x_ref[0] + gate_ref[0] * yn


def _finish_call(x3, mod3, att, ga, og, gg, g_gla, g_post, w_out, *, tm):
    g, r, d = x3.shape
    d_att = ga.shape[2]
    d_gv = og.shape[2]
    per_row = mod3.shape[1] != 1
    if per_row:
        gate_spec = pl.BlockSpec((1, tm, d), lambda i, j: (i, j, 2))
    else:
        gate_spec = pl.BlockSpec((1, 1, d), lambda i, j: (i, 0, 2))
    tok = lambda w: pl.BlockSpec((1, tm, w), lambda i, j: (i, j, 0))
    full = lambda a: pl.BlockSpec(a.shape, lambda i, j: (0,) * a.ndim)
    kern = functools.partial(_finish_kernel, d_att=d_att, d_gv=d_gv)
    return pl.pallas_call(
        kern,
        grid=(g, r // tm),
        in_specs=[tok(d), gate_spec, tok(d_att), tok(d_att), tok(d_gv), tok(d_gv),
                  full(g_gla), full(g_post), full(w_out)],
        out_specs=tok(d),
        out_shape=jax.ShapeDtypeStruct((g, r, d), F32),
        compiler_params=_cparams("parallel", "arbitrary"),
        name="finish",
    )(x3, mod3, att, ga, og, gg, g_gla, g_post, w_out)


def kernel(x_prompt, x_sample, c_prompt, c_sample, cache_k_win, cache_v_win, state_gla,
           w_mod, b_mod, g_pre, g_post, w_in, w_a2, b_a2, g_gla, w_out):
    b, s, d = x_prompt.shape
    nseq, t_dec, _ = x_sample.shape
    w = cache_k_win.shape[1]
    d_att = cache_k_win.shape[2] * cache_k_win.shape[3]
    d_gk = w_a2.shape[1]
    d_gv = state_gla.shape[1] * state_gla.shape[3]
    d_main = 4 * d_att + 2 * d_gk + 2 * d_gv
    assert w_in.shape[1] == d_main + GLA_GATE_RANK and cache_k_win.shape[3] == HEAD_DIM
    assert state_gla.shape[2] == GLA_DK and state_gla.shape[3] == GLA_DV
    assert w >= STEPS * max(DILATIONS) and s % (ATT_BLOCK * max(DILATIONS)) == 0 and s >= w

    w_pad = jnp.pad(w_in, ((0, 0), (0, LANES - GLA_GATE_RANK))).astype(BF16)
    wa2_pad = jnp.pad(w_a2, ((0, LANES - GLA_GATE_RANK), (0, 0)))
    wo = w_out.astype(BF16)
    gpre = g_pre.reshape(1, d)
    gpost = g_post.reshape(1, d)
    ggla = g_gla.reshape(1, GLA_DV)
    ba2 = b_a2.reshape(1, d_gk)

    n_c = b + nseq
    n_c_pad = -(-n_c // SUBLANES) * SUBLANES
    c_all = jnp.pad(jnp.concatenate([c_prompt, c_sample], axis=0), ((0, n_c_pad - n_c), (0, 0)))
    mod = _mod_call(c_all, w_mod, b_mod)
    mod_p = mod[:b].reshape(b, 1, 3 * d)

    proj = functools.partial(_proj_call, d_att=d_att, d_gk=d_gk, d_gv=d_gv)

    tm = 512
    tabs_p = _rope_tables(jnp.arange(s))
    q, k, v, ga, qg, kg, la, vg, gg, kt, vt = proj(x_prompt, mod_p, gpre, w_pad, wa2_pad, ba2, tabs_p,
                                                   tm=tm, win=w)
    rows = nseq * t_dec
    tms = min(512, rows)
    xs = x_sample.reshape(rows // tms, tms, d)
    mod_s = jnp.repeat(mod[b:b + nseq], t_dec, axis=0).reshape(rows // tms, tms, 3 * d)
    tabs_s = _rope_tables(jnp.tile(PAST_LEN + jnp.arange(t_dec), tms // t_dec))
    outs = proj(xs, mod_s, gpre, w_pad, wa2_pad, ba2, tabs_s, tm=tms)
    q_s, k_s, v_s, ga_s, qg_s, kg_s, la_s, vg_s, gg_s = [o.reshape(rows, o.shape[2]) for o in outs]

    o_att_s, k_win_sample, v_win_sample, og, gla_p = _stream_call(
        q_s, k_s, v_s, cache_k_win, cache_v_win, t_dec, qg, kg, la, vg)
    att = _att_prompt_call(q, k, v)
    og_s, gla_s = _gla_sample_call(qg_s, kg_s, la_s, vg_s, state_gla, t_dec)

    y_prompt = _finish_call(x_prompt, mod_p, att, ga, og, gg, ggla, gpost, wo, tm=tm)
    r3 = lambda a: a.reshape(rows // tms, tms, a.shape[1])
    y_sample = _finish_call(xs, mod_s, r3(o_att_s), r3(ga_s), r3(og_s), r3(gg_s),
                            ggla, gpost, wo, tm=tms).reshape(x_sample.shape)
    n_heads = d_att // HEAD_DIM
    k_win_prompt = jnp.transpose(kt.reshape(b, n_heads, HEAD_DIM, w), (0, 3, 1, 2))
    v_win_prompt = jnp.transpose(vt.reshape(b, n_heads, HEAD_DIM, w), (0, 3, 1, 2))

    return (y_prompt, y_sample, k_win_prompt, v_win_prompt, gla_p,
            k_win_sample, v_win_sample, gla_s)
```

```python
import functools
import math

import numpy as np
import jax
import jax.numpy as jnp
from jax import lax
from jax.experimental import pallas as pl
from jax.experimental.pallas import tpu as pltpu

F32 = jnp.float32
BF16 = jnp.bfloat16

HEAD_DIM = 64
ROT_DIM = HEAD_DIM // 4
ROPE_THETA = 500000.0
DILATIONS = (1, 4, 16)
STEPS = 128
ATT_BLOCK = 128
UNITS_PER_ITER = 16
GATHERS_PER_ITER = 4
PAST_LEN = 8192
GLA_DK = 64
GLA_DV = 128
GLA_GATE_RANK = 16
GLA_TAU = 16.0
EPS = 1e-6
NEG = -1e30
LOG2E = math.log2(math.e)

LANES = 128
SUBLANES = 8
VMEM_LIMIT = 48 * 1024 * 1024
STREAM_VMEM_LIMIT = 56 * 1024 * 1024


def _cparams(*sem):
    return pltpu.CompilerParams(dimension_semantics=sem, vmem_limit_bytes=VMEM_LIMIT)


def _nt(a, b):
    return lax.dot_general(a, b, (((1,), (1,)), ((), ())), preferred_element_type=F32)


def _split3(x):
    hi = x.astype(BF16)
    r1 = x - hi.astype(F32)
    mid = r1.astype(BF16)
    lo = (r1 - mid.astype(F32)).astype(BF16)
    return hi, mid, lo


def _mod_kernel(c_ref, w_ref, b_ref, o_ref):
    c = c_ref[...]
    a = c / (1.0 + jnp.exp(-c))
    o_ref[...] = jnp.dot(a, w_ref[...], precision=lax.Precision.HIGHEST,
                         preferred_element_type=F32) + b_ref[...]


def _mod_call(c, w_mod, b_mod):
    n, d = c.shape
    d3 = w_mod.shape[1]
    tn = 512
    return pl.pallas_call(
        _mod_kernel,
        grid=(d3 // tn,),
        in_specs=[pl.BlockSpec((n, d), lambda j: (0, 0)),
                  pl.BlockSpec((d, tn), lambda j: (0, j)),
                  pl.BlockSpec((1, tn), lambda j: (0, j))],
        out_specs=pl.BlockSpec((n, tn), lambda j: (0, j)),
        out_shape=jax.ShapeDtypeStruct((n, d3), F32),
        compiler_params=_cparams("arbitrary"),
        name="mod",
    )(c, w_mod, b_mod.reshape(1, d3))


def _rope_tables(pos):
    half = ROT_DIM // 2
    inv = 1.0 / (ROPE_THETA ** (jnp.arange(half, dtype=F32) / half))
    ang = pos.astype(F32)[:, None] * inv[None, :]
    cos, sin = jnp.cos(ang), jnp.sin(ang)
    n = pos.shape[0]
    ones = jnp.ones((n, HEAD_DIM - ROT_DIM), F32)
    zeros_h = jnp.zeros((n, half), F32)
    zeros_r = jnp.zeros((n, HEAD_DIM - ROT_DIM), F32)
    c = jnp.concatenate([cos, cos, ones], axis=1)
    s1 = jnp.concatenate([-sin, zeros_h, zeros_r], axis=1)
    s2 = jnp.concatenate([zeros_h, sin, zeros_r], axis=1)
    rep = LANES // HEAD_DIM
    return tuple(jnp.tile(t, (1, rep)) for t in (c, s1, s2))


def _proj_body(x, sh, sc, gpre_ref, w_ref, wa2_ref, ba2_ref, rc_ref, rs1_ref, rs2_ref,
               *, d_att, d_gk, d_gv):
    ms = jnp.mean(x * x, axis=-1, keepdims=True)
    h = x * lax.rsqrt(ms + EPS) * gpre_ref[...]
    h = h * (1.0 + sc) + sh
    hb = h.astype(BF16)

    def seg(a, n):
        return jnp.dot(hb, w_ref[:, a:a + n], preferred_element_type=F32)

    reps = d_att // LANES
    rc = jnp.concatenate([rc_ref[...]] * reps, axis=1)
    rs1 = jnp.concatenate([rs1_ref[...]] * reps, axis=1)
    rs2 = jnp.concatenate([rs2_ref[...]] * reps, axis=1)
    half = ROT_DIM // 2

    def rot(t):
        up = pltpu.roll(t, d_att - half, axis=1)
        dn = pltpu.roll(t, half, axis=1)
        return t * rc + up * rs1 + dn * rs2

    def silu(t):
        return t / (1.0 + jnp.exp(-t))

    o = 0
    q = rot(seg(o, d_att)) * (HEAD_DIM ** -0.5); o += d_att
    k = rot(seg(o, d_att)); o += d_att
    v = seg(o, d_att); o += d_att
    ga = silu(seg(o, d_att)); o += d_att
    qg = seg(o, d_gk) * (GLA_DK ** -0.5); o += d_gk
    kg = seg(o, d_gk); o += d_gk
    vg = seg(o, d_gv); o += d_gv
    gg = silu(seg(o, d_gv)); o += d_gv
    alr = seg(o, LANES)
    a1, a2, a3 = _split3(alr)
    w1, w2, w3 = _split3(wa2_ref[...])
    dot = functools.partial(jnp.dot, preferred_element_type=F32)
    z = (dot(a1, w1) + dot(a1, w2) + dot(a2, w1)
         + dot(a1, w3) + dot(a2, w2) + dot(a3, w1)) + ba2_ref[...]
    logsig = jnp.minimum(z, 0.0) - jnp.log1p(jnp.exp(-jnp.abs(z)))
    la = logsig * (1.0 / GLA_TAU)
    return q, k, v, ga, qg, kg, la, vg, gg


def _proj_kernel(x_ref, sh_ref, sc_ref, gpre_ref, w_ref, wa2_ref, ba2_ref,
                 rc_ref, rs1_ref, rs2_ref, *out_refs, d_att, d_gk, d_gv):
    vals = _proj_body(x_ref[0], sh_ref[0], sc_ref[0], gpre_ref, w_ref, wa2_ref, ba2_ref,
                      rc_ref, rs1_ref, rs2_ref, d_att=d_att, d_gk=d_gk, d_gv=d_gv)
    for ref, val in zip(out_refs, vals):
        ref[0] = val


def _proj_call(x3, mod3, gpre, w_pad, wa2_pad, ba2, tables, *, d_att, d_gk, d_gv, tm):
    g, r, d = x3.shape
    grid = (g, r // tm)
    mod_spec = lambda c: pl.BlockSpec((1, tm, d), lambda i, j: (i, j, c))
    tab_spec = pl.BlockSpec((tm, LANES), lambda i, j: (0, 0))
    full = lambda a: pl.BlockSpec(a.shape, lambda i, j: (0,) * a.ndim)
    widths = (d_att, d_att, d_att, d_att, d_gk, d_gk, d_gk, d_gv, d_gv)
    kern = functools.partial(_proj_kernel, d_att=d_att, d_gk=d_gk, d_gv=d_gv)
    return pl.pallas_call(
        kern,
        grid=grid,
        in_specs=[pl.BlockSpec((1, tm, d), lambda i, j: (i, j, 0)),
                  mod_spec(0), mod_spec(1), full(gpre), full(w_pad), full(wa2_pad), full(ba2),
                  tab_spec, tab_spec, tab_spec],
        out_specs=[pl.BlockSpec((1, tm, w), lambda i, j: (i, j, 0)) for w in widths],
        out_shape=[jax.ShapeDtypeStruct((g, r, w), F32) for w in widths],
        compiler_params=_cparams("parallel", "arbitrary"),
        name="proj",
    )(x3, mod3, mod3, gpre, w_pad, wa2_pad, ba2, *tables)


def _band_bias():
    row = np.arange(ATT_BLOCK)[:, None]
    col = np.arange(2 * ATT_BLOCK)[None, :]
    dist = row + ATT_BLOCK - col
    band = (dist >= 0) & (dist <= STEPS)
    first = band & (col >= ATT_BLOCK)
    return jnp.asarray(np.where(np.stack([first, band]), 0.0, NEG), F32)


def _att_buf_rows(seq):
    offs, total = {}, 0
    for dil in DILATIONS:
        offs[dil] = total
        total += seq + dil * ATT_BLOCK
    return offs, total


def _att_prompt_kernel(q_ref, k_ref, v_ref, bias_ref, o_ref,
                       qb_ref, kb_ref, vb_ref, acc_ref, m_ref, l_ref, *, seq):
    blk = ATT_BLOCK
    n_units = seq // blk
    offs, _ = _att_buf_rows(seq)
    lane = lax.broadcasted_iota(jnp.int32, (blk, LANES), 1)
    lo = lane < HEAD_DIM
    head_masks = (jnp.where(lo, 1.0, 0.0).astype(BF16), jnp.where(lo, 0.0, 1.0).astype(BF16))
    ones = jnp.ones((2 * blk, LANES), BF16)
    zeros = jnp.zeros((blk, LANES), BF16)

    def place(u, dil):
        r = lax.rem(u, dil)
        n = lax.div(u, dil)
        first = r + dil * blk * n
        base = offs[dil] + r * (seq // dil + blk) + blk * n
        return n, first, pl.multiple_of(base, blk)

    def rows(first, dil):
        return pl.ds(first, blk) if dil == 1 else pl.ds(first, blk, stride=dil)

    for dil in DILATIONS:
        for r in range(dil):
            z0 = offs[dil] + r * (seq // dil + blk)
            kb_ref[z0:z0 + blk, :] = zeros
            vb_ref[z0:z0 + blk, :] = zeros

        def gather(u, carry, dil=dil):
            _, first, base = place(u, dil)
            dst = pl.ds(base + blk, blk)
            qb_ref[dst, :] = (q_ref[0, rows(first, dil), :] * LOG2E).astype(BF16)
            kb_ref[dst, :] = k_ref[0, rows(first, dil), :].astype(BF16)
            vb_ref[dst, :] = v_ref[0, rows(first, dil), :].astype(BF16)
            return carry

        lax.fori_loop(0, n_units, gather, 0, unroll=GATHERS_PER_ITER)

    order = tuple(reversed(DILATIONS))
    for dil in order:
        def unit(u, carry, dil=dil):
            n, first, base = place(u, dil)
            qb = qb_ref[pl.ds(base + blk, blk), :]
            kcat = kb_ref[pl.ds(base, 2 * blk), :]
            vext = jnp.concatenate([vb_ref[pl.ds(base, 2 * blk), :], ones], axis=1)
            bias = bias_ref[jnp.minimum(n, 1)]
            os, ms, ls = [], [], []
            for hm in head_masks:
                s = _nt(qb * hm, kcat) + bias
                m = jnp.max(s, axis=-1, keepdims=True)
                p = jnp.exp2(s - m)
                res = jnp.dot(p.astype(BF16), vext, preferred_element_type=F32)
                os.append(res[:, :LANES])
                ls.append(res[:, LANES:])
                ms.append(m)
            o_new = jnp.where(lo, os[0], os[1])
            m_new = jnp.where(lo, ms[0], ms[1])
            l_new = jnp.where(lo, ls[0], ls[1])
            dst = rows(first, dil)
            if dil == order[0]:
                acc_ref[dst, :] = o_new
                m_ref[dst, :] = m_new
                l_ref[dst, :] = l_new
                return carry
            m_old = m_ref[dst, :]
            m_tot = jnp.maximum(m_old, m_new)
            a_old = jnp.exp2(m_old - m_tot)
            a_new = jnp.exp2(m_new - m_tot)
            acc = a_old * acc_ref[dst, :] + a_new * o_new
            den = a_old * l_ref[dst, :] + a_new * l_new
            if dil == order[-1]:
                o_ref[0, dst, :] = acc / den
            else:
                acc_ref[dst, :] = acc
                l_ref[dst, :] = den
                m_ref[dst, :] = m_tot
            return carry

        lax.fori_loop(0, n_units, unit, 0, unroll=UNITS_PER_ITER)


def _att_prompt_call(q, k, v):
    b, s, d_att = q.shape
    bias = _band_bias()
    _, buf_rows = _att_buf_rows(s)
    slab = pl.BlockSpec((1, s, LANES), lambda i, j: (i, 0, j))
    kern = functools.partial(_att_prompt_kernel, seq=s)
    return pl.pallas_call(
        kern,
        grid=(b, d_att // LANES),
        in_specs=[slab, slab, slab, pl.BlockSpec(bias.shape, lambda i, j: (0, 0, 0))],
        out_specs=slab,
        out_shape=jax.ShapeDtypeStruct((b, s, d_att), F32),
        scratch_shapes=[pltpu.VMEM((buf_rows, LANES), BF16)] * 3 + [pltpu.VMEM((s, LANES), F32)] * 3,
        compiler_params=_cparams("parallel", "parallel"),
        name="att_prompt",
    )(q, k, v, bias)


def _gla_consts(blk):
    t = np.arange(LANES)
    same = (t[:, None] // blk) == (t[None, :] // blk)
    tri = ((t[None, :] <= t[:, None]) & same).astype(np.int64)
    mats, lvl, right_rows = [tri], [], []
    h = blk // 2
    while h >= 1:
        pair = t // (2 * h)
        right = (t // h) % 2 == 1
        mid = pair * 2 * h + h - 1
        upto_mid = ((t[None, :] <= mid[:, None]) & same).astype(np.int64)
        mats.append(tri - upto_mid)
        lvl.append(pair[:, None] == pair[None, :])
        right_rows.append(np.broadcast_to(right[:, None], (LANES, LANES)))
        h //= 2
    lvl.append(np.eye(LANES, dtype=bool))
    if blk < LANES:
        mats.append(same.astype(np.int64))
    cmat = jnp.asarray(np.concatenate(mats, 0), BF16)
    lvl = jnp.asarray(np.concatenate(lvl, 0), F32)
    rmask = jnp.asarray(np.concatenate(right_rows, 0), F32)
    return cmat, lvl, rmask, len(right_rows)


def _gla_tile(q, k, la, cmat_ref, lvl_ref, rmask_ref, n_lvl):
    t = LANES
    hi, mid, lo3 = _split3(la)
    cm = cmat_ref[...]
    dot = functools.partial(jnp.dot, preferred_element_type=F32)
    g = dot(cm, hi) + dot(cm, mid) + dot(cm, lo3)
    b = g[0:t]
    if cm.shape[0] > (n_lvl + 1) * t:
        btot = g[(n_lvl + 1) * t:(n_lvl + 2) * t]
    else:
        btot = jnp.broadcast_to(b[t - 1:t, :], b.shape)
    lane = lax.broadcasted_iota(jnp.int32, (t, LANES), 1)
    lo = lane < GLA_DK
    pairs = []
    for j in range(q.shape[1] // LANES):
        sl = slice(j * LANES, (j + 1) * LANES)
        qp, kp, bp = q[:, sl], k[:, sl], b[:, sl]
        a0 = jnp.zeros((t, t), F32)
        a1 = jnp.zeros((t, t), F32)
        for lv in range(n_lvl + 1):
            m = lvl_ref[lv * t:(lv + 1) * t, :]
            if lv < n_lvl:
                dl = g[(lv + 1) * t:(lv + 2) * t, sl]
                e = jnp.exp(-jnp.abs(dl))
                rm = rmask_ref[lv * t:(lv + 1) * t, :]
                x = qp * e * rm
                y = (kp * e * (1.0 - rm)).astype(BF16)
            else:
                x = qp
                y = kp.astype(BF16)
            x0 = jnp.where(lo, x, 0.0).astype(BF16)
            x1 = jnp.where(lo, 0.0, x).astype(BF16)
            a0 = a0 + m * _nt(x0, y)
            a1 = a1 + m * _nt(x1, y)
        q_in = qp * jnp.exp(bp)
        k_in = kp * jnp.exp(btot[:, sl] - bp)
        pairs.append((a0, a1, q_in, k_in, btot[:, sl], lo))
    return pairs


def _gla_prompt_body(q, k, la, v, cmat_ref, lvl_ref, rmask_ref, o_ref, s_ref, *, n_lvl):
    pairs = _gla_tile(q, k, la, cmat_ref, lvl_ref, rmask_ref, n_lvl)
    t = LANES
    row = lax.broadcasted_iota(jnp.int32, (t, t), 0)
    top = row < GLA_DK
    for j, (a0, a1, q_in, k_in, btot, lo) in enumerate(pairs):
        s_prev = s_ref[j]
        sb = s_prev.astype(BF16)
        kt = k_in.T.astype(BF16)
        us = []
        for hh, (a, sel) in enumerate(((a0, lo), (a1, ~lo))):
            vs = slice((2 * j + hh) * GLA_DV, (2 * j + hh + 1) * GLA_DV)
            vb = v[:, vs].astype(BF16)
            qm = jnp.where(sel, q_in, 0.0).astype(BF16)
            o = (jnp.dot(a.astype(BF16), vb, preferred_element_type=F32)
                 + jnp.dot(qm, sb, preferred_element_type=F32))
            o_ref[0, :, vs] = o
            us.append(jnp.dot(kt, vb, preferred_element_type=F32))
        a_col = jnp.exp(btot.T)
        s_ref[j] = a_col * s_prev + jnp.where(top, us[0], us[1])


def _gla_sample_kernel(q_ref, k_ref, la_ref, v_ref, s_in_ref, cmat_ref, lvl_ref, rmask_ref,
                       o_ref, s_out_ref, *, n_lvl, t_dec):
    pairs = _gla_tile(q_ref[...], k_ref[...], la_ref[...], cmat_ref, lvl_ref, rmask_ref, n_lvl)
    t = LANES
    nb = t // t_dec
    row = lax.broadcasted_iota(jnp.int32, (t, t), 0)
    col = lax.broadcasted_iota(jnp.int32, (t, t), 1)
    top = row < GLA_DK
    for j, (a0, a1, q_in, k_in, btot, lo) in enumerate(pairs):
        s_prev = s_in_ref[:, j]
        sb = s_prev.astype(BF16)
        kt = k_in.T
        bt = btot.T
        vbs = []
        for hh, (a, sel) in enumerate(((a0, lo), (a1, ~lo))):
            vs = slice((2 * j + hh) * GLA_DV, (2 * j + hh + 1) * GLA_DV)
            vb = v_ref[:, vs].astype(BF16)
            vbs.append(vb)
            qm = jnp.where(sel, q_in, 0.0)
            o_inter = jnp.concatenate(
                [jnp.dot(qm[i * t_dec:(i + 1) * t_dec].astype(BF16), sb[i],
                         preferred_element_type=F32) for i in range(nb)], axis=0)
            o_ref[:, vs] = jnp.dot(a.astype(BF16), vb, preferred_element_type=F32) + o_inter
        for i in range(nb):
            mine = (col >= i * t_dec) & (col < (i + 1) * t_dec)
            kti = (kt * mine.astype(F32)).astype(BF16)
            u0 = jnp.dot(kti, vbs[0], preferred_element_type=F32)
            u1 = jnp.dot(kti, vbs[1], preferred_element_type=F32)
            a_col = jnp.exp(jnp.broadcast_to(bt[:, i * t_dec:i * t_dec + 1], (t, GLA_DV)))
            s_out_ref[i, j] = a_col * s_prev[i] + jnp.where(top, u0, u1)


def _gla_sample_call(qg, kg, la, vg, state, t_dec):
    rows, d_gk = qg.shape
    d_gv = vg.shape[1]
    n_pair = d_gk // LANES
    nseq = rows // t_dec
    t = LANES
    nb = t // t_dec
    cmat, lvl, rmask, n_lvl = _gla_consts(t_dec)
    s_in = state.reshape(nseq, n_pair, 2 * GLA_DK, GLA_DV)
    tok = lambda w: pl.BlockSpec((t, w), lambda i: (i, 0))
    full = lambda a: pl.BlockSpec(a.shape, lambda i: (0,) * a.ndim)
    st = pl.BlockSpec((nb, n_pair, t, GLA_DV), lambda i: (i, 0, 0, 0))
    kern = functools.partial(_gla_sample_kernel, n_lvl=n_lvl, t_dec=t_dec)
    o, s_out = pl.pallas_call(
        kern,
        grid=(rows // t,),
        in_specs=[tok(d_gk), tok(d_gk), tok(d_gk), tok(d_gv), st, full(cmat), full(lvl), full(rmask)],
        out_specs=[tok(d_gv), st],
        out_shape=[jax.ShapeDtypeStruct((rows, d_gv), F32),
                   jax.ShapeDtypeStruct(s_in.shape, F32)],
        compiler_params=_cparams("parallel"),
        name="gla_sample",
    )(qg, kg, la, vg, s_in, cmat, lvl, rmask)
    return o, s_out.reshape(state.shape)


def _sample_att_table(w, t_dec):
    idx = np.arange(w + LANES)[None, :]
    tq = np.arange(t_dec)[:, None]
    dist = w + tq - idx
    mult = np.zeros((t_dec, w + LANES), np.int64)
    for d in DILATIONS:
        mult += (dist >= 0) & (dist % d == 0) & (dist // d <= STEPS) & (idx < w + t_dec)
    logw = np.where(mult > 0, np.log(np.maximum(mult, 1)), NEG).astype(np.float32)
    return jnp.asarray(logw)


def _att_sample_body(q_ref, kn_ref, vn_ref, ck_ref, cv_ref, logw_ref,
                     o_ref, ko_ref, vo_ref, *, w, t_dec, n_heads):
    lane = lax.broadcasted_iota(jnp.int32, (HEAD_DIM, LANES), 1)
    keep = lane < LANES - t_dec
    zpad = jnp.zeros((LANES - t_dec, n_heads * HEAD_DIM), F32)
    knt = jnp.concatenate([kn_ref[...], zpad], axis=0).T
    vnt = jnp.concatenate([vn_ref[...], zpad], axis=0).T
    q = q_ref[...]
    logw_c = logw_ref[:, 0:w]
    logw_n = logw_ref[:, w:w + LANES]
    outs = []
    for h in range(n_heads):
        hs = slice(h * HEAD_DIM, (h + 1) * HEAD_DIM)
        kt = ck_ref[0, h]
        vt = cv_ref[0, h]
        knh = knt[hs, :]
        vnh = vnt[hs, :]
        qh = q[:, hs].astype(BF16)
        s_c = jnp.dot(qh, kt.astype(BF16), preferred_element_type=F32) + logw_c
        s_n = jnp.dot(qh, knh.astype(BF16), preferred_element_type=F32) + logw_n
        m = jnp.maximum(jnp.max(s_c, axis=-1, keepdims=True), jnp.max(s_n, axis=-1, keepdims=True))
        p_c = jnp.exp(s_c - m)
        p_n = jnp.exp(s_n - m)
        l = jnp.sum(p_c, axis=-1, keepdims=True) + jnp.sum(p_n, axis=-1, keepdims=True)
        o = _nt(p_c.astype(BF16), vt.astype(BF16)) + _nt(p_n.astype(BF16), vnh.astype(BF16))
        outs.append(o / l)
        for src, new, dst in ((kt, knh, ko_ref), (vt, vnh, vo_ref)):
            sh = pltpu.roll(src, w - t_dec, axis=1)
            tail = jnp.where(keep, sh[:, w - LANES:], pltpu.roll(new, LANES - t_dec, axis=1))
            dst[0, h, :, 0:w - LANES] = sh[:, 0:w - LANES]
            dst[0, h, :, w - LANES:w] = tail
    o_ref[...] = jnp.concatenate(outs, axis=1)


def _stream_kernel(q_ref, kn_ref, vn_ref, ck_ref, cv_ref, logw_ref,
                   x_ref, sh_ref, sc_ref, gpre_ref, w_ref, wa2_ref, ba2_ref, rc_ref, rs1_ref, rs2_ref,
                   cmat_ref, lvl_ref, rmask_ref,
                   o_ref, ko_ref, vo_ref,
                   pq_ref, pk_ref, pv_ref, pga_ref, pgg_ref, kt_ref, vt_ref, og_ref, s_out_ref,
                   s_ref, *, w, t_dec, n_heads, n_lvl, n_tiles, d_att, d_gk, d_gv):
    n = lax.rem(pl.program_id(0), n_tiles)

    @pl.when(n == 0)
    def _():
        s_ref[...] = jnp.zeros_like(s_ref)

    _att_sample_body(q_ref, kn_ref, vn_ref, ck_ref, cv_ref, logw_ref, o_ref, ko_ref, vo_ref,
                     w=w, t_dec=t_dec, n_heads=n_heads)
    q, k, v, ga, qg, kg, la, vg, gg = _proj_body(
        x_ref[0], sh_ref[0], sc_ref[0], gpre_ref, w_ref, wa2_ref, ba2_ref, rc_ref, rs1_ref, rs2_ref,
        d_att=d_att, d_gk=d_gk, d_gv=d_gv)
    pq_ref[0] = q
    pk_ref[0] = k
    pv_ref[0] = v
    pga_ref[0] = ga
    pgg_ref[0] = gg
    kt_ref[0] = k.T
    vt_ref[0] = v.T
    _gla_prompt_body(qg, kg, la, vg, cmat_ref, lvl_ref, rmask_ref, og_ref, s_ref, n_lvl=n_lvl)

    @pl.when(n == n_tiles - 1)
    def _():
        s_out_ref[0] = s_ref[...]


def _stream_call(q, kn, vn, cache_k, cache_v, t_dec,
                 x, mod_p, gpre, w_pad, wa2_pad, ba2, tables, *, d_att, d_gk, d_gv):
    nseq, w, n_heads, hd = cache_k.shape
    b, s, d = x.shape
    t = LANES
    n_tiles = s // t
    n_pair = d_gk // LANES
    first_win_tile = (s - w) // t
    assert nseq == b * n_tiles, "one 128-row prompt tile is paired with each sample sequence"
    ck = jnp.transpose(cache_k, (0, 2, 3, 1))
    cv = jnp.transpose(cache_v, (0, 2, 3, 1))
    logw = _sample_att_table(w, t_dec)
    cmat, lvl, rmask, n_lvl = _gla_consts(LANES)
    tok = pl.BlockSpec((t_dec, d_att), lambda i: (i, 0))
    win = pl.BlockSpec((1, n_heads, hd, w), lambda i: (i, 0, 0, 0))
    once = lambda a: pl.BlockSpec(a.shape, lambda i: (0,) * a.ndim, pipeline_mode=pl.Buffered(1))
    ptile = lambda wd: pl.BlockSpec((1, t, wd), lambda i: (i // n_tiles, i % n_tiles, 0))
    pmod = lambda c: pl.BlockSpec((1, 1, d), lambda i: (i // n_tiles, 0, c))
    ptab = pl.BlockSpec((t, LANES), lambda i: (i % n_tiles, 0))
    pwin = pl.BlockSpec((1, d_att, t),
                        lambda i: (i // n_tiles, 0, jnp.maximum(i % n_tiles - first_win_tile, 0)))
    kern = functools.partial(_stream_kernel, w=w, t_dec=t_dec, n_heads=n_heads, n_lvl=n_lvl,
                             n_tiles=n_tiles, d_att=d_att, d_gk=d_gk, d_gv=d_gv)
    f32 = lambda *shape: jax.ShapeDtypeStruct(shape, F32)
    outs = pl.pallas_call(
        kern,
        grid=(nseq,),
        in_specs=[tok, tok, tok, win, win, once(logw),
                  ptile(d), pmod(0), pmod(1), once(gpre), once(w_pad), once(wa2_pad), once(ba2),
                  ptab, ptab, ptab, once(cmat), once(lvl), once(rmask)],
        out_specs=[tok, win, win,
                   ptile(d_att), ptile(d_att), ptile(d_att), ptile(d_att), ptile(d_gv), pwin, pwin,
                   ptile(d_gv),
                   pl.BlockSpec((1, n_pair, t, GLA_DV), lambda i: (i // n_tiles, 0, 0, 0))],
        out_shape=[f32(nseq * t_dec, d_att), f32(*ck.shape), f32(*cv.shape),
                   f32(b, s, d_att), f32(b, s, d_att), f32(b, s, d_att), f32(b, s, d_att),
                   f32(b, s, d_gv), f32(b, d_att, w), f32(b, d_att, w),
                   f32(b, s, d_gv), f32(b, n_pair, t, GLA_DV)],
        scratch_shapes=[pltpu.VMEM((n_pair, t, GLA_DV), F32)],
        compiler_params=pltpu.CompilerParams(dimension_semantics=("arbitrary",),
                                             vmem_limit_bytes=STREAM_VMEM_LIMIT),
        name="stream",
    )(q, kn, vn, ck, cv, logw, x, mod_p, mod_p, gpre, w_pad, wa2_pad, ba2, *tables, cmat, lvl, rmask)
    o, ko, vo, pq, pk, pv, pga, pgg, kt, vt, og, s_out = outs
    to_win = lambda a: jnp.transpose(a, (0, 3, 1, 2))
    return (o, to_win(ko), to_win(vo), pq, pk, pv, pga, pgg,
            to_win(kt.reshape(b, n_heads, hd, w)), to_win(vt.reshape(b, n_heads, hd, w)),
            og, s_out.reshape(b, 2 * n_pair, GLA_DK, GLA_DV))


def _finish_kernel(x_ref, gate_ref, att_ref, ga_ref, og_ref, gg_ref, ggla_ref, gpost_ref, wo_ref,
                   y_ref, *, d_att, d_gv):
    att = att_ref[0] * ga_ref[0]
    og = og_ref[0]
    gg = gg_ref[0]
    y = jnp.dot(att.astype(BF16), wo_ref[0:d_att, :], preferred_element_type=F32)
    for h in range(d_gv // GLA_DV):
        sl = slice(h * GLA_DV, (h + 1) * GLA_DV)
        oh = og[:, sl]
        nh = oh * lax.rsqrt(jnp.mean(oh * oh, axis=-1, keepdims=True) + EPS) * ggla_ref[...]
        gl = (nh * gg[:, sl]).astype(BF16)
        y = y + jnp.dot(gl, wo_ref[d_att + h * GLA_DV:d_att + (h + 1) * GLA_DV, :],
                        preferred_element_type=F32)
    yn = y * lax.rsqrt(jnp.mean(y * y, axis=-1, keepdims=True) + EPS) * gpost_ref[...]
    y_ref[0] = x_ref[0] + gate_ref[0] * yn


def _finish_call(x3, mod3, att, ga, og, gg, g_gla, g_post, w_out, *, tm):
    g, r, d = x3.shape
    d_att = ga.shape[2]
    d_gv = og.shape[2]
    per_row = mod3.shape[1] != 1
    if per_row:
        gate_spec = pl.BlockSpec((1, tm, d), lambda i, j: (i, j, 2))
    else:
        gate_spec = pl.BlockSpec((1, 1, d), lambda i, j: (i, 0, 2))
    tok = lambda w: pl.BlockSpec((1, tm, w), lambda i, j: (i, j, 0))
    full = lambda a: pl.BlockSpec(a.shape, lambda i, j: (0,) * a.ndim)
    kern = functools.partial(_finish_kernel, d_att=d_att, d_gv=d_gv)
    return pl.pallas_call(
        kern,
        grid=(g, r // tm),
        in_specs=[tok(d), gate_spec, tok(d_att), tok(d_att), tok(d_gv), tok(d_gv),
                  full(g_gla), full(g_post), full(w_out)],
        out_specs=tok(d),
        out_shape=jax.ShapeDtypeStruct((g, r, d), F32),
        compiler_params=_cparams("parallel", "arbitrary"),
        name="finish",
    )(x3, mod3, att, ga, og, gg, g_gla, g_post, w_out)


def kernel(x_prompt, x_sample, c_prompt, c_sample, cache_k_win, cache_v_win, state_gla,
           w_mod, b_mod, g_pre, g_post, w_in, w_a2, b_a2, g_gla, w_out):
    b, s, d = x_prompt.shape
    nseq, t_dec, _ = x_sample.shape
    w = cache_k_win.shape[1]
    d_att = cache_k_win.shape[2] * cache_k_win.shape[3]
    d_gk = w_a2.shape[1]
    d_gv = state_gla.shape[1] * state_gla.shape[3]
    d_main = 4 * d_att + 2 * d_gk + 2 * d_gv
    assert w_in.shape[1] == d_main + GLA_GATE_RANK and cache_k_win.shape[3] == HEAD_DIM
    assert state_gla.shape[2] == GLA_DK and state_gla.shape[3] == GLA_DV
    assert w >= STEPS * max(DILATIONS) and s % (ATT_BLOCK * max(DILATIONS)) == 0 and s >= w

    w_pad = jnp.pad(w_in, ((0, 0), (0, LANES - GLA_GATE_RANK))).astype(BF16)
    wa2_pad = jnp.pad(w_a2, ((0, LANES - GLA_GATE_RANK), (0, 0)))
    wo = w_out.astype(BF16)
    gpre = g_pre.reshape(1, d)
    gpost = g_post.reshape(1, d)
    ggla = g_gla.reshape(1, GLA_DV)
    ba2 = b_a2.reshape(1, d_gk)

    n_c = b + nseq
    n_c_pad = -(-n_c // SUBLANES) * SUBLANES
    c_all = jnp.pad(jnp.concatenate([c_prompt, c_sample], axis=0), ((0, n_c_pad - n_c), (0, 0)))
    mod = _mod_call(c_all, w_mod, b_mod)
    mod_p = mod[:b].reshape(b, 1, 3 * d)

    dims = dict(d_att=d_att, d_gk=d_gk, d_gv=d_gv)

    rows = nseq * t_dec
    tms = min(512, rows)
    xs = x_sample.reshape(rows // tms, tms, d)
    mod_s = jnp.repeat(mod[b:b + nseq], t_dec, axis=0).reshape(rows // tms, tms, 3 * d)
    tabs_s = _rope_tables(jnp.tile(PAST_LEN + jnp.arange(t_dec), tms // t_dec))
    outs = _proj_call(xs, mod_s, gpre, w_pad, wa2_pad, ba2, tabs_s, tm=tms, **dims)
    q_s, k_s, v_s, ga_s, qg_s, kg_s, la_s, vg_s, gg_s = [o.reshape(rows, o.shape[2]) for o in outs]

    tabs_p = _rope_tables(jnp.arange(s))
    (o_att_s, k_win_sample, v_win_sample, q, k, v, ga, gg, k_win_prompt, v_win_prompt,
     og, gla_p) = _stream_call(q_s, k_s, v_s, cache_k_win, cache_v_win, t_dec,
                               x_prompt, mod_p, gpre, w_pad, wa2_pad, ba2, tabs_p, **dims)

    att = _att_prompt_call(q, k, v)
    og_s, gla_s = _gla_sample_call(qg_s, kg_s, la_s, vg_s, state_gla, t_dec)

    y_prompt = _finish_call(x_prompt, mod_p, att, ga, og, gg, ggla, gpost, wo, tm=512)
    r3 = lambda a: a.reshape(rows // tms, tms, a.shape[1])
    y_sample = _finish_call(xs, mod_s, r3(o_att_s), r3(ga_s), r3(og_s), r3(gg_s),
                            ggla, gpost, wo, tm=tms).reshape(x_sample.shape)

    return (y_prompt, y_sample, k_win_prompt, v_win_prompt, gla_p,
            k_win_sample, v_win_sample, gla_s)
```

```python
import functools
import math

import numpy as np
import jax
import jax.numpy as jnp
from jax import lax
from jax.experimental import pallas as pl
from jax.experimental.pallas import tpu as pltpu

F32 = jnp.float32
BF16 = jnp.bfloat16

HEAD_DIM = 64
ROT_DIM = HEAD_DIM // 4
ROPE_THETA = 500000.0
DILATIONS = (1, 4, 16)
STEPS = 128
ATT_BLOCK = 128
UNITS_PER_ITER = 16
GATHERS_PER_ITER = 4
PAST_LEN = 8192
GLA_DK = 64
GLA_DV = 128
GLA_GATE_RANK = 16
GLA_TAU = 16.0
EPS = 1e-6
NEG = -1e30
LOG2E = math.log2(math.e)

LANES = 128
SUBLANES = 8
VMEM_LIMIT = 48 * 1024 * 1024
STREAM_VMEM_LIMIT = 56 * 1024 * 1024


def _cparams(*sem):
    return pltpu.CompilerParams(dimension_semantics=sem, vmem_limit_bytes=VMEM_LIMIT)


def _nt(a, b):
    return lax.dot_general(a, b, (((1,), (1,)), ((), ())), preferred_element_type=F32)


def _split3(x):
    hi = x.astype(BF16)
    r1 = x - hi.astype(F32)
    mid = r1.astype(BF16)
    lo = (r1 - mid.astype(F32)).astype(BF16)
    return hi, mid, lo


def _mod_kernel(c_ref, w_ref, b_ref, o_ref):
    c = c_ref[...]
    a = c / (1.0 + jnp.exp(-c))
    o_ref[...] = jnp.dot(a, w_ref[...], precision=lax.Precision.HIGHEST,
                         preferred_element_type=F32) + b_ref[...]


def _mod_call(c, w_mod, b_mod):
    n, d = c.shape
    d3 = w_mod.shape[1]
    tn = 512
    return pl.pallas_call(
        _mod_kernel,
        grid=(d3 // tn,),
        in_specs=[pl.BlockSpec((n, d), lambda j: (0, 0)),
                  pl.BlockSpec((d, tn), lambda j: (0, j)),
                  pl.BlockSpec((1, tn), lambda j: (0, j))],
        out_specs=pl.BlockSpec((n, tn), lambda j: (0, j)),
        out_shape=jax.ShapeDtypeStruct((n, d3), F32),
        compiler_params=_cparams("arbitrary"),
        name="mod",
    )(c, w_mod, b_mod.reshape(1, d3))


def _rope_tables(pos):
    half = ROT_DIM // 2
    inv = 1.0 / (ROPE_THETA ** (jnp.arange(half, dtype=F32) / half))
    ang = pos.astype(F32)[:, None] * inv[None, :]
    cos, sin = jnp.cos(ang), jnp.sin(ang)
    n = pos.shape[0]
    ones = jnp.ones((n, HEAD_DIM - ROT_DIM), F32)
    zeros_h = jnp.zeros((n, half), F32)
    zeros_r = jnp.zeros((n, HEAD_DIM - ROT_DIM), F32)
    c = jnp.concatenate([cos, cos, ones], axis=1)
    s1 = jnp.concatenate([-sin, zeros_h, zeros_r], axis=1)
    s2 = jnp.concatenate([zeros_h, sin, zeros_r], axis=1)
    rep = LANES // HEAD_DIM
    return tuple(jnp.tile(t, (1, rep)) for t in (c, s1, s2))


def _proj_tasks(x, sh, sc, gpre_ref, w_ref, wa2_ref, ba2_ref, rc_ref, rs1_ref, rs2_ref,
                *, d_att, d_gk, d_gv):
    ms = jnp.mean(x * x, axis=-1, keepdims=True)
    h = x * lax.rsqrt(ms + EPS) * gpre_ref[...]
    h = h * (1.0 + sc) + sh
    hb = h.astype(BF16)
    reps = d_att // LANES
    half = ROT_DIM // 2
    res = {}

    def seg(a, n):
        return jnp.dot(hb, w_ref[:, a:a + n], preferred_element_type=F32)

    def rot(t):
        rc = jnp.concatenate([rc_ref[...]] * reps, axis=1)
        rs1 = jnp.concatenate([rs1_ref[...]] * reps, axis=1)
        rs2 = jnp.concatenate([rs2_ref[...]] * reps, axis=1)
        up = pltpu.roll(t, d_att - half, axis=1)
        dn = pltpu.roll(t, half, axis=1)
        return t * rc + up * rs1 + dn * rs2

    def silu(t):
        return t / (1.0 + jnp.exp(-t))

    def log_decay(t):
        a1, a2, a3 = _split3(t)
        w1, w2, w3 = _split3(wa2_ref[...])
        dot = functools.partial(jnp.dot, preferred_element_type=F32)
        z = (dot(a1, w1) + dot(a1, w2) + dot(a2, w1)
             + dot(a1, w3) + dot(a2, w2) + dot(a3, w1)) + ba2_ref[...]
        logsig = jnp.minimum(z, 0.0) - jnp.log1p(jnp.exp(-jnp.abs(z)))
        return logsig * (1.0 / GLA_TAU)

    plan = (("q", d_att, lambda t: rot(t) * (HEAD_DIM ** -0.5)), ("k", d_att, rot),
            ("v", d_att, None), ("ga", d_att, silu),
            ("qg", d_gk, lambda t: t * (GLA_DK ** -0.5)), ("kg", d_gk, None),
            ("vg", d_gv, None), ("gg", d_gv, silu), ("la", LANES, log_decay))
    steps, off = [], 0
    for name, width, post in plan:
        def step(name=name, width=width, post=post, off=off):
            t = seg(off, width)
            res[name] = t if post is None else post(t)
        steps.append(step)
        off += width
    return steps, res


def _proj_kernel(x_ref, sh_ref, sc_ref, gpre_ref, w_ref, wa2_ref, ba2_ref,
                 rc_ref, rs1_ref, rs2_ref, *out_refs, d_att, d_gk, d_gv):
    steps, res = _proj_tasks(x_ref[0], sh_ref[0], sc_ref[0], gpre_ref, w_ref, wa2_ref, ba2_ref,
                             rc_ref, rs1_ref, rs2_ref, d_att=d_att, d_gk=d_gk, d_gv=d_gv)
    for step in steps:
        step()
    for ref, name in zip(out_refs, ("q", "k", "v", "ga", "qg", "kg", "la", "vg", "gg")):
        ref[0] = res[name]


def _proj_call(x3, mod3, gpre, w_pad, wa2_pad, ba2, tables, *, d_att, d_gk, d_gv, tm):
    g, r, d = x3.shape
    grid = (g, r // tm)
    mod_spec = lambda c: pl.BlockSpec((1, tm, d), lambda i, j: (i, j, c))
    tab_spec = pl.BlockSpec((tm, LANES), lambda i, j: (0, 0))
    full = lambda a: pl.BlockSpec(a.shape, lambda i, j: (0,) * a.ndim)
    widths = (d_att, d_att, d_att, d_att, d_gk, d_gk, d_gk, d_gv, d_gv)
    kern = functools.partial(_proj_kernel, d_att=d_att, d_gk=d_gk, d_gv=d_gv)
    return pl.pallas_call(
        kern,
        grid=grid,
        in_specs=[pl.BlockSpec((1, tm, d), lambda i, j: (i, j, 0)),
                  mod_spec(0), mod_spec(1), full(gpre), full(w_pad), full(wa2_pad), full(ba2),
                  tab_spec, tab_spec, tab_spec],
        out_specs=[pl.BlockSpec((1, tm, w), lambda i, j: (i, j, 0)) for w in widths],
        out_shape=[jax.ShapeDtypeStruct((g, r, w), F32) for w in widths],
        compiler_params=_cparams("parallel", "arbitrary"),
        name="proj",
    )(x3, mod3, mod3, gpre, w_pad, wa2_pad, ba2, *tables)


def _band_bias():
    row = np.arange(ATT_BLOCK)[:, None]
    col = np.arange(2 * ATT_BLOCK)[None, :]
    dist = row + ATT_BLOCK - col
    band = (dist >= 0) & (dist <= STEPS)
    first = band & (col >= ATT_BLOCK)
    return jnp.asarray(np.where(np.stack([first, band]), 0.0, NEG), F32)


def _att_buf_rows(seq):
    offs, total = {}, 0
    for dil in DILATIONS:
        offs[dil] = total
        total += seq + dil * ATT_BLOCK
    return offs, total


def _att_prompt_kernel(q_ref, k_ref, v_ref, bias_ref, o_ref,
                       qb_ref, kb_ref, vb_ref, acc_ref, m_ref, l_ref, *, seq):
    blk = ATT_BLOCK
    n_units = seq // blk
    offs, _ = _att_buf_rows(seq)
    lane = lax.broadcasted_iota(jnp.int32, (blk, LANES), 1)
    lo = lane < HEAD_DIM
    head_masks = (jnp.where(lo, 1.0, 0.0).astype(BF16), jnp.where(lo, 0.0, 1.0).astype(BF16))
    ones = jnp.ones((2 * blk, LANES), BF16)
    zeros = jnp.zeros((blk, LANES), BF16)

    def place(u, dil):
        r = lax.rem(u, dil)
        n = lax.div(u, dil)
        first = r + dil * blk * n
        base = offs[dil] + r * (seq // dil + blk) + blk * n
        return n, first, pl.multiple_of(base, blk)

    def rows(first, dil):
        return pl.ds(first, blk) if dil == 1 else pl.ds(first, blk, stride=dil)

    for dil in DILATIONS:
        for r in range(dil):
            z0 = offs[dil] + r * (seq // dil + blk)
            kb_ref[z0:z0 + blk, :] = zeros
            vb_ref[z0:z0 + blk, :] = zeros

        def gather(u, carry, dil=dil):
            _, first, base = place(u, dil)
            dst = pl.ds(base + blk, blk)
            qb_ref[dst, :] = (q_ref[0, rows(first, dil), :] * LOG2E).astype(BF16)
            kb_ref[dst, :] = k_ref[0, rows(first, dil), :].astype(BF16)
            vb_ref[dst, :] = v_ref[0, rows(first, dil), :].astype(BF16)
            return carry

        lax.fori_loop(0, n_units, gather, 0, unroll=GATHERS_PER_ITER)

    order = tuple(reversed(DILATIONS))
    for dil in order:
        def unit(u, carry, dil=dil):
            n, first, base = place(u, dil)
            qb = qb_ref[pl.ds(base + blk, blk), :]
            kcat = kb_ref[pl.ds(base, 2 * blk), :]
            vext = jnp.concatenate([vb_ref[pl.ds(base, 2 * blk), :], ones], axis=1)
            bias = bias_ref[jnp.minimum(n, 1)]
            os, ms, ls = [], [], []
            for hm in head_masks:
                s = _nt(qb * hm, kcat) + bias
                m = jnp.max(s, axis=-1, keepdims=True)
                p = jnp.exp2(s - m)
                res = jnp.dot(p.astype(BF16), vext, preferred_element_type=F32)
                os.append(res[:, :LANES])
                ls.append(res[:, LANES:])
                ms.append(m)
            o_new = jnp.where(lo, os[0], os[1])
            m_new = jnp.where(lo, ms[0], ms[1])
            l_new = jnp.where(lo, ls[0], ls[1])
            dst = rows(first, dil)
            if dil == order[0]:
                acc_ref[dst, :] = o_new
                m_ref[dst, :] = m_new
                l_ref[dst, :] = l_new
                return carry
            m_old = m_ref[dst, :]
            m_tot = jnp.maximum(m_old, m_new)
            a_old = jnp.exp2(m_old - m_tot)
            a_new = jnp.exp2(m_new - m_tot)
            acc = a_old * acc_ref[dst, :] + a_new * o_new
            den = a_old * l_ref[dst, :] + a_new * l_new
            if dil == order[-1]:
                o_ref[0, pl.ds(pl.multiple_of(first, blk), blk), :] = (acc / den).astype(o_ref.dtype)
            else:
                acc_ref[dst, :] = acc
                l_ref[dst, :] = den
                m_ref[dst, :] = m_tot
            return carry

        lax.fori_loop(0, n_units, unit, 0, unroll=UNITS_PER_ITER)


def _att_prompt_call(q, k, v):
    b, s, d_att = q.shape
    bias = _band_bias()
    _, buf_rows = _att_buf_rows(s)
    slab = pl.BlockSpec((1, s, LANES), lambda i, j: (i, 0, j))
    kern = functools.partial(_att_prompt_kernel, seq=s)
    return pl.pallas_call(
        kern,
        grid=(b, d_att // LANES),
        in_specs=[slab, slab, slab, pl.BlockSpec(bias.shape, lambda i, j: (0, 0, 0))],
        out_specs=slab,
        out_shape=jax.ShapeDtypeStruct((b, s, d_att), BF16),
        scratch_shapes=[pltpu.VMEM((buf_rows, LANES), BF16)] * 3 + [pltpu.VMEM((s, LANES), F32)] * 3,
        compiler_params=_cparams("parallel", "parallel"),
        name="att_prompt",
    )(q, k, v, bias)


def _gla_consts(blk):
    t = np.arange(LANES)
    same = (t[:, None] // blk) == (t[None, :] // blk)
    tri = ((t[None, :] <= t[:, None]) & same).astype(np.int64)
    mats, lvl, right_rows = [tri], [], []
    h = blk // 2
    while h >= 1:
        pair = t // (2 * h)
        right = (t // h) % 2 == 1
        mid = pair * 2 * h + h - 1
        upto_mid = ((t[None, :] <= mid[:, None]) & same).astype(np.int64)
        mats.append(tri - upto_mid)
        lvl.append(pair[:, None] == pair[None, :])
        right_rows.append(np.broadcast_to(right[:, None], (LANES, LANES)))
        h //= 2
    lvl.append(np.eye(LANES, dtype=bool))
    if blk < LANES:
        mats.append(same.astype(np.int64))
    cmat = jnp.asarray(np.concatenate(mats, 0), BF16)
    lvl = jnp.asarray(np.concatenate(lvl, 0), F32)
    rmask = jnp.asarray(np.concatenate(right_rows, 0), F32)
    return cmat, lvl, rmask, len(right_rows)


def _gla_tile_tasks(q, k, la, cmat_ref, lvl_ref, rmask_ref, n_lvl):
    t = LANES
    n_pair = q.shape[1] // LANES
    st = {}
    pairs = [None] * n_pair

    def cumulate():
        hi, mid, lo3 = _split3(la)
        cm = cmat_ref[...]
        dot = functools.partial(jnp.dot, preferred_element_type=F32)
        g = dot(cm, hi) + dot(cm, mid) + dot(cm, lo3)
        b = g[0:t]
        if cm.shape[0] > (n_lvl + 1) * t:
            btot = g[(n_lvl + 1) * t:(n_lvl + 2) * t]
        else:
            btot = jnp.broadcast_to(b[t - 1:t, :], b.shape)
        st.update(g=g, b=b, btot=btot)

    def pair(j):
        g, b, btot = st["g"], st["b"], st["btot"]
        lane = lax.broadcasted_iota(jnp.int32, (t, LANES), 1)
        lo = lane < GLA_DK
        sl = slice(j * LANES, (j + 1) * LANES)
        qp, kp, bp = q[:, sl], k[:, sl], b[:, sl]
        a0 = jnp.zeros((t, t), F32)
        a1 = jnp.zeros((t, t), F32)
        for lv in range(n_lvl + 1):
            m = lvl_ref[lv * t:(lv + 1) * t, :]
            if lv < n_lvl:
                dl = g[(lv + 1) * t:(lv + 2) * t, sl]
                e = jnp.exp(-jnp.abs(dl))
                rm = rmask_ref[lv * t:(lv + 1) * t, :]
                x = qp * e * rm
                y = (kp * e * (1.0 - rm)).astype(BF16)
            else:
                x = qp
                y = kp.astype(BF16)
            x0 = jnp.where(lo, x, 0.0).astype(BF16)
            x1 = jnp.where(lo, 0.0, x).astype(BF16)
            a0 = a0 + m * _nt(x0, y)
            a1 = a1 + m * _nt(x1, y)
        q_in = qp * jnp.exp(bp)
        k_in = kp * jnp.exp(btot[:, sl] - bp)
        pairs[j] = (a0, a1, q_in, k_in, btot[:, sl], lo)

    return [cumulate] + [functools.partial(pair, j) for j in range(n_pair)], pairs


def _gla_prompt_tasks(res, cmat_ref, lvl_ref, rmask_ref, o_ref, s_ref, *, n_lvl, d_gk):
    t = LANES
    n_pair = d_gk // LANES
    holder = {}

    def start():
        steps, pairs = _gla_tile_tasks(res["qg"], res["kg"], res["la"],
                                       cmat_ref, lvl_ref, rmask_ref, n_lvl)
        holder.update(steps=steps, pairs=pairs)
        steps[0]()

    def levels(j):
        holder["steps"][1 + j]()

    def finish(j):
        a0, a1, q_in, k_in, btot, lo = holder["pairs"][j]
        v = res["vg"]
        row = lax.broadcasted_iota(jnp.int32, (t, t), 0)
        top = row < GLA_DK
        s_prev = s_ref[j]
        sb = s_prev.astype(BF16)
        kt = k_in.T.astype(BF16)
        us = []
        for hh, (a, sel) in enumerate(((a0, lo), (a1, ~lo))):
            vs = slice((2 * j + hh) * GLA_DV, (2 * j + hh + 1) * GLA_DV)
            vb = v[:, vs].astype(BF16)
            qm = jnp.where(sel, q_in, 0.0).astype(BF16)
            o = (jnp.dot(a.astype(BF16), vb, preferred_element_type=F32)
                 + jnp.dot(qm, sb, preferred_element_type=F32))
            o_ref[0, :, vs] = o.astype(o_ref.dtype)
            us.append(jnp.dot(kt, vb, preferred_element_type=F32))
        a_col = jnp.exp(btot.T)
        s_ref[j] = a_col * s_prev + jnp.where(top, us[0], us[1])

    tasks = [start]
    for j in range(n_pair):
        tasks += [functools.partial(levels, j), functools.partial(finish, j)]
    return tasks


def _gla_sample_kernel(q_ref, k_ref, la_ref, v_ref, s_in_ref, cmat_ref, lvl_ref, rmask_ref,
                       o_ref, s_out_ref, *, n_lvl, t_dec):
    steps, pairs = _gla_tile_tasks(q_ref[...], k_ref[...], la_ref[...],
                                   cmat_ref, lvl_ref, rmask_ref, n_lvl)
    for step in steps:
        step()
    t = LANES
    nb = t // t_dec
    row = lax.broadcasted_iota(jnp.int32, (t, t), 0)
    col = lax.broadcasted_iota(jnp.int32, (t, t), 1)
    top = row < GLA_DK
    for j, (a0, a1, q_in, k_in, btot, lo) in enumerate(pairs):
        s_prev = s_in_ref[:, j]
        sb = s_prev.astype(BF16)
        kt = k_in.T
        bt = btot.T
        vbs = []
        for hh, (a, sel) in enumerate(((a0, lo), (a1, ~lo))):
            vs = slice((2 * j + hh) * GLA_DV, (2 * j + hh + 1) * GLA_DV)
            vb = v_ref[:, vs].astype(BF16)
            vbs.append(vb)
            qm = jnp.where(sel, q_in, 0.0)
            o_inter = jnp.concatenate(
                [jnp.dot(qm[i * t_dec:(i + 1) * t_dec].astype(BF16), sb[i],
                         preferred_element_type=F32) for i in range(nb)], axis=0)
            o_ref[:, vs] = jnp.dot(a.astype(BF16), vb, preferred_element_type=F32) + o_inter
        for i in range(nb):
            mine = (col >= i * t_dec) & (col < (i + 1) * t_dec)
            kti = (kt * mine.astype(F32)).astype(BF16)
            u0 = jnp.dot(kti, vbs[0], preferred_element_type=F32)
            u1 = jnp.dot(kti, vbs[1], preferred_element_type=F32)
            a_col = jnp.exp(jnp.broadcast_to(bt[:, i * t_dec:i * t_dec + 1], (t, GLA_DV)))
            s_out_ref[i, j] = a_col * s_prev[i] + jnp.where(top, u0, u1)


def _gla_sample_call(qg, kg, la, vg, state, t_dec):
    rows, d_gk = qg.shape
    d_gv = vg.shape[1]
    n_pair = d_gk // LANES
    nseq = rows // t_dec
    t = LANES
    nb = t // t_dec
    cmat, lvl, rmask, n_lvl = _gla_consts(t_dec)
    s_in = state.reshape(nseq, n_pair, 2 * GLA_DK, GLA_DV)
    tok = lambda w: pl.BlockSpec((t, w), lambda i: (i, 0))
    full = lambda a: pl.BlockSpec(a.shape, lambda i: (0,) * a.ndim)
    st = pl.BlockSpec((nb, n_pair, t, GLA_DV), lambda i: (i, 0, 0, 0))
    kern = functools.partial(_gla_sample_kernel, n_lvl=n_lvl, t_dec=t_dec)
    o, s_out = pl.pallas_call(
        kern,
        grid=(rows // t,),
        in_specs=[tok(d_gk), tok(d_gk), tok(d_gk), tok(d_gv), st, full(cmat), full(lvl), full(rmask)],
        out_specs=[tok(d_gv), st],
        out_shape=[jax.ShapeDtypeStruct((rows, d_gv), F32),
                   jax.ShapeDtypeStruct(s_in.shape, F32)],
        compiler_params=_cparams("parallel"),
        name="gla_sample",
    )(qg, kg, la, vg, s_in, cmat, lvl, rmask)
    return o, s_out.reshape(state.shape)


def _sample_att_table(w, t_dec):
    idx = np.arange(w + LANES)[None, :]
    tq = np.arange(t_dec)[:, None]
    dist = w + tq - idx
    mult = np.zeros((t_dec, w + LANES), np.int64)
    for d in DILATIONS:
        mult += (dist >= 0) & (dist % d == 0) & (dist // d <= STEPS) & (idx < w + t_dec)
    logw = np.where(mult > 0, np.log(np.maximum(mult, 1)), NEG).astype(np.float32)
    return jnp.asarray(logw)


def _att_sample_tasks(q_ref, kn_ref, vn_ref, ck_ref, cv_ref, logw_ref,
                      o_ref, ko_ref, vo_ref, *, w, t_dec, n_heads):
    lane = lax.broadcasted_iota(jnp.int32, (HEAD_DIM, LANES), 1)
    keep = lane < LANES - t_dec
    zpad = jnp.zeros((LANES - t_dec, n_heads * HEAD_DIM), F32)
    knt = jnp.concatenate([kn_ref[...], zpad], axis=0).T
    vnt = jnp.concatenate([vn_ref[...], zpad], axis=0).T
    q = q_ref[...]
    logw_c = logw_ref[:, 0:w]
    logw_n = logw_ref[:, w:w + LANES]
    outs = [None] * n_heads

    def head(h):
        hs = slice(h * HEAD_DIM, (h + 1) * HEAD_DIM)
        kt = ck_ref[0, h]
        vt = cv_ref[0, h]
        knh = knt[hs, :]
        vnh = vnt[hs, :]
        qh = q[:, hs].astype(BF16)
        s_c = jnp.dot(qh, kt.astype(BF16), preferred_element_type=F32) + logw_c
        s_n = jnp.dot(qh, knh.astype(BF16), preferred_element_type=F32) + logw_n
        m = jnp.maximum(jnp.max(s_c, axis=-1, keepdims=True), jnp.max(s_n, axis=-1, keepdims=True))
        p_c = jnp.exp(s_c - m)
        p_n = jnp.exp(s_n - m)
        l = jnp.sum(p_c, axis=-1, keepdims=True) + jnp.sum(p_n, axis=-1, keepdims=True)
        o = _nt(p_c.astype(BF16), vt.astype(BF16)) + _nt(p_n.astype(BF16), vnh.astype(BF16))
        outs[h] = o / l
        for src, new, dst in ((kt, knh, ko_ref), (vt, vnh, vo_ref)):
            sh = pltpu.roll(src, w - t_dec, axis=1)
            tail = jnp.where(keep, sh[:, w - LANES:], pltpu.roll(new, LANES - t_dec, axis=1))
            dst[0, h, :, 0:w - LANES] = sh[:, 0:w - LANES]
            dst[0, h, :, w - LANES:w] = tail

    def finish():
        o_ref[...] = jnp.concatenate(outs, axis=1)

    return [functools.partial(head, h) for h in range(n_heads)] + [finish]


def _stream_kernel(q_ref, kn_ref, vn_ref, ck_ref, cv_ref, logw_ref,
                   x_ref, sh_ref, sc_ref, gpre_ref, w_ref, wa2_ref, ba2_ref, rc_ref, rs1_ref, rs2_ref,
                   cmat_ref, lvl_ref, rmask_ref,
                   o_ref, ko_ref, vo_ref,
                   pq_ref, pk_ref, pv_ref, pga_ref, pgg_ref, kt_ref, vt_ref, og_ref, s_out_ref,
                   s_ref, *, w, t_dec, n_heads, n_lvl, n_tiles, d_att, d_gk, d_gv):
    n = lax.rem(pl.program_id(0), n_tiles)

    @pl.when(n == 0)
    def _():
        s_ref[...] = jnp.zeros_like(s_ref)

    win_steps = _att_sample_tasks(q_ref, kn_ref, vn_ref, ck_ref, cv_ref, logw_ref,
                                  o_ref, ko_ref, vo_ref, w=w, t_dec=t_dec, n_heads=n_heads)
    proj_steps, res = _proj_tasks(
        x_ref[0], sh_ref[0], sc_ref[0], gpre_ref, w_ref, wa2_ref, ba2_ref, rc_ref, rs1_ref, rs2_ref,
        d_att=d_att, d_gk=d_gk, d_gv=d_gv)

    def emit():
        pq_ref[0] = res["q"]
        pk_ref[0] = res["k"]
        pv_ref[0] = res["v"]
        pga_ref[0] = res["ga"].astype(pga_ref.dtype)
        pgg_ref[0] = res["gg"].astype(pgg_ref.dtype)
        kt_ref[0] = res["k"].T
        vt_ref[0] = res["v"].T

    tile_steps = proj_steps + [emit] + _gla_prompt_tasks(
        res, cmat_ref, lvl_ref, rmask_ref, og_ref, s_ref, n_lvl=n_lvl, d_gk=d_gk)
    for step in win_steps + tile_steps:
        step()

    @pl.when(n == n_tiles - 1)
    def _():
        s_out_ref[0] = s_ref[...]


def _stream_call(q, kn, vn, cache_k, cache_v, t_dec,
                 x, mod_p, gpre, w_pad, wa2_pad, ba2, tables, *, d_att, d_gk, d_gv):
    nseq, w, n_heads, hd = cache_k.shape
    b, s, d = x.shape
    t = LANES
    n_tiles = s // t
    n_pair = d_gk // LANES
    first_win_tile = (s - w) // t
    assert nseq == b * n_tiles, "one 128-row prompt tile is paired with each sample sequence"
    ck = jnp.transpose(cache_k, (0, 2, 3, 1))
    cv = jnp.transpose(cache_v, (0, 2, 3, 1))
    logw = _sample_att_table(w, t_dec)
    cmat, lvl, rmask, n_lvl = _gla_consts(LANES)
    tok = pl.BlockSpec((t_dec, d_att), lambda i: (i, 0))
    win = pl.BlockSpec((1, n_heads, hd, w), lambda i: (i, 0, 0, 0))
    once = lambda a: pl.BlockSpec(a.shape, lambda i: (0,) * a.ndim, pipeline_mode=pl.Buffered(1))
    ptile = lambda wd: pl.BlockSpec((1, t, wd), lambda i: (i // n_tiles, i % n_tiles, 0))
    pmod = lambda c: pl.BlockSpec((1, 1, d), lambda i: (i // n_tiles, 0, c))
    ptab = pl.BlockSpec((t, LANES), lambda i: (i % n_tiles, 0))
    pwin = pl.BlockSpec((1, d_att, t),
                        lambda i: (i // n_tiles, 0, jnp.maximum(i % n_tiles - first_win_tile, 0)))
    kern = functools.partial(_stream_kernel, w=w, t_dec=t_dec, n_heads=n_heads, n_lvl=n_lvl,
                             n_tiles=n_tiles, d_att=d_att, d_gk=d_gk, d_gv=d_gv)
    f32 = lambda *shape: jax.ShapeDtypeStruct(shape, F32)
    bf16 = lambda *shape: jax.ShapeDtypeStruct(shape, BF16)
    outs = pl.pallas_call(
        kern,
        grid=(nseq,),
        in_specs=[tok, tok, tok, win, win, once(logw),
                  ptile(d), pmod(0), pmod(1), once(gpre), once(w_pad), once(wa2_pad), once(ba2),
                  ptab, ptab, ptab, once(cmat), once(lvl), once(rmask)],
        out_specs=[tok, win, win,
                   ptile(d_att), ptile(d_att), ptile(d_att), ptile(d_att), ptile(d_gv), pwin, pwin,
                   ptile(d_gv),
                   pl.BlockSpec((1, n_pair, t, GLA_DV), lambda i: (i // n_tiles, 0, 0, 0))],
        out_shape=[f32(nseq * t_dec, d_att), f32(*ck.shape), f32(*cv.shape),
                   f32(b, s, d_att), f32(b, s, d_att), f32(b, s, d_att), bf16(b, s, d_att),
                   bf16(b, s, d_gv), f32(b, d_att, w), f32(b, d_att, w),
                   bf16(b, s, d_gv), f32(b, n_pair, t, GLA_DV)],
        scratch_shapes=[pltpu.VMEM((n_pair, t, GLA_DV), F32)],
        compiler_params=pltpu.CompilerParams(dimension_semantics=("arbitrary",),
                                             vmem_limit_bytes=STREAM_VMEM_LIMIT),
        name="stream",
    )(q, kn, vn, ck, cv, logw, x, mod_p, mod_p, gpre, w_pad, wa2_pad, ba2, *tables, cmat, lvl, rmask)
    o, ko, vo, pq, pk, pv, pga, pgg, kt, vt, og, s_out = outs
    to_win = lambda a: jnp.transpose(a, (0, 3, 1, 2))
    return (o, to_win(ko), to_win(vo), pq, pk, pv, pga, pgg,
            to_win(kt.reshape(b, n_heads, hd, w)), to_win(vt.reshape(b, n_heads, hd, w)),
            og, s_out.reshape(b, 2 * n_pair, GLA_DK, GLA_DV))


def _finish_kernel(x_ref, gate_ref, att_ref, ga_ref, og_ref, gg_ref, ggla_ref, gpost_ref, wo_ref,
                   y_ref, *, d_att, d_gv):
    att = att_ref[0] * ga_ref[0]
    og = og_ref[0].astype(F32)
    gg = gg_ref[0].astype(F32)
    y = jnp.dot(att.astype(BF16), wo_ref[0:d_att, :], preferred_element_type=F32)
    for h in range(d_gv // GLA_DV):
        sl = slice(h * GLA_DV, (h + 1) * GLA_DV)
        oh = og[:, sl]
        nh = oh * lax.rsqrt(jnp.mean(oh * oh, axis=-1, keepdims=True) + EPS) * ggla_ref[...]
        gl = (nh * gg[:, sl]).astype(BF16)
        y = y + jnp.dot(gl, wo_ref[d_att + h * GLA_DV:d_att + (h + 1) * GLA_DV, :],
                        preferred_element_type=F32)
    yn = y * lax.rsqrt(jnp.mean(y * y, axis=-1, keepdims=True) + EPS) * gpost_ref[...]
    y_ref[0] = x_ref[0] + gate_ref[0] * yn


def _finish_call(x3, mod3, att, ga, og, gg, g_gla, g_post, w_out, *, tm):
    g, r, d = x3.shape
    d_att = ga.shape[2]
    d_gv = og.shape[2]
    per_row = mod3.shape[1] != 1
    if per_row:
        gate_spec = pl.BlockSpec((1, tm, d), lambda i, j: (i, j, 2))
    else:
        gate_spec = pl.BlockSpec((1, 1, d), lambda i, j: (i, 0, 2))
    tok = lambda w: pl.BlockSpec((1, tm, w), lambda i, j: (i, j, 0))
    full = lambda a: pl.BlockSpec(a.shape, lambda i, j: (0,) * a.ndim)
    kern = functools.partial(_finish_kernel, d_att=d_att, d_gv=d_gv)
    return pl.pallas_call(
        kern,
        grid=(g, r // tm),
        in_specs=[tok(d), gate_spec, tok(d_att), tok(d_att), tok(d_gv), tok(d_gv),
                  full(g_gla), full(g_post), full(w_out)],
        out_specs=tok(d),
        out_shape=jax.ShapeDtypeStruct((g, r, d), F32),
        compiler_params=_cparams("parallel", "arbitrary"),
        name="finish",
    )(x3, mod3, att, ga, og, gg, g_gla, g_post, w_out)


def kernel(x_prompt, x_sample, c_prompt, c_sample, cache_k_win, cache_v_win, state_gla,
           w_mod, b_mod, g_pre, g_post, w_in, w_a2, b_a2, g_gla, w_out):
    b, s, d = x_prompt.shape
    nseq, t_dec, _ = x_sample.shape
    w = cache_k_win.shape[1]
    d_att = cache_k_win.shape[2] * cache_k_win.shape[3]
    d_gk = w_a2.shape[1]
    d_gv = state_gla.shape[1] * state_gla.shape[3]
    d_main = 4 * d_att + 2 * d_gk + 2 * d_gv
    assert w_in.shape[1] == d_main + GLA_GATE_RANK and cache_k_win.shape[3] == HEAD_DIM
    assert state_gla.shape[2] == GLA_DK and state_gla.shape[3] == GLA_DV
    assert w >= STEPS * max(DILATIONS) and s % (ATT_BLOCK * max(DILATIONS)) == 0 and s >= w

    w_pad = jnp.pad(w_in, ((0, 0), (0, LANES - GLA_GATE_RANK))).astype(BF16)
    wa2_pad = jnp.pad(w_a2, ((0, LANES - GLA_GATE_RANK), (0, 0)))
    wo = w_out.astype(BF16)
    gpre = g_pre.reshape(1, d)
    gpost = g_post.reshape(1, d)
    ggla = g_gla.reshape(1, GLA_DV)
    ba2 = b_a2.reshape(1, d_gk)

    n_c = b + nseq
    n_c_pad = -(-n_c // SUBLANES) * SUBLANES
    c_all = jnp.pad(jnp.concatenate([c_prompt, c_sample], axis=0), ((0, n_c_pad - n_c), (0, 0)))
    mod = _mod_call(c_all, w_mod, b_mod)
    mod_p = mod[:b].reshape(b, 1, 3 * d)

    dims = dict(d_att=d_att, d_gk=d_gk, d_gv=d_gv)

    rows = nseq * t_dec
    tms = min(512, rows)
    xs = x_sample.reshape(rows // tms, tms, d)
    mod_s = jnp.repeat(mod[b:b + nseq], t_dec, axis=0).reshape(rows // tms, tms, 3 * d)
    tabs_s = _rope_tables(jnp.tile(PAST_LEN + jnp.arange(t_dec), tms // t_dec))
    outs = _proj_call(xs, mod_s, gpre, w_pad, wa2_pad, ba2, tabs_s, tm=tms, **dims)
    q_s, k_s, v_s, ga_s, qg_s, kg_s, la_s, vg_s, gg_s = [o.reshape(rows, o.shape[2]) for o in outs]

    tabs_p = _rope_tables(jnp.arange(s))
    (o_att_s, k_win_sample, v_win_sample, q, k, v, ga, gg, k_win_prompt, v_win_prompt,
     og, gla_p) = _stream_call(q_s, k_s, v_s, cache_k_win, cache_v_win, t_dec,
                               x_prompt, mod_p, gpre, w_pad, wa2_pad, ba2, tabs_p, **dims)

    att = _att_prompt_call(q, k, v)
    og_s, gla_s = _gla_sample_call(qg_s, kg_s, la_s, vg_s, state_gla, t_dec)

    y_prompt = _finish_call(x_prompt, mod_p, att, ga, og, gg, ggla, gpost, wo, tm=512)
    r3 = lambda a: a.reshape(rows // tms, tms, a.shape[1])
    y_sample = _finish_call(xs, mod_s, r3(o_att_s), r3(ga_s), r3(og_s), r3(gg_s),
                            ggla, gpost, wo, tm=tms).reshape(x_sample.shape)

    return (y_prompt, y_sample, k_win_prompt, v_win_prompt, gla_p,
            k_win_sample, v_win_sample, gla_s)
```

```python
import functools
import math

import numpy as np
import jax
import jax.numpy as jnp
from jax import lax
from jax.experimental import pallas as pl
from jax.experimental.pallas import tpu as pltpu

F32 = jnp.float32
BF16 = jnp.bfloat16

HEAD_DIM = 64
ROT_DIM = HEAD_DIM // 4
ROPE_THETA = 500000.0
DILATIONS = (1, 4, 16)
STEPS = 128
ATT_BLOCK = 128
UNITS_PER_ITER = 16
GATHERS_PER_ITER = 4
PAST_LEN = 8192
GLA_DK = 64
GLA_DV = 128
GLA_GATE_RANK = 16
GLA_TAU = 16.0
EPS = 1e-6
NEG = -1e30
LOG2E = math.log2(math.e)

LANES = 128
SUBLANES = 8
VMEM_LIMIT = 48 * 1024 * 1024
STREAM_VMEM_LIMIT = 56 * 1024 * 1024


def _cparams(*sem):
    return pltpu.CompilerParams(dimension_semantics=sem, vmem_limit_bytes=VMEM_LIMIT)


def _nt(a, b):
    return lax.dot_general(a, b, (((1,), (1,)), ((), ())), preferred_element_type=F32)


def _split3(x):
    hi = x.astype(BF16)
    r1 = x - hi.astype(F32)
    mid = r1.astype(BF16)
    lo = (r1 - mid.astype(F32)).astype(BF16)
    return hi, mid, lo


def _mod_kernel(c_ref, w_ref, b_ref, o_ref):
    c = c_ref[...]
    a = c / (1.0 + jnp.exp(-c))
    o_ref[...] = jnp.dot(a, w_ref[...], precision=lax.Precision.HIGHEST,
                         preferred_element_type=F32) + b_ref[...]


def _mod_call(c, w_mod, b_mod):
    n, d = c.shape
    d3 = w_mod.shape[1]
    tn = 512
    return pl.pallas_call(
        _mod_kernel,
        grid=(d3 // tn,),
        in_specs=[pl.BlockSpec((n, d), lambda j: (0, 0)),
                  pl.BlockSpec((d, tn), lambda j: (0, j)),
                  pl.BlockSpec((1, tn), lambda j: (0, j))],
        out_specs=pl.BlockSpec((n, tn), lambda j: (0, j)),
        out_shape=jax.ShapeDtypeStruct((n, d3), F32),
        compiler_params=_cparams("arbitrary"),
        name="mod",
    )(c, w_mod, b_mod.reshape(1, d3))


def _rope_tables(pos):
    half = ROT_DIM // 2
    inv = 1.0 / (ROPE_THETA ** (jnp.arange(half, dtype=F32) / half))
    ang = pos.astype(F32)[:, None] * inv[None, :]
    cos, sin = jnp.cos(ang), jnp.sin(ang)
    n = pos.shape[0]
    ones = jnp.ones((n, HEAD_DIM - ROT_DIM), F32)
    zeros_h = jnp.zeros((n, half), F32)
    zeros_r = jnp.zeros((n, HEAD_DIM - ROT_DIM), F32)
    c = jnp.concatenate([cos, cos, ones], axis=1)
    s1 = jnp.concatenate([-sin, zeros_h, zeros_r], axis=1)
    s2 = jnp.concatenate([zeros_h, sin, zeros_r], axis=1)
    rep = LANES // HEAD_DIM
    return jnp.concatenate([jnp.tile(t, (1, rep)) for t in (c, s1, s2)], axis=1)


def _proj_tasks(x, shsc, gpre_ref, w_ref, wa2_ref, ba2_ref, tab_ref, *, d_att, d_gk, d_gv):
    d = x.shape[1]
    ms = jnp.mean(x * x, axis=-1, keepdims=True)
    h = x * lax.rsqrt(ms + EPS) * gpre_ref[...]
    h = h * (1.0 + shsc[:, d:]) + shsc[:, :d]
    hb = h.astype(BF16)
    reps = d_att // LANES
    half = ROT_DIM // 2
    res = {}

    def seg(a, n):
        return jnp.dot(hb, w_ref[:, a:a + n], preferred_element_type=F32)

    def rot(t):
        rc = jnp.concatenate([tab_ref[:, 0:LANES]] * reps, axis=1)
        rs1 = jnp.concatenate([tab_ref[:, LANES:2 * LANES]] * reps, axis=1)
        rs2 = jnp.concatenate([tab_ref[:, 2 * LANES:3 * LANES]] * reps, axis=1)
        up = pltpu.roll(t, d_att - half, axis=1)
        dn = pltpu.roll(t, half, axis=1)
        return t * rc + up * rs1 + dn * rs2

    def silu(t):
        return t / (1.0 + jnp.exp(-t))

    def log_decay(t):
        a1, a2, a3 = _split3(t)
        w1, w2, w3 = _split3(wa2_ref[...])
        dot = functools.partial(jnp.dot, preferred_element_type=F32)
        z = (dot(a1, w1) + dot(a1, w2) + dot(a2, w1)
             + dot(a1, w3) + dot(a2, w2) + dot(a3, w1)) + ba2_ref[...]
        logsig = jnp.minimum(z, 0.0) - jnp.log1p(jnp.exp(-jnp.abs(z)))
        return logsig * (1.0 / GLA_TAU)

    plan = (("q", d_att, lambda t: rot(t) * (HEAD_DIM ** -0.5)), ("k", d_att, rot),
            ("v", d_att, None), ("ga", d_att, silu),
            ("qg", d_gk, lambda t: t * (GLA_DK ** -0.5)), ("kg", d_gk, None),
            ("vg", d_gv, None), ("gg", d_gv, silu), ("la", LANES, log_decay))
    steps, off = [], 0
    for name, width, post in plan:
        def step(name=name, width=width, post=post, off=off):
            t = seg(off, width)
            res[name] = t if post is None else post(t)
        steps.append(step)
        off += width
    return steps, res


def _proj_kernel(x_ref, shsc_ref, gpre_ref, w_ref, wa2_ref, ba2_ref, tab_ref,
                 qkv_ref, *out_refs, d_att, d_gk, d_gv):
    steps, res = _proj_tasks(x_ref[0], shsc_ref[0], gpre_ref, w_ref, wa2_ref, ba2_ref, tab_ref,
                             d_att=d_att, d_gk=d_gk, d_gv=d_gv)
    for step in steps:
        step()
    for i, name in enumerate(("q", "k", "v")):
        qkv_ref[0, :, i * d_att:(i + 1) * d_att] = res[name]
    for ref, name in zip(out_refs, ("ga", "qg", "kg", "la", "vg", "gg")):
        ref[0] = res[name]


def _proj_call(x3, mod3, gpre, w_pad, wa2_pad, ba2, table, *, d_att, d_gk, d_gv, tm):
    g, r, d = x3.shape
    grid = (g, r // tm)
    full = lambda a: pl.BlockSpec(a.shape, lambda i, j: (0,) * a.ndim)
    widths = (3 * d_att, d_att, d_gk, d_gk, d_gk, d_gv, d_gv)
    kern = functools.partial(_proj_kernel, d_att=d_att, d_gk=d_gk, d_gv=d_gv)
    return pl.pallas_call(
        kern,
        grid=grid,
        in_specs=[pl.BlockSpec((1, tm, d), lambda i, j: (i, j, 0)),
                  pl.BlockSpec((1, tm, 2 * d), lambda i, j: (i, j, 0)),
                  full(gpre), full(w_pad), full(wa2_pad), full(ba2),
                  pl.BlockSpec((tm, 3 * LANES), lambda i, j: (0, 0))],
        out_specs=[pl.BlockSpec((1, tm, w), lambda i, j: (i, j, 0)) for w in widths],
        out_shape=[jax.ShapeDtypeStruct((g, r, w), F32) for w in widths],
        compiler_params=_cparams("parallel", "arbitrary"),
        name="proj",
    )(x3, mod3, gpre, w_pad, wa2_pad, ba2, table)


def _band_bias():
    row = np.arange(ATT_BLOCK)[:, None]
    col = np.arange(2 * ATT_BLOCK)[None, :]
    dist = row + ATT_BLOCK - col
    band = (dist >= 0) & (dist <= STEPS)
    first = band & (col >= ATT_BLOCK)
    return jnp.asarray(np.where(np.stack([first, band]), 0.0, NEG), F32)


def _att_buf_rows(seq):
    offs, total = {}, 0
    for dil in DILATIONS:
        offs[dil] = total
        total += seq + dil * ATT_BLOCK
    return offs, total


def _att_prompt_kernel(q_ref, k_ref, v_ref, bias_ref, o_ref,
                       qb_ref, kb_ref, vb_ref, acc_ref, m_ref, l_ref, *, seq):
    blk = ATT_BLOCK
    n_units = seq // blk
    offs, _ = _att_buf_rows(seq)
    lane = lax.broadcasted_iota(jnp.int32, (blk, LANES), 1)
    lo = lane < HEAD_DIM
    head_masks = (jnp.where(lo, 1.0, 0.0).astype(BF16), jnp.where(lo, 0.0, 1.0).astype(BF16))
    ones = jnp.ones((2 * blk, LANES), BF16)
    zeros = jnp.zeros((blk, LANES), BF16)

    def place(u, dil):
        r = lax.rem(u, dil)
        n = lax.div(u, dil)
        first = r + dil * blk * n
        base = offs[dil] + r * (seq // dil + blk) + blk * n
        return n, first, pl.multiple_of(base, blk)

    def rows(first, dil):
        return pl.ds(first, blk) if dil == 1 else pl.ds(first, blk, stride=dil)

    for dil in DILATIONS:
        for r in range(dil):
            z0 = offs[dil] + r * (seq // dil + blk)
            kb_ref[z0:z0 + blk, :] = zeros
            vb_ref[z0:z0 + blk, :] = zeros

        def gather(u, carry, dil=dil):
            _, first, base = place(u, dil)
            dst = pl.ds(base + blk, blk)
            qb_ref[dst, :] = (q_ref[0, rows(first, dil), :] * LOG2E).astype(BF16)
            kb_ref[dst, :] = k_ref[0, rows(first, dil), :].astype(BF16)
            vb_ref[dst, :] = v_ref[0, rows(first, dil), :].astype(BF16)
            return carry

        lax.fori_loop(0, n_units, gather, 0, unroll=GATHERS_PER_ITER)

    order = tuple(reversed(DILATIONS))
    for dil in order:
        def unit(u, carry, dil=dil):
            n, first, base = place(u, dil)
            qb = qb_ref[pl.ds(base + blk, blk), :]
            kcat = kb_ref[pl.ds(base, 2 * blk), :]
            vext = jnp.concatenate([vb_ref[pl.ds(base, 2 * blk), :], ones], axis=1)
            bias = bias_ref[jnp.minimum(n, 1)]
            os, ms, ls = [], [], []
            for hm in head_masks:
                s = _nt(qb * hm, kcat) + bias
                m = jnp.max(s, axis=-1, keepdims=True)
                p = jnp.exp2(s - m)
                res = jnp.dot(p.astype(BF16), vext, preferred_element_type=F32)
                os.append(res[:, :LANES])
                ls.append(res[:, LANES:])
                ms.append(m)
            o_new = jnp.where(lo, os[0], os[1])
            m_new = jnp.where(lo, ms[0], ms[1])
            l_new = jnp.where(lo, ls[0], ls[1])
            dst = rows(first, dil)
            if dil == order[0]:
                acc_ref[dst, :] = o_new
                m_ref[dst, :] = m_new
                l_ref[dst, :] = l_new
                return carry
            m_old = m_ref[dst, :]
            m_tot = jnp.maximum(m_old, m_new)
            a_old = jnp.exp2(m_old - m_tot)
            a_new = jnp.exp2(m_new - m_tot)
            acc = a_old * acc_ref[dst, :] + a_new * o_new
            den = a_old * l_ref[dst, :] + a_new * l_new
            if dil == order[-1]:
                o_ref[0, pl.ds(pl.multiple_of(first, blk), blk), :] = (acc / den).astype(o_ref.dtype)
            else:
                acc_ref[dst, :] = acc
                l_ref[dst, :] = den
                m_ref[dst, :] = m_tot
            return carry

        lax.fori_loop(0, n_units, unit, 0, unroll=UNITS_PER_ITER)


def _att_prompt_call(qkv):
    b, s, d3 = qkv.shape
    d_att = d3 // 3
    nblk = d_att // LANES
    bias = _band_bias()
    _, buf_rows = _att_buf_rows(s)
    slab = pl.BlockSpec((1, s, LANES), lambda i, j: (i, 0, j))
    part = lambda p: pl.BlockSpec((1, s, LANES), lambda i, j: (i, 0, p * nblk + j))
    kern = functools.partial(_att_prompt_kernel, seq=s)
    return pl.pallas_call(
        kern,
        grid=(b, d_att // LANES),
        in_specs=[part(0), part(1), part(2), pl.BlockSpec(bias.shape, lambda i, j: (0, 0, 0))],
        out_specs=slab,
        out_shape=jax.ShapeDtypeStruct((b, s, d_att), BF16),
        scratch_shapes=[pltpu.VMEM((buf_rows, LANES), BF16)] * 3 + [pltpu.VMEM((s, LANES), F32)] * 3,
        compiler_params=_cparams("parallel", "parallel"),
        name="att_prompt",
    )(qkv, qkv, qkv, bias)


def _gla_consts(blk):
    t = np.arange(LANES)
    same = (t[:, None] // blk) == (t[None, :] // blk)
    tri = ((t[None, :] <= t[:, None]) & same).astype(np.int64)
    mats, lvl, right_rows = [tri], [], []
    h = blk // 2
    while h >= 1:
        pair = t // (2 * h)
        right = (t // h) % 2 == 1
        mid = pair * 2 * h + h - 1
        upto_mid = ((t[None, :] <= mid[:, None]) & same).astype(np.int64)
        mats.append(tri - upto_mid)
        lvl.append(pair[:, None] == pair[None, :])
        right_rows.append(np.broadcast_to(right[:, None], (LANES, LANES)))
        h //= 2
    lvl.append(np.eye(LANES, dtype=bool))
    if blk < LANES:
        mats.append(same.astype(np.int64))
    cmat = jnp.asarray(np.concatenate(mats, 0), BF16)
    lvl = jnp.asarray(np.concatenate(lvl, 0), F32)
    rmask = jnp.asarray(np.concatenate(right_rows, 0), F32)
    return cmat, lvl, rmask, len(right_rows)


def _gla_tile_tasks(q, k, la, cmat_ref, lvl_ref, rmask_ref, n_lvl):
    t = LANES
    n_pair = q.shape[1] // LANES
    st = {}
    pairs = [None] * n_pair

    def cumulate():
        hi, mid, lo3 = _split3(la)
        cm = cmat_ref[...]
        dot = functools.partial(jnp.dot, preferred_element_type=F32)
        g = dot(cm, hi) + dot(cm, mid) + dot(cm, lo3)
        b = g[0:t]
        if cm.shape[0] > (n_lvl + 1) * t:
            btot = g[(n_lvl + 1) * t:(n_lvl + 2) * t]
        else:
            btot = jnp.broadcast_to(b[t - 1:t, :], b.shape)
        st.update(g=g, b=b, btot=btot)

    def pair(j):
        g, b, btot = st["g"], st["b"], st["btot"]
        lane = lax.broadcasted_iota(jnp.int32, (t, LANES), 1)
        lo = lane < GLA_DK
        sl = slice(j * LANES, (j + 1) * LANES)
        qp, kp, bp = q[:, sl], k[:, sl], b[:, sl]
        a0 = jnp.zeros((t, t), F32)
        a1 = jnp.zeros((t, t), F32)
        for lv in range(n_lvl + 1):
            m = lvl_ref[lv * t:(lv + 1) * t, :]
            if lv < n_lvl:
                dl = g[(lv + 1) * t:(lv + 2) * t, sl]
                e = jnp.exp(-jnp.abs(dl))
                rm = rmask_ref[lv * t:(lv + 1) * t, :]
                x = qp * e * rm
                y = (kp * e * (1.0 - rm)).astype(BF16)
            else:
                x = qp
                y = kp.astype(BF16)
            x0 = jnp.where(lo, x, 0.0).astype(BF16)
            x1 = jnp.where(lo, 0.0, x).astype(BF16)
            a0 = a0 + m * _nt(x0, y)
            a1 = a1 + m * _nt(x1, y)
        q_in = qp * jnp.exp(bp)
        k_in = kp * jnp.exp(btot[:, sl] - bp)
        pairs[j] = (a0, a1, q_in, k_in, btot[:, sl], lo)

    return [cumulate] + [functools.partial(pair, j) for j in range(n_pair)], pairs


def _gla_prompt_tasks(res, cmat_ref, lvl_ref, rmask_ref, o_ref, o_col, s_ref, *, n_lvl, d_gk):
    t = LANES
    n_pair = d_gk // LANES
    holder = {}

    def start():
        steps, pairs = _gla_tile_tasks(res["qg"], res["kg"], res["la"],
                                       cmat_ref, lvl_ref, rmask_ref, n_lvl)
        holder.update(steps=steps, pairs=pairs)
        steps[0]()

    def levels(j):
        holder["steps"][1 + j]()

    def finish(j):
        a0, a1, q_in, k_in, btot, lo = holder["pairs"][j]
        v = res["vg"]
        row = lax.broadcasted_iota(jnp.int32, (t, t), 0)
        top = row < GLA_DK
        s_prev = s_ref[j]
        sb = s_prev.astype(BF16)
        kt = k_in.T.astype(BF16)
        us = []
        for hh, (a, sel) in enumerate(((a0, lo), (a1, ~lo))):
            vs = slice((2 * j + hh) * GLA_DV, (2 * j + hh + 1) * GLA_DV)
            vb = v[:, vs].astype(BF16)
            qm = jnp.where(sel, q_in, 0.0).astype(BF16)
            o = (jnp.dot(a.astype(BF16), vb, preferred_element_type=F32)
                 + jnp.dot(qm, sb, preferred_element_type=F32))
            o_ref[0, :, o_col + vs.start:o_col + vs.stop] = o.astype(o_ref.dtype)
            us.append(jnp.dot(kt, vb, preferred_element_type=F32))
        a_col = jnp.exp(btot.T)
        s_ref[j] = a_col * s_prev + jnp.where(top, us[0], us[1])

    tasks = [start]
    for j in range(n_pair):
        tasks += [functools.partial(levels, j), functools.partial(finish, j)]
    return tasks


def _gla_sample_kernel(q_ref, k_ref, la_ref, v_ref, s_in_ref, cmat_ref, lvl_ref, rmask_ref,
                       o_ref, s_out_ref, *, n_lvl, t_dec):
    steps, pairs = _gla_tile_tasks(q_ref[...], k_ref[...], la_ref[...],
                                   cmat_ref, lvl_ref, rmask_ref, n_lvl)
    for step in steps:
        step()
    t = LANES
    nb = t // t_dec
    row = lax.broadcasted_iota(jnp.int32, (t, t), 0)
    col = lax.broadcasted_iota(jnp.int32, (t, t), 1)
    top = row < GLA_DK
    for j, (a0, a1, q_in, k_in, btot, lo) in enumerate(pairs):
        s_prev = s_in_ref[:, j]
        sb = s_prev.astype(BF16)
        kt = k_in.T
        bt = btot.T
        vbs = []
        for hh, (a, sel) in enumerate(((a0, lo), (a1, ~lo))):
            vs = slice((2 * j + hh) * GLA_DV, (2 * j + hh + 1) * GLA_DV)
            vb = v_ref[:, vs].astype(BF16)
            vbs.append(vb)
            qm = jnp.where(sel, q_in, 0.0)
            o_inter = jnp.concatenate(
                [jnp.dot(qm[i * t_dec:(i + 1) * t_dec].astype(BF16), sb[i],
                         preferred_element_type=F32) for i in range(nb)], axis=0)
            o_ref[:, vs] = jnp.dot(a.astype(BF16), vb, preferred_element_type=F32) + o_inter
        for i in range(nb):
            mine = (col >= i * t_dec) & (col < (i + 1) * t_dec)
            kti = (kt * mine.astype(F32)).astype(BF16)
            u0 = jnp.dot(kti, vbs[0], preferred_element_type=F32)
            u1 = jnp.dot(kti, vbs[1], preferred_element_type=F32)
            a_col = jnp.exp(jnp.broadcast_to(bt[:, i * t_dec:i * t_dec + 1], (t, GLA_DV)))
            s_out_ref[i, j] = a_col * s_prev[i] + jnp.where(top, u0, u1)


def _gla_sample_call(qg, kg, la, vg, state, t_dec):
    rows, d_gk = qg.shape
    d_gv = vg.shape[1]
    n_pair = d_gk // LANES
    nseq = rows // t_dec
    t = LANES
    nb = t // t_dec
    cmat, lvl, rmask, n_lvl = _gla_consts(t_dec)
    s_in = state.reshape(nseq, n_pair, 2 * GLA_DK, GLA_DV)
    tok = lambda w: pl.BlockSpec((t, w), lambda i: (i, 0))
    full = lambda a: pl.BlockSpec(a.shape, lambda i: (0,) * a.ndim)
    st = pl.BlockSpec((nb, n_pair, t, GLA_DV), lambda i: (i, 0, 0, 0))
    kern = functools.partial(_gla_sample_kernel, n_lvl=n_lvl, t_dec=t_dec)
    o, s_out = pl.pallas_call(
        kern,
        grid=(rows // t,),
        in_specs=[tok(d_gk), tok(d_gk), tok(d_gk), tok(d_gv), st, full(cmat), full(lvl), full(rmask)],
        out_specs=[tok(d_gv), st],
        out_shape=[jax.ShapeDtypeStruct((rows, d_gv), F32),
                   jax.ShapeDtypeStruct(s_in.shape, F32)],
        compiler_params=_cparams("parallel"),
        name="gla_sample",
    )(qg, kg, la, vg, s_in, cmat, lvl, rmask)
    return o, s_out.reshape(state.shape)


def _sample_att_table(w, t_dec):
    idx = np.arange(w + LANES)[None, :]
    tq = np.arange(t_dec)[:, None]
    dist = w + tq - idx
    mult = np.zeros((t_dec, w + LANES), np.int64)
    for d in DILATIONS:
        mult += (dist >= 0) & (dist % d == 0) & (dist // d <= STEPS) & (idx < w + t_dec)
    logw = np.where(mult > 0, np.log(np.maximum(mult, 1)), NEG).astype(np.float32)
    return jnp.asarray(logw)


def _att_sample_tasks(qkv_ref, ck_ref, cv_ref, logw_ref, o_ref, ko_ref, vo_ref, *, w, t_dec, n_heads):
    lane = lax.broadcasted_iota(jnp.int32, (HEAD_DIM, LANES), 1)
    keep = lane < LANES - t_dec
    d_att = n_heads * HEAD_DIM
    zpad = jnp.zeros((LANES - t_dec, d_att), F32)
    q = qkv_ref[:, 0:d_att]
    knt = jnp.concatenate([qkv_ref[:, d_att:2 * d_att], zpad], axis=0).T
    vnt = jnp.concatenate([qkv_ref[:, 2 * d_att:3 * d_att], zpad], axis=0).T
    logw_c = logw_ref[:, 0:w]
    logw_n = logw_ref[:, w:w + LANES]
    outs = [None] * n_heads

    def head(h):
        hs = slice(h * HEAD_DIM, (h + 1) * HEAD_DIM)
        kt = ck_ref[0, h]
        vt = cv_ref[0, h]
        knh = knt[hs, :]
        vnh = vnt[hs, :]
        qh = q[:, hs].astype(BF16)
        s_c = jnp.dot(qh, kt.astype(BF16), preferred_element_type=F32) + logw_c
        s_n = jnp.dot(qh, knh.astype(BF16), preferred_element_type=F32) + logw_n
        m = jnp.maximum(jnp.max(s_c, axis=-1, keepdims=True), jnp.max(s_n, axis=-1, keepdims=True))
        p_c = jnp.exp(s_c - m)
        p_n = jnp.exp(s_n - m)
        l = jnp.sum(p_c, axis=-1, keepdims=True) + jnp.sum(p_n, axis=-1, keepdims=True)
        o = _nt(p_c.astype(BF16), vt.astype(BF16)) + _nt(p_n.astype(BF16), vnh.astype(BF16))
        outs[h] = o / l
        for src, new, dst in ((kt, knh, ko_ref), (vt, vnh, vo_ref)):
            sh = pltpu.roll(src, w - t_dec, axis=1)
            tail = jnp.where(keep, sh[:, w - LANES:], pltpu.roll(new, LANES - t_dec, axis=1))
            dst[0, h, :, 0:w - LANES] = sh[:, 0:w - LANES]
            dst[0, h, :, w - LANES:w] = tail

    def finish():
        o_ref[...] = jnp.concatenate(outs, axis=1)

    return [functools.partial(head, h) for h in range(n_heads)] + [finish]


def _stream_kernel(sqkv_ref, ck_ref, cv_ref, logw_ref,
                   x_ref, shsc_ref, gpre_ref, w_ref, wa2_ref, ba2_ref, tab_ref,
                   cmat_ref, lvl_ref, rmask_ref,
                   o_ref, ko_ref, vo_ref, pqkv_ref, gates_ref, kt_ref, vt_ref, s_out_ref,
                   s_ref, *, w, t_dec, n_heads, n_lvl, n_tiles, d_att, d_gk, d_gv):
    n = lax.rem(pl.program_id(0), n_tiles)

    @pl.when(n == 0)
    def _():
        s_ref[...] = jnp.zeros_like(s_ref)

    win_steps = _att_sample_tasks(sqkv_ref, ck_ref, cv_ref, logw_ref, o_ref, ko_ref, vo_ref,
                                  w=w, t_dec=t_dec, n_heads=n_heads)
    proj_steps, res = _proj_tasks(x_ref[0], shsc_ref[0], gpre_ref, w_ref, wa2_ref, ba2_ref, tab_ref,
                                  d_att=d_att, d_gk=d_gk, d_gv=d_gv)

    def emit():
        for i, name in enumerate(("q", "k", "v")):
            pqkv_ref[0, :, i * d_att:(i + 1) * d_att] = res[name]
        gates_ref[0, :, 0:d_att] = res["ga"].astype(gates_ref.dtype)
        gates_ref[0, :, d_att:d_att + d_gv] = res["gg"].astype(gates_ref.dtype)
        kt_ref[0] = res["k"].T
        vt_ref[0] = res["v"].T

    tile_steps = proj_steps + [emit] + _gla_prompt_tasks(
        res, cmat_ref, lvl_ref, rmask_ref, gates_ref, d_att + d_gv, s_ref, n_lvl=n_lvl, d_gk=d_gk)
    for step in win_steps + tile_steps:
        step()

    @pl.when(n == n_tiles - 1)
    def _():
        s_out_ref[0] = s_ref[...]


def _stream_call(sqkv, cache_k, cache_v, t_dec, x, mod_p, gpre, w_pad, wa2_pad, ba2, table,
                 *, d_att, d_gk, d_gv):
    nseq, w, n_heads, hd = cache_k.shape
    b, s, d = x.shape
    t = LANES
    n_tiles = s // t
    n_pair = d_gk // LANES
    first_win_tile = (s - w) // t
    assert nseq == b * n_tiles, "one 128-row prompt tile is paired with each sample sequence"
    ck = jnp.transpose(cache_k, (0, 2, 3, 1))
    cv = jnp.transpose(cache_v, (0, 2, 3, 1))
    logw = _sample_att_table(w, t_dec)
    cmat, lvl, rmask, n_lvl = _gla_consts(LANES)
    tok = lambda wd: pl.BlockSpec((t_dec, wd), lambda i: (i, 0))
    win = pl.BlockSpec((1, n_heads, hd, w), lambda i: (i, 0, 0, 0))
    once = lambda a: pl.BlockSpec(a.shape, lambda i: (0,) * a.ndim, pipeline_mode=pl.Buffered(1))
    ptile = lambda wd: pl.BlockSpec((1, t, wd), lambda i: (i // n_tiles, i % n_tiles, 0))
    pwin = pl.BlockSpec((1, d_att, t),
                        lambda i: (i // n_tiles, 0, jnp.maximum(i % n_tiles - first_win_tile, 0)))
    kern = functools.partial(_stream_kernel, w=w, t_dec=t_dec, n_heads=n_heads, n_lvl=n_lvl,
                             n_tiles=n_tiles, d_att=d_att, d_gk=d_gk, d_gv=d_gv)
    f32 = lambda *shape: jax.ShapeDtypeStruct(shape, F32)
    bf16 = lambda *shape: jax.ShapeDtypeStruct(shape, BF16)
    outs = pl.pallas_call(
        kern,
        grid=(nseq,),
        in_specs=[tok(3 * d_att), win, win, once(logw),
                  ptile(d), pl.BlockSpec((1, 1, 2 * d), lambda i: (i // n_tiles, 0, 0)),
                  once(gpre), once(w_pad), once(wa2_pad), once(ba2),
                  pl.BlockSpec((t, 3 * LANES), lambda i: (i % n_tiles, 0)),
                  once(cmat), once(lvl), once(rmask)],
        out_specs=[tok(d_att), win, win, ptile(3 * d_att), ptile(d_att + 2 * d_gv), pwin, pwin,
                   pl.BlockSpec((1, n_pair, t, GLA_DV), lambda i: (i // n_tiles, 0, 0, 0))],
        out_shape=[f32(nseq * t_dec, d_att), f32(*ck.shape), f32(*cv.shape),
                   f32(b, s, 3 * d_att), bf16(b, s, d_att + 2 * d_gv),
                   f32(b, d_att, w), f32(b, d_att, w), f32(b, n_pair, t, GLA_DV)],
        scratch_shapes=[pltpu.VMEM((n_pair, t, GLA_DV), F32)],
        compiler_params=pltpu.CompilerParams(dimension_semantics=("arbitrary",),
                                             vmem_limit_bytes=STREAM_VMEM_LIMIT),
        name="stream",
    )(sqkv, ck, cv, logw, x, mod_p, gpre, w_pad, wa2_pad, ba2, table, cmat, lvl, rmask)
    o, ko, vo, pqkv, gates, kt, vt, s_out = outs
    to_win = lambda a: jnp.transpose(a, (0, 3, 1, 2))
    return (o, to_win(ko), to_win(vo), pqkv, gates,
            to_win(kt.reshape(b, n_heads, hd, w)), to_win(vt.reshape(b, n_heads, hd, w)),
            s_out.reshape(b, 2 * n_pair, GLA_DK, GLA_DV))


def _finish_kernel(x_ref, gate_ref, att_ref, ga_ref, og_ref, gg_ref, ggla_ref, gpost_ref, wo_ref,
                   y_ref, *, d_att, d_gv):
    att = att_ref[0] * ga_ref[0]
    og = og_ref[0].astype(F32)
    gg = gg_ref[0].astype(F32)
    y = jnp.dot(att.astype(BF16), wo_ref[0:d_att, :], preferred_element_type=F32)
    for h in range(d_gv // GLA_DV):
        sl = slice(h * GLA_DV, (h + 1) * GLA_DV)
        oh = og[:, sl]
        nh = oh * lax.rsqrt(jnp.mean(oh * oh, axis=-1, keepdims=True) + EPS) * ggla_ref[...]
        gl = (nh * gg[:, sl]).astype(BF16)
        y = y + jnp.dot(gl, wo_ref[d_att + h * GLA_DV:d_att + (h + 1) * GLA_DV, :],
                        preferred_element_type=F32)
    yn = y * lax.rsqrt(jnp.mean(y * y, axis=-1, keepdims=True) + EPS) * gpost_ref[...]
    y_ref[0] = x_ref[0] + gate_ref[0] * yn


def _finish_call(x3, mod3, att, ga, og, gg, g_gla, g_post, w_out, *, tm, d_att, d_gv):
    g, r, d = x3.shape
    per_row = mod3.shape[1] != 1
    if per_row:
        gate_spec = pl.BlockSpec((1, tm, d), lambda i, j: (i, j, 2))
    else:
        gate_spec = pl.BlockSpec((1, 1, d), lambda i, j: (i, 0, 2))
    tok = lambda w, c=0: pl.BlockSpec((1, tm, w), lambda i, j: (i, j, c))
    full = lambda a: pl.BlockSpec(a.shape, lambda i, j: (0,) * a.ndim)
    kern = functools.partial(_finish_kernel, d_att=d_att, d_gv=d_gv)
    return pl.pallas_call(
        kern,
        grid=(g, r // tm),
        in_specs=[tok(d), gate_spec, tok(d_att), tok(d_att, ga[1]), tok(d_gv, og[1]), tok(d_gv, gg[1]),
                  full(g_gla), full(g_post), full(w_out)],
        out_specs=tok(d),
        out_shape=jax.ShapeDtypeStruct((g, r, d), F32),
        compiler_params=_cparams("parallel", "arbitrary"),
        name="finish",
    )(x3, mod3, att, ga[0], og[0], gg[0], g_gla, g_post, w_out)


def kernel(x_prompt, x_sample, c_prompt, c_sample, cache_k_win, cache_v_win, state_gla,
           w_mod, b_mod, g_pre, g_post, w_in, w_a2, b_a2, g_gla, w_out):
    b, s, d = x_prompt.shape
    nseq, t_dec, _ = x_sample.shape
    w = cache_k_win.shape[1]
    d_att = cache_k_win.shape[2] * cache_k_win.shape[3]
    d_gk = w_a2.shape[1]
    d_gv = state_gla.shape[1] * state_gla.shape[3]
    d_main = 4 * d_att + 2 * d_gk + 2 * d_gv
    assert w_in.shape[1] == d_main + GLA_GATE_RANK and cache_k_win.shape[3] == HEAD_DIM
    assert state_gla.shape[2] == GLA_DK and state_gla.shape[3] == GLA_DV
    assert w >= STEPS * max(DILATIONS) and s % (ATT_BLOCK * max(DILATIONS)) == 0 and s >= w

    w_pad = jnp.pad(w_in, ((0, 0), (0, LANES - GLA_GATE_RANK))).astype(BF16)
    wa2_pad = jnp.pad(w_a2, ((0, LANES - GLA_GATE_RANK), (0, 0)))
    wo = w_out.astype(BF16)
    gpre = g_pre.reshape(1, d)
    gpost = g_post.reshape(1, d)
    ggla = g_gla.reshape(1, GLA_DV)
    ba2 = b_a2.reshape(1, d_gk)

    n_c = b + nseq
    n_c_pad = -(-n_c // SUBLANES) * SUBLANES
    c_all = jnp.pad(jnp.concatenate([c_prompt, c_sample], axis=0), ((0, n_c_pad - n_c), (0, 0)))
    mod = _mod_call(c_all, w_mod, b_mod)
    mod_p = mod[:b].reshape(b, 1, 3 * d)

    dims = dict(d_att=d_att, d_gk=d_gk, d_gv=d_gv)

    rows = nseq * t_dec
    tms = min(512, rows)
    xs = x_sample.reshape(rows // tms, tms, d)
    mod_s = jnp.repeat(mod[b:b + nseq], t_dec, axis=0).reshape(rows // tms, tms, 3 * d)
    tab_s = _rope_tables(jnp.tile(PAST_LEN + jnp.arange(t_dec), tms // t_dec))
    outs = _proj_call(xs, mod_s, gpre, w_pad, wa2_pad, ba2, tab_s, tm=tms, **dims)
    qkv_s, ga_s, qg_s, kg_s, la_s, vg_s, gg_s = [o.reshape(rows, o.shape[2]) for o in outs]

    tab_p = _rope_tables(jnp.arange(s))
    (o_att_s, k_win_sample, v_win_sample, qkv, gates, k_win_prompt, v_win_prompt,
     gla_p) = _stream_call(qkv_s, cache_k_win, cache_v_win, t_dec,
                           x_prompt, mod_p, gpre, w_pad, wa2_pad, ba2, tab_p, **dims)

    att = _att_prompt_call(qkv)
    og_s, gla_s = _gla_sample_call(qg_s, kg_s, la_s, vg_s, state_gla, t_dec)

    assert d_att == d_gv
    y_prompt = _finish_call(x_prompt, mod_p, att, (gates, 0), (gates, 2), (gates, 1),
                            ggla, gpost, wo, tm=512, d_att=d_att, d_gv=d_gv)
    r3 = lambda a: a.reshape(rows // tms, tms, a.shape[1])
    y_sample = _finish_call(xs, mod_s, r3(o_att_s), (r3(ga_s), 0), (r3(og_s), 0), (r3(gg_s), 0),
                            ggla, gpost, wo, tm=tms, d_att=d_att, d_gv=d_gv).reshape(x_sample.shape)

    return (y_prompt, y_sample, k_win_prompt, v_win_prompt, gla_p,
            k_win_sample, v_win_sample, gla_s)
```

```python
import functools
import math

import numpy as np
import jax
import jax.numpy as jnp
from jax import lax
from jax.experimental import pallas as pl
from jax.experimental.pallas import tpu as pltpu

F32 = jnp.float32
BF16 = jnp.bfloat16

HEAD_DIM = 64
ROT_DIM = HEAD_DIM // 4
ROPE_THETA = 500000.0
DILATIONS = (1, 4, 16)
STEPS = 128
ATT_BLOCK = 128
UNITS_PER_ITER = 32
GATHERS_PER_ITER = 4
STAGED_DILATION = 4
PAST_LEN = 8192
GLA_DK = 64
GLA_DV = 128
GLA_GATE_RANK = 16
GLA_TAU = 16.0
EPS = 1e-6
NEG = -1e30
LOG2E = math.log2(math.e)

LANES = 128
SUBLANES = 8
VMEM_LIMIT = 48 * 1024 * 1024
STREAM_VMEM_LIMIT = 56 * 1024 * 1024


def _cparams(*sem):
    return pltpu.CompilerParams(dimension_semantics=sem, vmem_limit_bytes=VMEM_LIMIT)


def _nt(a, b):
    return lax.dot_general(a, b, (((1,), (1,)), ((), ())), preferred_element_type=F32)


def _split3(x):
    hi = x.astype(BF16)
    r1 = x - hi.astype(F32)
    mid = r1.astype(BF16)
    lo = (r1 - mid.astype(F32)).astype(BF16)
    return hi, mid, lo


def _mod_kernel(c_ref, w_ref, b_ref, o_ref):
    c = c_ref[...]
    a = c / (1.0 + jnp.exp(-c))
    o_ref[...] = jnp.dot(a, w_ref[...], precision=lax.Precision.HIGHEST,
                         preferred_element_type=F32) + b_ref[...]


def _mod_call(c, w_mod, b_mod):
    n, d = c.shape
    d3 = w_mod.shape[1]
    tn = 512
    return pl.pallas_call(
        _mod_kernel,
        grid=(d3 // tn,),
        in_specs=[pl.BlockSpec((n, d), lambda j: (0, 0)),
                  pl.BlockSpec((d, tn), lambda j: (0, j)),
                  pl.BlockSpec((1, tn), lambda j: (0, j))],
        out_specs=pl.BlockSpec((n, tn), lambda j: (0, j)),
        out_shape=jax.ShapeDtypeStruct((n, d3), F32),
        compiler_params=_cparams("arbitrary"),
        name="mod",
    )(c, w_mod, b_mod.reshape(1, d3))


def _rope_tables(pos):
    half = ROT_DIM // 2
    inv = 1.0 / (ROPE_THETA ** (jnp.arange(half, dtype=F32) / half))
    ang = pos.astype(F32)[:, None] * inv[None, :]
    cos, sin = jnp.cos(ang), jnp.sin(ang)
    n = pos.shape[0]
    ones = jnp.ones((n, HEAD_DIM - ROT_DIM), F32)
    zeros_h = jnp.zeros((n, half), F32)
    zeros_r = jnp.zeros((n, HEAD_DIM - ROT_DIM), F32)
    c = jnp.concatenate([cos, cos, ones], axis=1)
    s1 = jnp.concatenate([-sin, zeros_h, zeros_r], axis=1)
    s2 = jnp.concatenate([zeros_h, sin, zeros_r], axis=1)
    rep = LANES // HEAD_DIM
    return jnp.concatenate([jnp.tile(t, (1, rep)) for t in (c, s1, s2)], axis=1)


def _proj_tasks(x, shsc, gpre_ref, w_ref, wa2_ref, ba2_ref, tab_ref, *, d_att, d_gk, d_gv):
    d = x.shape[1]
    ms = jnp.mean(x * x, axis=-1, keepdims=True)
    h = x * lax.rsqrt(ms + EPS) * gpre_ref[...]
    h = h * (1.0 + shsc[:, d:]) + shsc[:, :d]
    hb = h.astype(BF16)
    reps = d_att // LANES
    half = ROT_DIM // 2
    res = {}

    def seg(a, n):
        return jnp.dot(hb, w_ref[:, a:a + n], preferred_element_type=F32)

    def rot(t):
        rc = jnp.concatenate([tab_ref[:, 0:LANES]] * reps, axis=1)
        rs1 = jnp.concatenate([tab_ref[:, LANES:2 * LANES]] * reps, axis=1)
        rs2 = jnp.concatenate([tab_ref[:, 2 * LANES:3 * LANES]] * reps, axis=1)
        up = pltpu.roll(t, d_att - half, axis=1)
        dn = pltpu.roll(t, half, axis=1)
        return t * rc + up * rs1 + dn * rs2

    def silu(t):
        return t / (1.0 + jnp.exp(-t))

    def log_decay(t):
        a1, a2, a3 = _split3(t)
        w1, w2, w3 = _split3(wa2_ref[...])
        dot = functools.partial(jnp.dot, preferred_element_type=F32)
        z = (dot(a1, w1) + dot(a1, w2) + dot(a2, w1)
             + dot(a1, w3) + dot(a2, w2) + dot(a3, w1)) + ba2_ref[...]
        logsig = jnp.minimum(z, 0.0) - jnp.log1p(jnp.exp(-jnp.abs(z)))
        return logsig * (1.0 / GLA_TAU)

    plan = (("q", d_att, lambda t: rot(t) * (HEAD_DIM ** -0.5)), ("k", d_att, rot),
            ("v", d_att, None), ("ga", d_att, silu),
            ("qg", d_gk, lambda t: t * (GLA_DK ** -0.5)), ("kg", d_gk, None),
            ("vg", d_gv, None), ("gg", d_gv, silu), ("la", LANES, log_decay))
    steps, off = [], 0
    for name, width, post in plan:
        def step(name=name, width=width, post=post, off=off):
            t = seg(off, width)
            res[name] = t if post is None else post(t)
        steps.append(step)
        off += width
    return steps, res


def _proj_kernel(x_ref, shsc_ref, gpre_ref, w_ref, wa2_ref, ba2_ref, tab_ref,
                 qkv_ref, *out_refs, d_att, d_gk, d_gv):
    steps, res = _proj_tasks(x_ref[0], shsc_ref[0], gpre_ref, w_ref, wa2_ref, ba2_ref, tab_ref,
                             d_att=d_att, d_gk=d_gk, d_gv=d_gv)
    for step in steps:
        step()
    for i, name in enumerate(("q", "k", "v")):
        qkv_ref[0, :, i * d_att:(i + 1) * d_att] = res[name]
    for ref, name in zip(out_refs, ("ga", "qg", "kg", "la", "vg", "gg")):
        ref[0] = res[name]


def _proj_call(x3, mod3, gpre, w_pad, wa2_pad, ba2, table, *, d_att, d_gk, d_gv, tm):
    g, r, d = x3.shape
    grid = (g, r // tm)
    full = lambda a: pl.BlockSpec(a.shape, lambda i, j: (0,) * a.ndim)
    widths = (3 * d_att, d_att, d_gk, d_gk, d_gk, d_gv, d_gv)
    kern = functools.partial(_proj_kernel, d_att=d_att, d_gk=d_gk, d_gv=d_gv)
    return pl.pallas_call(
        kern,
        grid=grid,
        in_specs=[pl.BlockSpec((1, tm, d), lambda i, j: (i, j, 0)),
                  pl.BlockSpec((1, tm, 2 * d), lambda i, j: (i, j, 0)),
                  full(gpre), full(w_pad), full(wa2_pad), full(ba2),
                  pl.BlockSpec((tm, 3 * LANES), lambda i, j: (0, 0))],
        out_specs=[pl.BlockSpec((1, tm, w), lambda i, j: (i, j, 0)) for w in widths],
        out_shape=[jax.ShapeDtypeStruct((g, r, w), F32) for w in widths],
        compiler_params=_cparams("parallel", "arbitrary"),
        name="proj",
    )(x3, mod3, gpre, w_pad, wa2_pad, ba2, table)


def _band_bias():
    row = np.arange(ATT_BLOCK)[:, None]
    col = np.arange(2 * ATT_BLOCK)[None, :]
    dist = row + ATT_BLOCK - col
    band = (dist >= 0) & (dist <= STEPS)
    first = band & (col >= ATT_BLOCK)
    return jnp.asarray(np.where(np.stack([first, band]), 0.0, NEG), F32)


def _att_buf_rows(seq):
    offs, total = {}, 0
    for dil in DILATIONS:
        offs[dil] = total
        total += seq + dil * ATT_BLOCK
    return offs, total


def _att_prompt_kernel(q_ref, k_ref, v_ref, bias_ref, o_ref,
                       qb_ref, kb_ref, vb_ref, acc_ref, m_ref, l_ref, *stage_refs, seq):
    blk = ATT_BLOCK
    n_units = seq // blk
    offs, _ = _att_buf_rows(seq)
    lane = lax.broadcasted_iota(jnp.int32, (blk, LANES), 1)
    lo = lane < HEAD_DIM
    head_masks = (jnp.where(lo, 1.0, 0.0).astype(BF16), jnp.where(lo, 0.0, 1.0).astype(BF16))
    ones = jnp.ones((2 * blk, LANES), BF16)
    zeros = jnp.zeros((blk, LANES), BF16)

    def place(u, dil):
        r = lax.rem(u, dil)
        n = lax.div(u, dil)
        first = r + dil * blk * n
        base = offs[dil] + r * (seq // dil + blk) + blk * n
        return n, first, pl.multiple_of(base, blk)

    def rows(first, dil):
        return pl.ds(first, blk) if dil == 1 else pl.ds(first, blk, stride=dil)

    srcs = (q_ref, k_ref, v_ref)
    bufs = (qb_ref, kb_ref, vb_ref)
    scales = (LOG2E, None, None)
    parent = {d: max([p for p in DILATIONS if 1 < p < d and d % p == 0], default=None) for d in DILATIONS}
    assert set(parent.values()) <= {None, STAGED_DILATION}
    for dil in DILATIONS:
        for r in range(dil):
            z0 = offs[dil] + r * (seq // dil + blk)
            kb_ref[z0:z0 + blk, :] = zeros
            vb_ref[z0:z0 + blk, :] = zeros

        def gather(u, carry, dil=dil):
            n, first, base = place(u, dil)
            r = lax.rem(u, dil)
            par = parent[dil]
            for a in range(3):
                if par is None:
                    x = srcs[a][0, rows(first, dil), :]
                else:
                    step = dil // par
                    start = lax.rem(r, par) * (seq // par) + lax.div(r, par) + step * blk * n
                    x = stage_refs[a][pl.ds(start, blk, stride=step), :]
                if dil == STAGED_DILATION:
                    stage_refs[a][pl.ds(pl.multiple_of(r * (seq // dil) + blk * n, blk), blk), :] = x
                if scales[a] is not None:
                    x = x * scales[a]
                bufs[a][pl.ds(base + blk, blk), :] = x.astype(BF16)
            return carry

        lax.fori_loop(0, n_units, gather, 0, unroll=GATHERS_PER_ITER)

    order = tuple(reversed(DILATIONS))
    for dil in order:
        def unit(u, carry, dil=dil):
            n, first, base = place(u, dil)
            qb = qb_ref[pl.ds(base + blk, blk), :]
            kcat = kb_ref[pl.ds(base, 2 * blk), :]
            vext = jnp.concatenate([vb_ref[pl.ds(base, 2 * blk), :], ones], axis=1)
            bias = bias_ref[jnp.minimum(n, 1)]
            os, ms, ls = [], [], []
            for hm in head_masks:
                s = _nt(qb * hm, kcat) + bias
                m = jnp.max(s, axis=-1, keepdims=True)
                p = jnp.exp2(s - m)
                res = jnp.dot(p.astype(BF16), vext, preferred_element_type=F32)
                os.append(res[:, :LANES])
                ls.append(res[:, LANES:])
                ms.append(m)
            o_new = jnp.where(lo, os[0], os[1])
            m_new = jnp.where(lo, ms[0], ms[1])
            l_new = jnp.where(lo, ls[0], ls[1])
            dst = rows(first, dil)
            if dil == order[0]:
                acc_ref[dst, :] = o_new
                m_ref[dst, :] = m_new
                l_ref[dst, :] = l_new
                return carry
            m_old = m_ref[dst, :]
            m_tot = jnp.maximum(m_old, m_new)
            a_old = jnp.exp2(m_old - m_tot)
            a_new = jnp.exp2(m_new - m_tot)
            acc = a_old * acc_ref[dst, :] + a_new * o_new
            den = a_old * l_ref[dst, :] + a_new * l_new
            if dil == order[-1]:
                o_ref[0, pl.ds(pl.multiple_of(first, blk), blk), :] = (acc / den).astype(o_ref.dtype)
            else:
                acc_ref[dst, :] = acc
                l_ref[dst, :] = den
                m_ref[dst, :] = m_tot
            return carry

        lax.fori_loop(0, n_units, unit, 0, unroll=UNITS_PER_ITER)


def _att_prompt_call(qkv):
    b, s, d3 = qkv.shape
    d_att = d3 // 3
    nblk = d_att // LANES
    bias = _band_bias()
    _, buf_rows = _att_buf_rows(s)
    slab = pl.BlockSpec((1, s, LANES), lambda i, j: (i, 0, j))
    part = lambda p: pl.BlockSpec((1, s, LANES), lambda i, j: (i, 0, p * nblk + j))
    kern = functools.partial(_att_prompt_kernel, seq=s)
    return pl.pallas_call(
        kern,
        grid=(b, d_att // LANES),
        in_specs=[part(0), part(1), part(2), pl.BlockSpec(bias.shape, lambda i, j: (0, 0, 0))],
        out_specs=slab,
        out_shape=jax.ShapeDtypeStruct((b, s, d_att), BF16),
        scratch_shapes=[pltpu.VMEM((buf_rows, LANES), BF16)] * 3 + [pltpu.VMEM((s, LANES), F32)] * 6,
        compiler_params=_cparams("parallel", "parallel"),
        name="att_prompt",
    )(qkv, qkv, qkv, bias)


def _gla_consts(blk):
    t = np.arange(LANES)
    same = (t[:, None] // blk) == (t[None, :] // blk)
    tri = ((t[None, :] <= t[:, None]) & same).astype(np.int64)
    mats, lvl, right_rows = [tri], [], []
    h = blk // 2
    while h >= 1:
        pair = t // (2 * h)
        right = (t // h) % 2 == 1
        mid = pair * 2 * h + h - 1
        upto_mid = ((t[None, :] <= mid[:, None]) & same).astype(np.int64)
        mats.append(tri - upto_mid)
        lvl.append(pair[:, None] == pair[None, :])
        right_rows.append(np.broadcast_to(right[:, None], (LANES, LANES)))
        h //= 2
    lvl.append(np.eye(LANES, dtype=bool))
    if blk < LANES:
        mats.append(same.astype(np.int64))
    cmat = jnp.asarray(np.concatenate(mats, 0), BF16)
    lvl = jnp.asarray(np.concatenate(lvl, 0), F32)
    rmask = jnp.asarray(np.concatenate(right_rows, 0), F32)
    return cmat, lvl, rmask, len(right_rows)


def _gla_tile_tasks(q, k, la, cmat_ref, lvl_ref, rmask_ref, n_lvl):
    t = LANES
    n_pair = q.shape[1] // LANES
    st = {}
    pairs = [None] * n_pair

    def cumulate():
        hi, mid, lo3 = _split3(la)
        cm = cmat_ref[...]
        dot = functools.partial(jnp.dot, preferred_element_type=F32)
        g = dot(cm, hi) + dot(cm, mid) + dot(cm, lo3)
        b = g[0:t]
        if cm.shape[0] > (n_lvl + 1) * t:
            btot = g[(n_lvl + 1) * t:(n_lvl + 2) * t]
        else:
            btot = jnp.broadcast_to(b[t - 1:t, :], b.shape)
        st.update(g=g, b=b, btot=btot)

    def pair(j):
        g, b, btot = st["g"], st["b"], st["btot"]
        lane = lax.broadcasted_iota(jnp.int32, (t, LANES), 1)
        lo = lane < GLA_DK
        sl = slice(j * LANES, (j + 1) * LANES)
        qp, kp, bp = q[:, sl], k[:, sl], b[:, sl]
        a0 = jnp.zeros((t, t), F32)
        a1 = jnp.zeros((t, t), F32)
        for lv in range(n_lvl + 1):
            m = lvl_ref[lv * t:(lv + 1) * t, :]
            if lv < n_lvl:
                dl = g[(lv + 1) * t:(lv + 2) * t, sl]
                e = jnp.exp(-jnp.abs(dl))
                rm = rmask_ref[lv * t:(lv + 1) * t, :]
                x = qp * e * rm
                y = (kp * e * (1.0 - rm)).astype(BF16)
            else:
                x = qp
                y = kp.astype(BF16)
            x0 = jnp.where(lo, x, 0.0).astype(BF16)
            x1 = jnp.where(lo, 0.0, x).astype(BF16)
            a0 = a0 + m * _nt(x0, y)
            a1 = a1 + m * _nt(x1, y)
        q_in = qp * jnp.exp(bp)
        k_in = kp * jnp.exp(btot[:, sl] - bp)
        pairs[j] = (a0, a1, q_in, k_in, btot[:, sl], lo)

    return [cumulate] + [functools.partial(pair, j) for j in range(n_pair)], pairs


def _gla_prompt_tasks(res, cmat_ref, lvl_ref, rmask_ref, o_ref, o_col, s_ref, *, n_lvl, d_gk):
    t = LANES
    n_pair = d_gk // LANES
    holder = {}

    def start():
        steps, pairs = _gla_tile_tasks(res["qg"], res["kg"], res["la"],
                                       cmat_ref, lvl_ref, rmask_ref, n_lvl)
        holder.update(steps=steps, pairs=pairs)
        steps[0]()

    def levels(j):
        holder["steps"][1 + j]()

    def finish(j):
        a0, a1, q_in, k_in, btot, lo = holder["pairs"][j]
        v = res["vg"]
        row = lax.broadcasted_iota(jnp.int32, (t, t), 0)
        top = row < GLA_DK
        s_prev = s_ref[j]
        sb = s_prev.astype(BF16)
        kt = k_in.T.astype(BF16)
        us = []
        for hh, (a, sel) in enumerate(((a0, lo), (a1, ~lo))):
            vs = slice((2 * j + hh) * GLA_DV, (2 * j + hh + 1) * GLA_DV)
            vb = v[:, vs].astype(BF16)
            qm = jnp.where(sel, q_in, 0.0).astype(BF16)
            o = (jnp.dot(a.astype(BF16), vb, preferred_element_type=F32)
                 + jnp.dot(qm, sb, preferred_element_type=F32))
            o_ref[0, :, o_col + vs.start:o_col + vs.stop] = o.astype(o_ref.dtype)
            us.append(jnp.dot(kt, vb, preferred_element_type=F32))
        a_col = jnp.exp(btot.T)
        s_ref[j] = a_col * s_prev + jnp.where(top, us[0], us[1])

    tasks = [start]
    for j in range(n_pair):
        tasks += [functools.partial(levels, j), functools.partial(finish, j)]
    return tasks


def _gla_sample_kernel(q_ref, k_ref, la_ref, v_ref, s_in_ref, cmat_ref, lvl_ref, rmask_ref,
                       o_ref, s_out_ref, *, n_lvl, t_dec):
    steps, pairs = _gla_tile_tasks(q_ref[...], k_ref[...], la_ref[...],
                                   cmat_ref, lvl_ref, rmask_ref, n_lvl)
    for step in steps:
        step()
    t = LANES
    nb = t // t_dec
    row = lax.broadcasted_iota(jnp.int32, (t, t), 0)
    col = lax.broadcasted_iota(jnp.int32, (t, t), 1)
    top = row < GLA_DK
    for j, (a0, a1, q_in, k_in, btot, lo) in enumerate(pairs):
        s_prev = s_in_ref[:, j]
        sb = s_prev.astype(BF16)
        kt = k_in.T
        bt = btot.T
        vbs = []
        for hh, (a, sel) in enumerate(((a0, lo), (a1, ~lo))):
            vs = slice((2 * j + hh) * GLA_DV, (2 * j + hh + 1) * GLA_DV)
            vb = v_ref[:, vs].astype(BF16)
            vbs.append(vb)
            qm = jnp.where(sel, q_in, 0.0)
            o_inter = jnp.concatenate(
                [jnp.dot(qm[i * t_dec:(i + 1) * t_dec].astype(BF16), sb[i],
                         preferred_element_type=F32) for i in range(nb)], axis=0)
            o_ref[:, vs] = jnp.dot(a.astype(BF16), vb, preferred_element_type=F32) + o_inter
        for i in range(nb):
            mine = (col >= i * t_dec) & (col < (i + 1) * t_dec)
            kti = (kt * mine.astype(F32)).astype(BF16)
            u0 = jnp.dot(kti, vbs[0], preferred_element_type=F32)
            u1 = jnp.dot(kti, vbs[1], preferred_element_type=F32)
            a_col = jnp.exp(jnp.broadcast_to(bt[:, i * t_dec:i * t_dec + 1], (t, GLA_DV)))
            s_out_ref[i, j] = a_col * s_prev[i] + jnp.where(top, u0, u1)


def _gla_sample_call(qg, kg, la, vg, state, t_dec):
    rows, d_gk = qg.shape
    d_gv = vg.shape[1]
    n_pair = d_gk // LANES
    nseq = rows // t_dec
    t = LANES
    nb = t // t_dec
    cmat, lvl, rmask, n_lvl = _gla_consts(t_dec)
    s_in = state.reshape(nseq, n_pair, 2 * GLA_DK, GLA_DV)
    tok = lambda w: pl.BlockSpec((t, w), lambda i: (i, 0))
    full = lambda a: pl.BlockSpec(a.shape, lambda i: (0,) * a.ndim)
    st = pl.BlockSpec((nb, n_pair, t, GLA_DV), lambda i: (i, 0, 0, 0))
    kern = functools.partial(_gla_sample_kernel, n_lvl=n_lvl, t_dec=t_dec)
    o, s_out = pl.pallas_call(
        kern,
        grid=(rows // t,),
        in_specs=[tok(d_gk), tok(d_gk), tok(d_gk), tok(d_gv), st, full(cmat), full(lvl), full(rmask)],
        out_specs=[tok(d_gv), st],
        out_shape=[jax.ShapeDtypeStruct((rows, d_gv), F32),
                   jax.ShapeDtypeStruct(s_in.shape, F32)],
        compiler_params=_cparams("parallel"),
        name="gla_sample",
    )(qg, kg, la, vg, s_in, cmat, lvl, rmask)
    return o, s_out.reshape(state.shape)


def _sample_att_table(w, t_dec):
    idx = np.arange(w + LANES)[None, :]
    tq = np.arange(t_dec)[:, None]
    dist = w + tq - idx
    mult = np.zeros((t_dec, w + LANES), np.int64)
    for d in DILATIONS:
        mult += (dist >= 0) & (dist % d == 0) & (dist // d <= STEPS) & (idx < w + t_dec)
    logw = np.where(mult > 0, np.log(np.maximum(mult, 1)), NEG).astype(np.float32)
    return jnp.asarray(logw)


def _att_sample_tasks(qkv_ref, ck_ref, cv_ref, logw_ref, o_ref, ko_ref, vo_ref, *, w, t_dec, n_heads):
    lane = lax.broadcasted_iota(jnp.int32, (HEAD_DIM, LANES), 1)
    keep = lane < LANES - t_dec
    d_att = n_heads * HEAD_DIM
    zpad = jnp.zeros((LANES - t_dec, d_att), F32)
    q = qkv_ref[:, 0:d_att]
    knt = jnp.concatenate([qkv_ref[:, d_att:2 * d_att], zpad], axis=0).T
    vnt = jnp.concatenate([qkv_ref[:, 2 * d_att:3 * d_att], zpad], axis=0).T
    logw_c = logw_ref[:, 0:w]
    logw_n = logw_ref[:, w:w + LANES]
    outs = [None] * n_heads

    def head(h):
        hs = slice(h * HEAD_DIM, (h + 1) * HEAD_DIM)
        kt = ck_ref[0, h]
        vt = cv_ref[0, h]
        knh = knt[hs, :]
        vnh = vnt[hs, :]
        qh = q[:, hs].astype(BF16)
        s_c = jnp.dot(qh, kt.astype(BF16), preferred_element_type=F32) + logw_c
        s_n = jnp.dot(qh, knh.astype(BF16), preferred_element_type=F32) + logw_n
        m = jnp.maximum(jnp.max(s_c, axis=-1, keepdims=True), jnp.max(s_n, axis=-1, keepdims=True))
        p_c = jnp.exp(s_c - m)
        p_n = jnp.exp(s_n - m)
        l = jnp.sum(p_c, axis=-1, keepdims=True) + jnp.sum(p_n, axis=-1, keepdims=True)
        o = _nt(p_c.astype(BF16), vt.astype(BF16)) + _nt(p_n.astype(BF16), vnh.astype(BF16))
        outs[h] = o / l
        for src, new, dst in ((kt, knh, ko_ref), (vt, vnh, vo_ref)):
            sh = pltpu.roll(src, w - t_dec, axis=1)
            tail = jnp.where(keep, sh[:, w - LANES:], pltpu.roll(new, LANES - t_dec, axis=1))
            dst[0, h, :, 0:w - LANES] = sh[:, 0:w - LANES]
            dst[0, h, :, w - LANES:w] = tail

    def finish():
        o_ref[...] = jnp.concatenate(outs, axis=1)

    return [functools.partial(head, h) for h in range(n_heads)] + [finish]


def _stream_kernel(sqkv_ref, ck_ref, cv_ref, logw_ref,
                   x_ref, shsc_ref, gpre_ref, w_ref, wa2_ref, ba2_ref, tab_ref,
                   cmat_ref, lvl_ref, rmask_ref,
                   o_ref, ko_ref, vo_ref, pqkv_ref, gates_ref, kt_ref, vt_ref, s_out_ref,
                   s_ref, *, w, t_dec, n_heads, n_lvl, n_tiles, d_att, d_gk, d_gv):
    n = lax.rem(pl.program_id(0), n_tiles)

    @pl.when(n == 0)
    def _():
        s_ref[...] = jnp.zeros_like(s_ref)

    win_steps = _att_sample_tasks(sqkv_ref, ck_ref, cv_ref, logw_ref, o_ref, ko_ref, vo_ref,
                                  w=w, t_dec=t_dec, n_heads=n_heads)
    proj_steps, res = _proj_tasks(x_ref[0], shsc_ref[0], gpre_ref, w_ref, wa2_ref, ba2_ref, tab_ref,
                                  d_att=d_att, d_gk=d_gk, d_gv=d_gv)

    def emit():
        for i, name in enumerate(("q", "k", "v")):
            pqkv_ref[0, :, i * d_att:(i + 1) * d_att] = res[name]
        gates_ref[0, :, 0:d_att] = res["ga"].astype(gates_ref.dtype)
        gates_ref[0, :, d_att:d_att + d_gv] = res["gg"].astype(gates_ref.dtype)
        kt_ref[0] = res["k"].T
        vt_ref[0] = res["v"].T

    tile_steps = proj_steps + [emit] + _gla_prompt_tasks(
        res, cmat_ref, lvl_ref, rmask_ref, gates_ref, d_att + d_gv, s_ref, n_lvl=n_lvl, d_gk=d_gk)
    for step in win_steps + tile_steps:
        step()

    @pl.when(n == n_tiles - 1)
    def _():
        s_out_ref[0] = s_ref[...]


def _stream_call(sqkv, cache_k, cache_v, t_dec, x, mod_p, gpre, w_pad, wa2_pad, ba2, table,
                 *, d_att, d_gk, d_gv):
    nseq, w, n_heads, hd = cache_k.shape
    b, s, d = x.shape
    t = LANES
    n_tiles = s // t
    n_pair = d_gk // LANES
    first_win_tile = (s - w) // t
    assert nseq == b * n_tiles, "one 128-row prompt tile is paired with each sample sequence"
    ck = jnp.transpose(cache_k, (0, 2, 3, 1))
    cv = jnp.transpose(cache_v, (0, 2, 3, 1))
    logw = _sample_att_table(w, t_dec)
    cmat, lvl, rmask, n_lvl = _gla_consts(LANES)
    tok = lambda wd: pl.BlockSpec((t_dec, wd), lambda i: (i, 0))
    win = pl.BlockSpec((1, n_heads, hd, w), lambda i: (i, 0, 0, 0))
    once = lambda a: pl.BlockSpec(a.shape, lambda i: (0,) * a.ndim, pipeline_mode=pl.Buffered(1))
    ptile = lambda wd: pl.BlockSpec((1, t, wd), lambda i: (i // n_tiles, i % n_tiles, 0))
    pwin = pl.BlockSpec((1, d_att, t),
                        lambda i: (i // n_tiles, 0, jnp.maximum(i % n_tiles - first_win_tile, 0)))
    kern = functools.partial(_stream_kernel, w=w, t_dec=t_dec, n_heads=n_heads, n_lvl=n_lvl,
                             n_tiles=n_tiles, d_att=d_att, d_gk=d_gk, d_gv=d_gv)
    f32 = lambda *shape: jax.ShapeDtypeStruct(shape, F32)
    bf16 = lambda *shape: jax.ShapeDtypeStruct(shape, BF16)
    outs = pl.pallas_call(
        kern,
        grid=(nseq,),
        in_specs=[tok(3 * d_att), win, win, once(logw),
                  ptile(d), pl.BlockSpec((1, 1, 2 * d), lambda i: (i // n_tiles, 0, 0)),
                  once(gpre), once(w_pad), once(wa2_pad), once(ba2),
                  pl.BlockSpec((t, 3 * LANES), lambda i: (i % n_tiles, 0)),
                  once(cmat), once(lvl), once(rmask)],
        out_specs=[tok(d_att), win, win, ptile(3 * d_att), ptile(d_att + 2 * d_gv), pwin, pwin,
                   pl.BlockSpec((1, n_pair, t, GLA_DV), lambda i: (i // n_tiles, 0, 0, 0))],
        out_shape=[f32(nseq * t_dec, d_att), f32(*ck.shape), f32(*cv.shape),
                   f32(b, s, 3 * d_att), bf16(b, s, d_att + 2 * d_gv),
                   f32(b, d_att, w), f32(b, d_att, w), f32(b, n_pair, t, GLA_DV)],
        scratch_shapes=[pltpu.VMEM((n_pair, t, GLA_DV), F32)],
        compiler_params=pltpu.CompilerParams(dimension_semantics=("arbitrary",),
                                             vmem_limit_bytes=STREAM_VMEM_LIMIT),
        name="stream",
    )(sqkv, ck, cv, logw, x, mod_p, gpre, w_pad, wa2_pad, ba2, table, cmat, lvl, rmask)
    o, ko, vo, pqkv, gates, kt, vt, s_out = outs
    to_win = lambda a: jnp.transpose(a, (0, 3, 1, 2))
    return (o, to_win(ko), to_win(vo), pqkv, gates,
            to_win(kt.reshape(b, n_heads, hd, w)), to_win(vt.reshape(b, n_heads, hd, w)),
            s_out.reshape(b, 2 * n_pair, GLA_DK, GLA_DV))


def _finish_kernel(x_ref, gate_ref, att_ref, ga_ref, og_ref, gg_ref, ggla_ref, gpost_ref, wo_ref,
                   y_ref, *, d_att, d_gv):
    att = att_ref[0] * ga_ref[0]
    og = og_ref[0].astype(F32)
    gg = gg_ref[0].astype(F32)
    y = jnp.dot(att.astype(BF16), wo_ref[0:d_att, :], preferred_element_type=F32)
    for h in range(d_gv // GLA_DV):
        sl = slice(h * GLA_DV, (h + 1) * GLA_DV)
        oh = og[:, sl]
        nh = oh * lax.rsqrt(jnp.mean(oh * oh, axis=-1, keepdims=True) + EPS) * ggla_ref[...]
        gl = (nh * gg[:, sl]).astype(BF16)
        y = y + jnp.dot(gl, wo_ref[d_att + h * GLA_DV:d_att + (h + 1) * GLA_DV, :],
                        preferred_element_type=F32)
    yn = y * lax.rsqrt(jnp.mean(y * y, axis=-1, keepdims=True) + EPS) * gpost_ref[...]
    y_ref[0] = x_ref[0] + gate_ref[0] * yn


def _finish_call(x3, mod3, att, ga, og, gg, g_gla, g_post, w_out, *, tm, d_att, d_gv):
    g, r, d = x3.shape
    per_row = mod3.shape[1] != 1
    if per_row:
        gate_spec = pl.BlockSpec((1, tm, d), lambda i, j: (i, j, 2))
    else:
        gate_spec = pl.BlockSpec((1, 1, d), lambda i, j: (i, 0, 2))
    tok = lambda w, c=0: pl.BlockSpec((1, tm, w), lambda i, j: (i, j, c))
    full = lambda a: pl.BlockSpec(a.shape, lambda i, j: (0,) * a.ndim)
    kern = functools.partial(_finish_kernel, d_att=d_att, d_gv=d_gv)
    return pl.pallas_call(
        kern,
        grid=(g, r // tm),
        in_specs=[tok(d), gate_spec, tok(d_att), tok(d_att, ga[1]), tok(d_gv, og[1]), tok(d_gv, gg[1]),
                  full(g_gla), full(g_post), full(w_out)],
        out_specs=tok(d),
        out_shape=jax.ShapeDtypeStruct((g, r, d), F32),
        compiler_params=_cparams("parallel", "arbitrary"),
        name="finish",
    )(x3, mod3, att, ga[0], og[0], gg[0], g_gla, g_post, w_out)


def kernel(x_prompt, x_sample, c_prompt, c_sample, cache_k_win, cache_v_win, state_gla,
           w_mod, b_mod, g_pre, g_post, w_in, w_a2, b_a2, g_gla, w_out):
    b, s, d = x_prompt.shape
    nseq, t_dec, _ = x_sample.shape
    w = cache_k_win.shape[1]
    d_att = cache_k_win.shape[2] * cache_k_win.shape[3]
    d_gk = w_a2.shape[1]
    d_gv = state_gla.shape[1] * state_gla.shape[3]
    d_main = 4 * d_att + 2 * d_gk + 2 * d_gv
    assert w_in.shape[1] == d_main + GLA_GATE_RANK and cache_k_win.shape[3] == HEAD_DIM
    assert state_gla.shape[2] == GLA_DK and state_gla.shape[3] == GLA_DV
    assert w >= STEPS * max(DILATIONS) and s % (ATT_BLOCK * max(DILATIONS)) == 0 and s >= w

    w_pad = jnp.pad(w_in, ((0, 0), (0, LANES - GLA_GATE_RANK))).astype(BF16)
    wa2_pad = jnp.pad(w_a2, ((0, LANES - GLA_GATE_RANK), (0, 0)))
    wo = w_out.astype(BF16)
    gpre = g_pre.reshape(1, d)
    gpost = g_post.reshape(1, d)
    ggla = g_gla.reshape(1, GLA_DV)
    ba2 = b_a2.reshape(1, d_gk)

    n_c = b + nseq
    n_c_pad = -(-n_c // SUBLANES) * SUBLANES
    c_all = jnp.pad(jnp.concatenate([c_prompt, c_sample], axis=0), ((0, n_c_pad - n_c), (0, 0)))
    mod = _mod_call(c_all, w_mod, b_mod)
    mod_p = mod[:b].reshape(b, 1, 3 * d)

    dims = dict(d_att=d_att, d_gk=d_gk, d_gv=d_gv)

    rows = nseq * t_dec
    tms = min(512, rows)
    xs = x_sample.reshape(rows // tms, tms, d)
    mod_s = jnp.repeat(mod[b:b + nseq], t_dec, axis=0).reshape(rows // tms, tms, 3 * d)
    tab_s = _rope_tables(jnp.tile(PAST_LEN + jnp.arange(t_dec), tms // t_dec))
    outs = _proj_call(xs, mod_s, gpre, w_pad, wa2_pad, ba2, tab_s, tm=tms, **dims)
    qkv_s, ga_s, qg_s, kg_s, la_s, vg_s, gg_s = [o.reshape(rows, o.shape[2]) for o in outs]

    tab_p = _rope_tables(jnp.arange(s))
    (o_att_s, k_win_sample, v_win_sample, qkv, gates, k_win_prompt, v_win_prompt,
     gla_p) = _stream_call(qkv_s, cache_k_win, cache_v_win, t_dec,
                           x_prompt, mod_p, gpre, w_pad, wa2_pad, ba2, tab_p, **dims)

    att = _att_prompt_call(qkv)
    og_s, gla_s = _gla_sample_call(qg_s, kg_s, la_s, vg_s, state_gla, t_dec)

    assert d_att == d_gv
    y_prompt = _finish_call(x_prompt, mod_p, att, (gates, 0), (gates, 2), (gates, 1),
                            ggla, gpost, wo, tm=512, d_att=d_att, d_gv=d_gv)
    r3 = lambda a: a.reshape(rows // tms, tms, a.shape[1])
    y_sample = _finish_call(xs, mod_s, r3(o_att_s), (r3(ga_s), 0), (r3(og_s), 0), (r3(gg_s), 0),
                            ggla, gpost, wo, tm=tms, d_att=d_att, d_gv=d_gv).reshape(x_sample.shape)

    return (y_prompt, y_sample, k_win_prompt, v_win_prompt, gla_p,
            k_win_sample, v_win_sample, gla_s)
```

```python
import functools
import math

import numpy as np
import jax
import jax.numpy as jnp
from jax import lax
from jax.experimental import pallas as pl
from jax.experimental.pallas import tpu as pltpu

F32 = jnp.float32
BF16 = jnp.bfloat16

HEAD_DIM = 64
ROT_DIM = HEAD_DIM // 4
ROPE_THETA = 500000.0
DILATIONS = (1, 4, 16)
STEPS = 128
ATT_BLOCK = 128
UNITS_PER_ITER = 32
GATHERS_PER_ITER = 4
STAGED_DILATION = 4
PAST_LEN = 8192
GLA_DK = 64
GLA_DV = 128
GLA_GATE_RANK = 16
GLA_TAU = 16.0
EPS = 1e-6
NEG = -1e30
LOG2E = math.log2(math.e)

LANES = 128
SUBLANES = 8
VMEM_LIMIT = 48 * 1024 * 1024
STREAM_VMEM_LIMIT = 56 * 1024 * 1024


def _cparams(*sem):
    return pltpu.CompilerParams(dimension_semantics=sem, vmem_limit_bytes=VMEM_LIMIT)


def _nt(a, b):
    return lax.dot_general(a, b, (((1,), (1,)), ((), ())), preferred_element_type=F32)


def _split3(x):
    hi = x.astype(BF16)
    r1 = x - hi.astype(F32)
    mid = r1.astype(BF16)
    lo = (r1 - mid.astype(F32)).astype(BF16)
    return hi, mid, lo


def _mod_kernel(c_ref, w_ref, b_ref, o_ref):
    c = c_ref[...]
    a = c / (1.0 + jnp.exp(-c))
    a_hi, a_mid, _ = _split3(a)
    w_hi, w_mid, _ = _split3(w_ref[...])
    dot = functools.partial(jnp.dot, preferred_element_type=F32)
    o_ref[...] = dot(a_hi, w_hi) + (dot(a_hi, w_mid) + dot(a_mid, w_hi)) + b_ref[...]


def _mod_call(c, w_mod, b_mod):
    n, d = c.shape
    d3 = w_mod.shape[1]
    tn = 512
    return pl.pallas_call(
        _mod_kernel,
        grid=(d3 // tn,),
        in_specs=[pl.BlockSpec((n, d), lambda j: (0, 0)),
                  pl.BlockSpec((d, tn), lambda j: (0, j)),
                  pl.BlockSpec((1, tn), lambda j: (0, j))],
        out_specs=pl.BlockSpec((n, tn), lambda j: (0, j)),
        out_shape=jax.ShapeDtypeStruct((n, d3), F32),
        compiler_params=_cparams("arbitrary"),
        name="mod",
    )(c, w_mod, b_mod.reshape(1, d3))


def _rope_tables(pos):
    f32 = np.float32
    half = ROT_DIM // 2
    inv = (f32(1.0) / (f32(ROPE_THETA) ** (np.arange(half, dtype=f32) / f32(half)))).astype(f32)
    ang = np.asarray(pos, f32)[:, None] * inv[None, :]
    cos, sin = np.cos(ang).astype(f32), np.sin(ang).astype(f32)
    n = ang.shape[0]
    ones = np.ones((n, HEAD_DIM - ROT_DIM), f32)
    zeros_h = np.zeros((n, half), f32)
    zeros_r = np.zeros((n, HEAD_DIM - ROT_DIM), f32)
    c = np.concatenate([cos, cos, ones], axis=1)
    s1 = np.concatenate([-sin, zeros_h, zeros_r], axis=1)
    s2 = np.concatenate([zeros_h, sin, zeros_r], axis=1)
    rep = LANES // HEAD_DIM
    return jnp.asarray(np.concatenate([np.tile(t, (1, rep)) for t in (c, s1, s2)], axis=1))


def _proj_tasks(x, shsc, gpre_ref, w_ref, wa2_ref, ba2_ref, tab_ref, *, d_att, d_gk, d_gv):
    d = x.shape[1]
    ms = jnp.mean(x * x, axis=-1, keepdims=True)
    h = x * lax.rsqrt(ms + EPS) * gpre_ref[...]
    h = h * (1.0 + shsc[:, d:]) + shsc[:, :d]
    hb = h.astype(BF16)
    reps = d_att // LANES
    half = ROT_DIM // 2
    res = {}

    def seg(a, n):
        return jnp.dot(hb, w_ref[:, a:a + n], preferred_element_type=F32)

    def rot(t):
        rc = jnp.concatenate([tab_ref[:, 0:LANES]] * reps, axis=1)
        rs1 = jnp.concatenate([tab_ref[:, LANES:2 * LANES]] * reps, axis=1)
        rs2 = jnp.concatenate([tab_ref[:, 2 * LANES:3 * LANES]] * reps, axis=1)
        up = pltpu.roll(t, d_att - half, axis=1)
        dn = pltpu.roll(t, half, axis=1)
        return t * rc + up * rs1 + dn * rs2

    def silu(t):
        return t / (1.0 + jnp.exp(-t))

    def log_decay(t):
        a1, a2, a3 = _split3(t)
        w1, w2, w3 = _split3(wa2_ref[...])
        dot = functools.partial(jnp.dot, preferred_element_type=F32)
        z = (dot(a1, w1) + dot(a1, w2) + dot(a2, w1)
             + dot(a1, w3) + dot(a2, w2) + dot(a3, w1)) + ba2_ref[...]
        logsig = jnp.minimum(z, 0.0) - jnp.log1p(jnp.exp(-jnp.abs(z)))
        return logsig * (1.0 / GLA_TAU)

    plan = (("q", d_att, lambda t: rot(t) * (HEAD_DIM ** -0.5)), ("k", d_att, rot),
            ("v", d_att, None), ("ga", d_att, silu),
            ("qg", d_gk, lambda t: t * (GLA_DK ** -0.5)), ("kg", d_gk, None),
            ("vg", d_gv, None), ("gg", d_gv, silu), ("la", LANES, log_decay))
    steps, off = [], 0
    for name, width, post in plan:
        def step(name=name, width=width, post=post, off=off):
            t = seg(off, width)
            res[name] = t if post is None else post(t)
        steps.append(step)
        off += width
    return steps, res


def _proj_kernel(x_ref, shsc_ref, gpre_ref, w_ref, wa2_ref, ba2_ref, tab_ref,
                 qkv_ref, *out_refs, d_att, d_gk, d_gv):
    steps, res = _proj_tasks(x_ref[0], shsc_ref[0], gpre_ref, w_ref, wa2_ref, ba2_ref, tab_ref,
                             d_att=d_att, d_gk=d_gk, d_gv=d_gv)
    for step in steps:
        step()
    for i, name in enumerate(("q", "k", "v")):
        qkv_ref[0, :, i * d_att:(i + 1) * d_att] = res[name]
    for ref, name in zip(out_refs, ("ga", "qg", "kg", "la", "vg", "gg")):
        ref[0] = res[name]


def _proj_call(x3, mod3, gpre, w_pad, wa2_pad, ba2, table, *, d_att, d_gk, d_gv, tm):
    g, r, d = x3.shape
    grid = (g, r // tm)
    full = lambda a: pl.BlockSpec(a.shape, lambda i, j: (0,) * a.ndim)
    widths = (3 * d_att, d_att, d_gk, d_gk, d_gk, d_gv, d_gv)
    kern = functools.partial(_proj_kernel, d_att=d_att, d_gk=d_gk, d_gv=d_gv)
    return pl.pallas_call(
        kern,
        grid=grid,
        in_specs=[pl.BlockSpec((1, tm, d), lambda i, j: (i, j, 0)),
                  pl.BlockSpec((1, tm, 2 * d), lambda i, j: (i, j, 0)),
                  full(gpre), full(w_pad), full(wa2_pad), full(ba2),
                  pl.BlockSpec((tm, 3 * LANES), lambda i, j: (0, 0))],
        out_specs=[pl.BlockSpec((1, tm, w), lambda i, j: (i, j, 0)) for w in widths],
        out_shape=[jax.ShapeDtypeStruct((g, r, w), F32) for w in widths],
        compiler_params=_cparams("parallel", "arbitrary"),
        name="proj",
    )(x3, mod3, gpre, w_pad, wa2_pad, ba2, table)


def _band_bias():
    row = np.arange(ATT_BLOCK)[:, None]
    col = np.arange(2 * ATT_BLOCK)[None, :]
    dist = row + ATT_BLOCK - col
    band = (dist >= 0) & (dist <= STEPS)
    first = band & (col >= ATT_BLOCK)
    return jnp.asarray(np.where(np.stack([first, band]), 0.0, NEG), F32)


def _att_buf_rows(seq):
    offs, total = {}, 0
    for dil in DILATIONS:
        offs[dil] = total
        total += seq + dil * ATT_BLOCK
    return offs, total


def _att_prompt_kernel(q_ref, k_ref, v_ref, bias_ref, o_ref,
                       qb_ref, kb_ref, vb_ref, acc_ref, m_ref, l_ref, *stage_refs, seq):
    blk = ATT_BLOCK
    n_units = seq // blk
    offs, _ = _att_buf_rows(seq)
    lane = lax.broadcasted_iota(jnp.int32, (blk, LANES), 1)
    lo = lane < HEAD_DIM
    head_masks = (jnp.where(lo, 1.0, 0.0).astype(BF16), jnp.where(lo, 0.0, 1.0).astype(BF16))
    ones = jnp.ones((2 * blk, LANES), BF16)
    zeros = jnp.zeros((blk, LANES), BF16)

    def place(u, dil):
        r = lax.rem(u, dil)
        n = lax.div(u, dil)
        first = r + dil * blk * n
        base = offs[dil] + r * (seq // dil + blk) + blk * n
        return n, first, pl.multiple_of(base, blk)

    def rows(first, dil):
        return pl.ds(first, blk) if dil == 1 else pl.ds(first, blk, stride=dil)

    srcs = (q_ref, k_ref, v_ref)
    bufs = (qb_ref, kb_ref, vb_ref)
    scales = (LOG2E, None, None)
    parent = {d: max([p for p in DILATIONS if 1 < p < d and d % p == 0], default=None) for d in DILATIONS}
    assert set(parent.values()) <= {None, STAGED_DILATION}
    for dil in DILATIONS:
        for r in range(dil):
            z0 = offs[dil] + r * (seq // dil + blk)
            kb_ref[z0:z0 + blk, :] = zeros
            vb_ref[z0:z0 + blk, :] = zeros

        def gather(u, carry, dil=dil):
            n, first, base = place(u, dil)
            r = lax.rem(u, dil)
            par = parent[dil]
            for a in range(3):
                if par is None:
                    x = srcs[a][0, rows(first, dil), :]
                else:
                    step = dil // par
                    start = lax.rem(r, par) * (seq // par) + lax.div(r, par) + step * blk * n
                    x = stage_refs[a][pl.ds(start, blk, stride=step), :]
                if dil == STAGED_DILATION:
                    stage_refs[a][pl.ds(pl.multiple_of(r * (seq // dil) + blk * n, blk), blk), :] = x
                if scales[a] is not None:
                    x = x * scales[a]
                bufs[a][pl.ds(base + blk, blk), :] = x.astype(BF16)
            return carry

        lax.fori_loop(0, n_units, gather, 0, unroll=GATHERS_PER_ITER)

    order = tuple(reversed(DILATIONS))
    for dil in order:
        def unit(u, carry, dil=dil):
            n, first, base = place(u, dil)
            qb = qb_ref[pl.ds(base + blk, blk), :]
            kcat = kb_ref[pl.ds(base, 2 * blk), :]
            vext = jnp.concatenate([vb_ref[pl.ds(base, 2 * blk), :], ones], axis=1)
            bias = bias_ref[jnp.minimum(n, 1)]
            os, ms, ls = [], [], []
            for hm in head_masks:
                s = _nt(qb * hm, kcat) + bias
                m = jnp.max(s, axis=-1, keepdims=True)
                p = jnp.exp2(s - m)
                res = jnp.dot(p.astype(BF16), vext, preferred_element_type=F32)
                os.append(res[:, :LANES])
                ls.append(res[:, LANES:])
                ms.append(m)
            o_new = jnp.where(lo, os[0], os[1])
            m_new = jnp.where(lo, ms[0], ms[1])
            l_new = jnp.where(lo, ls[0], ls[1])
            dst = rows(first, dil)
            if dil == order[0]:
                acc_ref[dst, :] = o_new
                m_ref[dst, :] = m_new
                l_ref[dst, :] = l_new
                return carry
            m_old = m_ref[dst, :]
            m_tot = jnp.maximum(m_old, m_new)
            a_old = jnp.exp2(m_old - m_tot)
            a_new = jnp.exp2(m_new - m_tot)
            acc = a_old * acc_ref[dst, :] + a_new * o_new
            den = a_old * l_ref[dst, :] + a_new * l_new
            if dil == order[-1]:
                o_ref[0, pl.ds(pl.multiple_of(first, blk), blk), :] = (acc / den).astype(o_ref.dtype)
            else:
                acc_ref[dst, :] = acc
                l_ref[dst, :] = den
                m_ref[dst, :] = m_tot
            return carry

        lax.fori_loop(0, n_units, unit, 0, unroll=UNITS_PER_ITER)


def _att_prompt_call(qkv):
    b, s, d3 = qkv.shape
    d_att = d3 // 3
    nblk = d_att // LANES
    bias = _band_bias()
    _, buf_rows = _att_buf_rows(s)
    slab = pl.BlockSpec((1, s, LANES), lambda i, j: (i, 0, j))
    part = lambda p: pl.BlockSpec((1, s, LANES), lambda i, j: (i, 0, p * nblk + j))
    kern = functools.partial(_att_prompt_kernel, seq=s)
    return pl.pallas_call(
        kern,
        grid=(b, d_att // LANES),
        in_specs=[part(0), part(1), part(2), pl.BlockSpec(bias.shape, lambda i, j: (0, 0, 0))],
        out_specs=slab,
        out_shape=jax.ShapeDtypeStruct((b, s, d_att), BF16),
        scratch_shapes=[pltpu.VMEM((buf_rows, LANES), BF16)] * 3 + [pltpu.VMEM((s, LANES), F32)] * 6,
        compiler_params=_cparams("parallel", "parallel"),
        name="att_prompt",
    )(qkv, qkv, qkv, bias)


def _gla_consts(blk):
    t = np.arange(LANES)
    same = (t[:, None] // blk) == (t[None, :] // blk)
    tri = ((t[None, :] <= t[:, None]) & same).astype(np.int64)
    mats, lvl, right_rows = [tri], [], []
    h = blk // 2
    while h >= 1:
        pair = t // (2 * h)
        right = (t // h) % 2 == 1
        mid = pair * 2 * h + h - 1
        upto_mid = ((t[None, :] <= mid[:, None]) & same).astype(np.int64)
        mats.append(tri - upto_mid)
        lvl.append(pair[:, None] == pair[None, :])
        right_rows.append(np.broadcast_to(right[:, None], (LANES, LANES)))
        h //= 2
    lvl.append(np.eye(LANES, dtype=bool))
    if blk < LANES:
        mats.append(same.astype(np.int64))
    cmat = jnp.asarray(np.concatenate(mats, 0), BF16)
    lvl = jnp.asarray(np.concatenate(lvl, 0), F32)
    rmask = jnp.asarray(np.concatenate(right_rows, 0), F32)
    return cmat, lvl, rmask, len(right_rows)


def _gla_tile_tasks(q, k, la, cmat_ref, lvl_ref, rmask_ref, n_lvl):
    t = LANES
    n_pair = q.shape[1] // LANES
    st = {}
    pairs = [None] * n_pair

    def cumulate():
        hi, mid, lo3 = _split3(la)
        cm = cmat_ref[...]
        dot = functools.partial(jnp.dot, preferred_element_type=F32)
        g = dot(cm, hi) + dot(cm, mid) + dot(cm, lo3)
        b = g[0:t]
        if cm.shape[0] > (n_lvl + 1) * t:
            btot = g[(n_lvl + 1) * t:(n_lvl + 2) * t]
        else:
            btot = jnp.broadcast_to(b[t - 1:t, :], b.shape)
        st.update(g=g, b=b, btot=btot)

    def pair(j):
        g, b, btot = st["g"], st["b"], st["btot"]
        lane = lax.broadcasted_iota(jnp.int32, (t, LANES), 1)
        lo = lane < GLA_DK
        sl = slice(j * LANES, (j + 1) * LANES)
        qp, kp, bp = q[:, sl], k[:, sl], b[:, sl]
        a0 = jnp.zeros((t, t), F32)
        a1 = jnp.zeros((t, t), F32)
        for lv in range(n_lvl + 1):
            m = lvl_ref[lv * t:(lv + 1) * t, :]
            if lv < n_lvl:
                dl = g[(lv + 1) * t:(lv + 2) * t, sl]
                e = jnp.exp(-jnp.abs(dl))
                rm = rmask_ref[lv * t:(lv + 1) * t, :]
                x = qp * e * rm
                y = (kp * e * (1.0 - rm)).astype(BF16)
            else:
                x = qp
                y = kp.astype(BF16)
            x0 = jnp.where(lo, x, 0.0).astype(BF16)
            x1 = jnp.where(lo, 0.0, x).astype(BF16)
            a0 = a0 + m * _nt(x0, y)
            a1 = a1 + m * _nt(x1, y)
        q_in = qp * jnp.exp(bp)
        k_in = kp * jnp.exp(btot[:, sl] - bp)
        pairs[j] = (a0, a1, q_in, k_in, btot[:, sl], lo)

    return [cumulate] + [functools.partial(pair, j) for j in range(n_pair)], pairs


def _gla_prompt_tasks(res, cmat_ref, lvl_ref, rmask_ref, o_ref, o_col, s_ref, *, n_lvl, d_gk):
    t = LANES
    n_pair = d_gk // LANES
    holder = {}

    def start():
        steps, pairs = _gla_tile_tasks(res["qg"], res["kg"], res["la"],
                                       cmat_ref, lvl_ref, rmask_ref, n_lvl)
        holder.update(steps=steps, pairs=pairs)
        steps[0]()

    def levels(j):
        holder["steps"][1 + j]()

    def finish(j):
        a0, a1, q_in, k_in, btot, lo = holder["pairs"][j]
        v = res["vg"]
        row = lax.broadcasted_iota(jnp.int32, (t, t), 0)
        top = row < GLA_DK
        s_prev = s_ref[j]
        sb = s_prev.astype(BF16)
        kt = k_in.T.astype(BF16)
        us = []
        for hh, (a, sel) in enumerate(((a0, lo), (a1, ~lo))):
            vs = slice((2 * j + hh) * GLA_DV, (2 * j + hh + 1) * GLA_DV)
            vb = v[:, vs].astype(BF16)
            qm = jnp.where(sel, q_in, 0.0).astype(BF16)
            o = (jnp.dot(a.astype(BF16), vb, preferred_element_type=F32)
                 + jnp.dot(qm, sb, preferred_element_type=F32))
            o_ref[0, :, o_col + vs.start:o_col + vs.stop] = o.astype(o_ref.dtype)
            us.append(jnp.dot(kt, vb, preferred_element_type=F32))
        a_col = jnp.exp(btot.T)
        s_ref[j] = a_col * s_prev + jnp.where(top, us[0], us[1])

    tasks = [start]
    for j in range(n_pair):
        tasks += [functools.partial(levels, j), functools.partial(finish, j)]
    return tasks


def _gla_sample_kernel(q_ref, k_ref, la_ref, v_ref, s_in_ref, cmat_ref, lvl_ref, rmask_ref,
                       o_ref, s_out_ref, *, n_lvl, t_dec):
    steps, pairs = _gla_tile_tasks(q_ref[...], k_ref[...], la_ref[...],
                                   cmat_ref, lvl_ref, rmask_ref, n_lvl)
    for step in steps:
        step()
    t = LANES
    nb = t // t_dec
    row = lax.broadcasted_iota(jnp.int32, (t, t), 0)
    col = lax.broadcasted_iota(jnp.int32, (t, t), 1)
    top = row < GLA_DK
    for j, (a0, a1, q_in, k_in, btot, lo) in enumerate(pairs):
        s_prev = s_in_ref[:, j]
        sb = s_prev.astype(BF16)
        kt = k_in.T
        bt = btot.T
        vbs = []
        for hh, (a, sel) in enumerate(((a0, lo), (a1, ~lo))):
            vs = slice((2 * j + hh) * GLA_DV, (2 * j + hh + 1) * GLA_DV)
            vb = v_ref[:, vs].astype(BF16)
            vbs.append(vb)
            qm = jnp.where(sel, q_in, 0.0)
            o_inter = jnp.concatenate(
                [jnp.dot(qm[i * t_dec:(i + 1) * t_dec].astype(BF16), sb[i],
                         preferred_element_type=F32) for i in range(nb)], axis=0)
            o_ref[:, vs] = jnp.dot(a.astype(BF16), vb, preferred_element_type=F32) + o_inter
        for i in range(nb):
            mine = (col >= i * t_dec) & (col < (i + 1) * t_dec)
            kti = (kt * mine.astype(F32)).astype(BF16)
            u0 = jnp.dot(kti, vbs[0], preferred_element_type=F32)
            u1 = jnp.dot(kti, vbs[1], preferred_element_type=F32)
            a_col = jnp.exp(jnp.broadcast_to(bt[:, i * t_dec:i * t_dec + 1], (t, GLA_DV)))
            s_out_ref[i, j] = a_col * s_prev[i] + jnp.where(top, u0, u1)


def _gla_sample_call(qg, kg, la, vg, state, t_dec):
    rows, d_gk = qg.shape
    d_gv = vg.shape[1]
    n_pair = d_gk // LANES
    nseq = rows // t_dec
    t = LANES
    nb = t // t_dec
    cmat, lvl, rmask, n_lvl = _gla_consts(t_dec)
    s_in = state.reshape(nseq, n_pair, 2 * GLA_DK, GLA_DV)
    tok = lambda w: pl.BlockSpec((t, w), lambda i: (i, 0))
    full = lambda a: pl.BlockSpec(a.shape, lambda i: (0,) * a.ndim)
    st = pl.BlockSpec((nb, n_pair, t, GLA_DV), lambda i: (i, 0, 0, 0))
    kern = functools.partial(_gla_sample_kernel, n_lvl=n_lvl, t_dec=t_dec)
    o, s_out = pl.pallas_call(
        kern,
        grid=(rows // t,),
        in_specs=[tok(d_gk), tok(d_gk), tok(d_gk), tok(d_gv), st, full(cmat), full(lvl), full(rmask)],
        out_specs=[tok(d_gv), st],
        out_shape=[jax.ShapeDtypeStruct((rows, d_gv), F32),
                   jax.ShapeDtypeStruct(s_in.shape, F32)],
        compiler_params=_cparams("parallel"),
        name="gla_sample",
    )(qg, kg, la, vg, s_in, cmat, lvl, rmask)
    return o, s_out.reshape(state.shape)


def _sample_att_table(w, t_dec):
    idx = np.arange(w + LANES)[None, :]
    tq = np.arange(t_dec)[:, None]
    dist = w + tq - idx
    mult = np.zeros((t_dec, w + LANES), np.int64)
    for d in DILATIONS:
        mult += (dist >= 0) & (dist % d == 0) & (dist // d <= STEPS) & (idx < w + t_dec)
    logw = np.where(mult > 0, np.log(np.maximum(mult, 1)), NEG).astype(np.float32)
    return jnp.asarray(logw)


def _att_sample_tasks(qkv_ref, ck_ref, cv_ref, logw_ref, o_ref, ko_ref, vo_ref, *, w, t_dec, n_heads):
    lane = lax.broadcasted_iota(jnp.int32, (HEAD_DIM, LANES), 1)
    keep = lane < LANES - t_dec
    d_att = n_heads * HEAD_DIM
    zpad = jnp.zeros((LANES - t_dec, d_att), F32)
    q = qkv_ref[:, 0:d_att]
    knt = jnp.concatenate([qkv_ref[:, d_att:2 * d_att], zpad], axis=0).T
    vnt = jnp.concatenate([qkv_ref[:, 2 * d_att:3 * d_att], zpad], axis=0).T
    logw_c = logw_ref[:, 0:w]
    logw_n = logw_ref[:, w:w + LANES]
    outs = [None] * n_heads

    def head(h):
        hs = slice(h * HEAD_DIM, (h + 1) * HEAD_DIM)
        kt = ck_ref[0, h]
        vt = cv_ref[0, h]
        knh = knt[hs, :]
        vnh = vnt[hs, :]
        qh = q[:, hs].astype(BF16)
        s_c = jnp.dot(qh, kt.astype(BF16), preferred_element_type=F32) + logw_c
        s_n = jnp.dot(qh, knh.astype(BF16), preferred_element_type=F32) + logw_n
        m = jnp.maximum(jnp.max(s_c, axis=-1, keepdims=True), jnp.max(s_n, axis=-1, keepdims=True))
        p_c = jnp.exp(s_c - m)
        p_n = jnp.exp(s_n - m)
        l = jnp.sum(p_c, axis=-1, keepdims=True) + jnp.sum(p_n, axis=-1, keepdims=True)
        o = _nt(p_c.astype(BF16), vt.astype(BF16)) + _nt(p_n.astype(BF16), vnh.astype(BF16))
        outs[h] = o / l
        for src, new, dst in ((kt, knh, ko_ref), (vt, vnh, vo_ref)):
            sh = pltpu.roll(src, w - t_dec, axis=1)
            tail = jnp.where(keep, sh[:, w - LANES:], pltpu.roll(new, LANES - t_dec, axis=1))
            dst[0, h, :, 0:w - LANES] = sh[:, 0:w - LANES]
            dst[0, h, :, w - LANES:w] = tail

    def finish():
        o_ref[...] = jnp.concatenate(outs, axis=1)

    return [functools.partial(head, h) for h in range(n_heads)] + [finish]


def _stream_kernel(sqkv_ref, ck_ref, cv_ref, logw_ref,
                   x_ref, shsc_ref, gpre_ref, w_ref, wa2_ref, ba2_ref, tab_ref,
                   cmat_ref, lvl_ref, rmask_ref,
                   o_ref, ko_ref, vo_ref, pqkv_ref, gates_ref, kt_ref, vt_ref, s_out_ref,
                   s_ref, *, w, t_dec, n_heads, n_lvl, n_tiles, d_att, d_gk, d_gv):
    n = lax.rem(pl.program_id(0), n_tiles)

    @pl.when(n == 0)
    def _():
        s_ref[...] = jnp.zeros_like(s_ref)

    win_steps = _att_sample_tasks(sqkv_ref, ck_ref, cv_ref, logw_ref, o_ref, ko_ref, vo_ref,
                                  w=w, t_dec=t_dec, n_heads=n_heads)
    proj_steps, res = _proj_tasks(x_ref[0], shsc_ref[0], gpre_ref, w_ref, wa2_ref, ba2_ref, tab_ref,
                                  d_att=d_att, d_gk=d_gk, d_gv=d_gv)

    def emit():
        for i, name in enumerate(("q", "k", "v")):
            pqkv_ref[0, :, i * d_att:(i + 1) * d_att] = res[name]
        gates_ref[0, :, 0:d_att] = res["ga"].astype(gates_ref.dtype)
        gates_ref[0, :, d_att:d_att + d_gv] = res["gg"].astype(gates_ref.dtype)
        kt_ref[0] = res["k"].T
        vt_ref[0] = res["v"].T

    tile_steps = proj_steps + [emit] + _gla_prompt_tasks(
        res, cmat_ref, lvl_ref, rmask_ref, gates_ref, d_att + d_gv, s_ref, n_lvl=n_lvl, d_gk=d_gk)
    for step in win_steps + tile_steps:
        step()

    @pl.when(n == n_tiles - 1)
    def _():
        s_out_ref[0] = s_ref[...]


def _stream_call(sqkv, cache_k, cache_v, t_dec, x, mod_p, gpre, w_pad, wa2_pad, ba2, table,
                 *, d_att, d_gk, d_gv):
    nseq, w, n_heads, hd = cache_k.shape
    b, s, d = x.shape
    t = LANES
    n_tiles = s // t
    n_pair = d_gk // LANES
    first_win_tile = (s - w) // t
    assert nseq == b * n_tiles, "one 128-row prompt tile is paired with each sample sequence"
    ck = jnp.transpose(cache_k, (0, 2, 3, 1))
    cv = jnp.transpose(cache_v, (0, 2, 3, 1))
    logw = _sample_att_table(w, t_dec)
    cmat, lvl, rmask, n_lvl = _gla_consts(LANES)
    tok = lambda wd: pl.BlockSpec((t_dec, wd), lambda i: (i, 0))
    win = pl.BlockSpec((1, n_heads, hd, w), lambda i: (i, 0, 0, 0))
    once = lambda a: pl.BlockSpec(a.shape, lambda i: (0,) * a.ndim, pipeline_mode=pl.Buffered(1))
    ptile = lambda wd: pl.BlockSpec((1, t, wd), lambda i: (i // n_tiles, i % n_tiles, 0))
    pwin = pl.BlockSpec((1, d_att, t),
                        lambda i: (i // n_tiles, 0, jnp.maximum(i % n_tiles - first_win_tile, 0)))
    kern = functools.partial(_stream_kernel, w=w, t_dec=t_dec, n_heads=n_heads, n_lvl=n_lvl,
                             n_tiles=n_tiles, d_att=d_att, d_gk=d_gk, d_gv=d_gv)
    f32 = lambda *shape: jax.ShapeDtypeStruct(shape, F32)
    bf16 = lambda *shape: jax.ShapeDtypeStruct(shape, BF16)
    outs = pl.pallas_call(
        kern,
        grid=(nseq,),
        in_specs=[tok(3 * d_att), win, win, once(logw),
                  ptile(d), pl.BlockSpec((1, 1, 2 * d), lambda i: (i // n_tiles, 0, 0)),
                  once(gpre), once(w_pad), once(wa2_pad), once(ba2),
                  pl.BlockSpec((t, 3 * LANES), lambda i: (i % n_tiles, 0)),
                  once(cmat), once(lvl), once(rmask)],
        out_specs=[tok(d_att), win, win, ptile(3 * d_att), ptile(d_att + 2 * d_gv), pwin, pwin,
                   pl.BlockSpec((1, n_pair, t, GLA_DV), lambda i: (i // n_tiles, 0, 0, 0))],
        out_shape=[f32(nseq * t_dec, d_att), f32(*ck.shape), f32(*cv.shape),
                   f32(b, s, 3 * d_att), bf16(b, s, d_att + 2 * d_gv),
                   f32(b, d_att, w), f32(b, d_att, w), f32(b, n_pair, t, GLA_DV)],
        scratch_shapes=[pltpu.VMEM((n_pair, t, GLA_DV), F32)],
        compiler_params=pltpu.CompilerParams(dimension_semantics=("arbitrary",),
                                             vmem_limit_bytes=STREAM_VMEM_LIMIT),
        name="stream",
    )(sqkv, ck, cv, logw, x, mod_p, gpre, w_pad, wa2_pad, ba2, table, cmat, lvl, rmask)
    o, ko, vo, pqkv, gates, kt, vt, s_out = outs
    to_win = lambda a: jnp.transpose(a, (0, 3, 1, 2))
    return (o, to_win(ko), to_win(vo), pqkv, gates,
            to_win(kt.reshape(b, n_heads, hd, w)), to_win(vt.reshape(b, n_heads, hd, w)),
            s_out.reshape(b, 2 * n_pair, GLA_DK, GLA_DV))


def _finish_kernel(x_ref, gate_ref, att_ref, ga_ref, og_ref, gg_ref, ggla_ref, gpost_ref, wo_ref,
                   y_ref, *, d_att, d_gv):
    att = att_ref[0] * ga_ref[0]
    og = og_ref[0].astype(F32)
    gg = gg_ref[0].astype(F32)
    gl = []
    for h in range(d_gv // GLA_DV):
        sl = slice(h * GLA_DV, (h + 1) * GLA_DV)
        oh = og[:, sl]
        nh = oh * lax.rsqrt(jnp.mean(oh * oh, axis=-1, keepdims=True) + EPS) * ggla_ref[...]
        gl.append((nh * gg[:, sl]).astype(BF16))
    mix = jnp.concatenate([att.astype(BF16)] + gl, axis=1)
    y = jnp.dot(mix, wo_ref[...], preferred_element_type=F32)
    yn = y * lax.rsqrt(jnp.mean(y * y, axis=-1, keepdims=True) + EPS) * gpost_ref[...]
    y_ref[0] = x_ref[0] + gate_ref[0] * yn


def _finish_call(x3, mod3, att, ga, og, gg, g_gla, g_post, w_out, *, tm, d_att, d_gv):
    g, r, d = x3.shape
    per_row = mod3.shape[1] != 1
    if per_row:
        gate_spec = pl.BlockSpec((1, tm, d), lambda i, j: (i, j, 2))
    else:
        gate_spec = pl.BlockSpec((1, 1, d), lambda i, j: (i, 0, 2))
    tok = lambda w, c=0: pl.BlockSpec((1, tm, w), lambda i, j: (i, j, c))
    full = lambda a: pl.BlockSpec(a.shape, lambda i, j: (0,) * a.ndim)
    kern = functools.partial(_finish_kernel, d_att=d_att, d_gv=d_gv)
    return pl.pallas_call(
        kern,
        grid=(g, r // tm),
        in_specs=[tok(d), gate_spec, tok(d_att), tok(d_att, ga[1]), tok(d_gv, og[1]), tok(d_gv, gg[1]),
                  full(g_gla), full(g_post), full(w_out)],
        out_specs=tok(d),
        out_shape=jax.ShapeDtypeStruct((g, r, d), F32),
        compiler_params=_cparams("parallel", "arbitrary"),
        name="finish",
    )(x3, mod3, att, ga[0], og[0], gg[0], g_gla, g_post, w_out)


def kernel(x_prompt, x_sample, c_prompt, c_sample, cache_k_win, cache_v_win, state_gla,
           w_mod, b_mod, g_pre, g_post, w_in, w_a2, b_a2, g_gla, w_out):
    b, s, d = x_prompt.shape
    nseq, t_dec, _ = x_sample.shape
    w = cache_k_win.shape[1]
    d_att = cache_k_win.shape[2] * cache_k_win.shape[3]
    d_gk = w_a2.shape[1]
    d_gv = state_gla.shape[1] * state_gla.shape[3]
    d_main = 4 * d_att + 2 * d_gk + 2 * d_gv
    assert w_in.shape[1] == d_main + GLA_GATE_RANK and cache_k_win.shape[3] == HEAD_DIM
    assert state_gla.shape[2] == GLA_DK and state_gla.shape[3] == GLA_DV
    assert w >= STEPS * max(DILATIONS) and s % (ATT_BLOCK * max(DILATIONS)) == 0 and s >= w

    w_pad = jnp.pad(w_in.astype(BF16), ((0, 0), (0, LANES - GLA_GATE_RANK)))
    wa2_pad = jnp.pad(w_a2, ((0, LANES - GLA_GATE_RANK), (0, 0)))
    wo = w_out.astype(BF16)
    gpre = g_pre.reshape(1, d)
    gpost = g_post.reshape(1, d)
    ggla = g_gla.reshape(1, GLA_DV)
    ba2 = b_a2.reshape(1, d_gk)

    n_c = b + nseq
    n_c_pad = -(-n_c // SUBLANES) * SUBLANES
    c_all = jnp.pad(jnp.concatenate([c_prompt, c_sample], axis=0), ((0, n_c_pad - n_c), (0, 0)))
    mod = _mod_call(c_all, w_mod, b_mod)
    mod_p = mod[:b].reshape(b, 1, 3 * d)

    dims = dict(d_att=d_att, d_gk=d_gk, d_gv=d_gv)

    rows = nseq * t_dec
    tms = min(512, rows)
    xs = x_sample.reshape(rows // tms, tms, d)
    mod_s = jnp.repeat(mod[b:b + nseq], t_dec, axis=0).reshape(rows // tms, tms, 3 * d)
    tab_s = _rope_tables(np.tile(PAST_LEN + np.arange(t_dec), tms // t_dec))
    outs = _proj_call(xs, mod_s, gpre, w_pad, wa2_pad, ba2, tab_s, tm=tms, **dims)
    qkv_s, ga_s, qg_s, kg_s, la_s, vg_s, gg_s = [o.reshape(rows, o.shape[2]) for o in outs]

    tab_p = _rope_tables(np.arange(s))
    (o_att_s, k_win_sample, v_win_sample, qkv, gates, k_win_prompt, v_win_prompt,
     gla_p) = _stream_call(qkv_s, cache_k_win, cache_v_win, t_dec,
                           x_prompt, mod_p, gpre, w_pad, wa2_pad, ba2, tab_p, **dims)

    att = _att_prompt_call(qkv)
    og_s, gla_s = _gla_sample_call(qg_s, kg_s, la_s, vg_s, state_gla, t_dec)

    assert d_att == d_gv
    y_prompt = _finish_call(x_prompt, mod_p, att, (gates, 0), (gates, 2), (gates, 1),
                            ggla, gpost, wo, tm=512, d_att=d_att, d_gv=d_gv)
    r3 = lambda a: a.reshape(rows // tms, tms, a.shape[1])
    y_sample = _finish_call(xs, mod_s, r3(o_att_s), (r3(ga_s), 0), (r3(og_s), 0), (r3(gg_s), 0),
                            ggla, gpost, wo, tm=tms, d_att=d_att, d_gv=d_gv).reshape(x_sample.shape)

    return (y_prompt, y_sample, k_win_prompt, v_win_prompt, gla_p,
            k_win_sample, v_win_sample, gla_s)
```

```python
import functools
import math

import numpy as np
import jax
import jax.numpy as jnp
from jax import lax
from jax.experimental import pallas as pl
from jax.experimental.pallas import tpu as pltpu

F32 = jnp.float32
BF16 = jnp.bfloat16

HEAD_DIM = 64
ROT_DIM = HEAD_DIM // 4
ROPE_THETA = 500000.0
DILATIONS = (1, 4, 16)
STEPS = 128
ATT_BLOCK = 128
UNITS_PER_ITER = 32
GATHERS_PER_ITER = 4
STAGED_DILATION = 4
PAST_LEN = 8192
GLA_DK = 64
GLA_DV = 128
GLA_GATE_RANK = 16
GLA_TAU = 16.0
EPS = 1e-6
NEG = -1e30
LOG2E = math.log2(math.e)

LANES = 128
SUBLANES = 8
VMEM_LIMIT = 48 * 1024 * 1024
STREAM_VMEM_LIMIT = 56 * 1024 * 1024


def _cparams(*sem):
    return pltpu.CompilerParams(dimension_semantics=sem, vmem_limit_bytes=VMEM_LIMIT)


def _nt(a, b):
    return lax.dot_general(a, b, (((1,), (1,)), ((), ())), preferred_element_type=F32)


def _split3(x):
    hi = x.astype(BF16)
    r1 = x - hi.astype(F32)
    mid = r1.astype(BF16)
    lo = (r1 - mid.astype(F32)).astype(BF16)
    return hi, mid, lo


def _mod_kernel(c_ref, w_ref, b_ref, o_ref):
    c = c_ref[...]
    a = c / (1.0 + jnp.exp(-c))
    a_hi, a_mid, _ = _split3(a)
    w_hi, w_mid, _ = _split3(w_ref[...])
    dot = functools.partial(jnp.dot, preferred_element_type=F32)
    o_ref[...] = dot(a_hi, w_hi) + (dot(a_hi, w_mid) + dot(a_mid, w_hi)) + b_ref[...]


def _mod_call(c, w_mod, b_mod):
    n, d = c.shape
    d3 = w_mod.shape[1]
    tn = 1024
    return pl.pallas_call(
        _mod_kernel,
        grid=(d3 // tn,),
        in_specs=[pl.BlockSpec((n, d), lambda j: (0, 0)),
                  pl.BlockSpec((d, tn), lambda j: (0, j)),
                  pl.BlockSpec((1, tn), lambda j: (0, j))],
        out_specs=pl.BlockSpec((n, tn), lambda j: (0, j)),
        out_shape=jax.ShapeDtypeStruct((n, d3), F32),
        compiler_params=_cparams("arbitrary"),
        name="mod",
    )(c, w_mod, b_mod.reshape(1, d3))


def _rope_tables(pos):
    f32 = np.float32
    half = ROT_DIM // 2
    inv = (f32(1.0) / (f32(ROPE_THETA) ** (np.arange(half, dtype=f32) / f32(half)))).astype(f32)
    ang = np.asarray(pos, f32)[:, None] * inv[None, :]
    cos, sin = np.cos(ang).astype(f32), np.sin(ang).astype(f32)
    n = ang.shape[0]
    ones = np.ones((n, HEAD_DIM - ROT_DIM), f32)
    zeros_h = np.zeros((n, half), f32)
    zeros_r = np.zeros((n, HEAD_DIM - ROT_DIM), f32)
    c = np.concatenate([cos, cos, ones], axis=1)
    s1 = np.concatenate([-sin, zeros_h, zeros_r], axis=1)
    s2 = np.concatenate([zeros_h, sin, zeros_r], axis=1)
    rep = LANES // HEAD_DIM
    return jnp.asarray(np.concatenate([np.tile(t, (1, rep)) for t in (c, s1, s2)], axis=1))


def _proj_tasks(x, shsc, gpre_ref, w_ref, walr_ref, wa2_ref, ba2_ref, tab_ref, *, d_att, d_gk, d_gv):
    d = x.shape[1]
    ms = jnp.mean(x * x, axis=-1, keepdims=True)
    h = x * lax.rsqrt(ms + EPS) * gpre_ref[...]
    h = h * (1.0 + shsc[:, d:]) + shsc[:, :d]
    hb = h.astype(BF16)
    reps = d_att // LANES
    half = ROT_DIM // 2
    res = {}

    def seg(a, n):
        return jnp.dot(hb, w_ref[:, a:a + n], preferred_element_type=F32)

    def rot(t):
        rc = jnp.concatenate([tab_ref[:, 0:LANES]] * reps, axis=1)
        rs1 = jnp.concatenate([tab_ref[:, LANES:2 * LANES]] * reps, axis=1)
        rs2 = jnp.concatenate([tab_ref[:, 2 * LANES:3 * LANES]] * reps, axis=1)
        up = pltpu.roll(t, d_att - half, axis=1)
        dn = pltpu.roll(t, half, axis=1)
        return t * rc + up * rs1 + dn * rs2

    def silu(t):
        return t / (1.0 + jnp.exp(-t))

    def log_decay(t):
        a1, a2, a3 = _split3(t)
        w1, w2, w3 = _split3(wa2_ref[...])
        dot = functools.partial(jnp.dot, preferred_element_type=F32)
        z = (dot(a1, w1) + dot(a1, w2) + dot(a2, w1)
             + dot(a1, w3) + dot(a2, w2) + dot(a3, w1)) + ba2_ref[...]
        logsig = jnp.minimum(z, 0.0) - jnp.log1p(jnp.exp(-jnp.abs(z)))
        return logsig * (1.0 / GLA_TAU)

    plan = (("q", d_att, lambda t: rot(t) * (HEAD_DIM ** -0.5)), ("k", d_att, rot),
            ("v", d_att, None), ("ga", d_att, silu),
            ("qg", d_gk, lambda t: t * (GLA_DK ** -0.5)), ("kg", d_gk, None),
            ("vg", d_gv, None), ("gg", d_gv, silu))
    steps, off = [], 0
    for name, width, post in plan:
        def step(name=name, width=width, post=post, off=off):
            t = seg(off, width)
            res[name] = t if post is None else post(t)
        steps.append(step)
        off += width

    def decay_step():
        res["la"] = log_decay(jnp.dot(hb, walr_ref[...], preferred_element_type=F32))

    return steps + [decay_step], res


def _proj_kernel(x_ref, shsc_ref, gpre_ref, w_ref, walr_ref, wa2_ref, ba2_ref, tab_ref,
                 qkv_ref, *out_refs, d_att, d_gk, d_gv):
    steps, res = _proj_tasks(x_ref[0], shsc_ref[0], gpre_ref, w_ref, walr_ref, wa2_ref, ba2_ref,
                             tab_ref, d_att=d_att, d_gk=d_gk, d_gv=d_gv)
    for step in steps:
        step()
    for i, name in enumerate(("q", "k", "v")):
        qkv_ref[0, :, i * d_att:(i + 1) * d_att] = res[name]
    for ref, name in zip(out_refs, ("ga", "qg", "kg", "la", "vg", "gg")):
        ref[0] = res[name]


def _proj_call(x3, mod3, gpre, w_main, w_alr, wa2_pad, ba2, table, *, d_att, d_gk, d_gv, tm):
    g, r, d = x3.shape
    grid = (g, r // tm)
    full = lambda a: pl.BlockSpec(a.shape, lambda i, j: (0,) * a.ndim)
    widths = (3 * d_att, d_att, d_gk, d_gk, d_gk, d_gv, d_gv)
    kern = functools.partial(_proj_kernel, d_att=d_att, d_gk=d_gk, d_gv=d_gv)
    return pl.pallas_call(
        kern,
        grid=grid,
        in_specs=[pl.BlockSpec((1, tm, d), lambda i, j: (i, j, 0)),
                  pl.BlockSpec((1, tm, 2 * d), lambda i, j: (i, j, 0)),
                  full(gpre), full(w_main), full(w_alr), full(wa2_pad), full(ba2),
                  pl.BlockSpec((tm, 3 * LANES), lambda i, j: (0, 0))],
        out_specs=[pl.BlockSpec((1, tm, w), lambda i, j: (i, j, 0)) for w in widths],
        out_shape=[jax.ShapeDtypeStruct((g, r, w), F32) for w in widths],
        compiler_params=_cparams("parallel", "arbitrary"),
        name="proj",
    )(x3, mod3, gpre, w_main, w_alr, wa2_pad, ba2, table)


def _band_bias():
    row = np.arange(ATT_BLOCK)[:, None]
    col = np.arange(2 * ATT_BLOCK)[None, :]
    dist = row + ATT_BLOCK - col
    band = (dist >= 0) & (dist <= STEPS)
    first = band & (col >= ATT_BLOCK)
    return jnp.asarray(np.where(np.stack([first, band]), 0.0, NEG), F32)


def _att_buf_rows(seq):
    offs, total = {}, 0
    for dil in DILATIONS:
        offs[dil] = total
        total += seq + dil * ATT_BLOCK
    return offs, total


def _att_prompt_kernel(q_ref, k_ref, v_ref, bias_ref, o_ref,
                       qb_ref, kb_ref, vb_ref, acc_ref, m_ref, l_ref, *stage_refs, seq):
    blk = ATT_BLOCK
    n_units = seq // blk
    offs, _ = _att_buf_rows(seq)
    lane = lax.broadcasted_iota(jnp.int32, (blk, LANES), 1)
    lo = lane < HEAD_DIM
    head_masks = (jnp.where(lo, 1.0, 0.0).astype(BF16), jnp.where(lo, 0.0, 1.0).astype(BF16))
    ones = jnp.ones((2 * blk, LANES), BF16)
    zeros = jnp.zeros((blk, LANES), BF16)

    def place(u, dil):
        r = lax.rem(u, dil)
        n = lax.div(u, dil)
        first = r + dil * blk * n
        base = offs[dil] + r * (seq // dil + blk) + blk * n
        return n, first, pl.multiple_of(base, blk)

    def rows(first, dil):
        return pl.ds(first, blk) if dil == 1 else pl.ds(first, blk, stride=dil)

    srcs = (q_ref, k_ref, v_ref)
    bufs = (qb_ref, kb_ref, vb_ref)
    scales = (LOG2E, None, None)
    parent = {d: max([p for p in DILATIONS if 1 < p < d and d % p == 0], default=None) for d in DILATIONS}
    assert set(parent.values()) <= {None, STAGED_DILATION}
    for dil in DILATIONS:
        for r in range(dil):
            z0 = offs[dil] + r * (seq // dil + blk)
            kb_ref[z0:z0 + blk, :] = zeros
            vb_ref[z0:z0 + blk, :] = zeros

        def gather(u, carry, dil=dil):
            n, first, base = place(u, dil)
            r = lax.rem(u, dil)
            par = parent[dil]
            for a in range(3):
                if par is None:
                    x = srcs[a][0, rows(first, dil), :]
                else:
                    step = dil // par
                    start = lax.rem(r, par) * (seq // par) + lax.div(r, par) + step * blk * n
                    x = stage_refs[a][pl.ds(start, blk, stride=step), :]
                if dil == STAGED_DILATION:
                    stage_refs[a][pl.ds(pl.multiple_of(r * (seq // dil) + blk * n, blk), blk), :] = x
                if scales[a] is not None:
                    x = x * scales[a]
                bufs[a][pl.ds(base + blk, blk), :] = x.astype(BF16)
            return carry

        lax.fori_loop(0, n_units, gather, 0, unroll=GATHERS_PER_ITER)

    order = tuple(reversed(DILATIONS))
    for dil in order:
        def unit(u, carry, dil=dil):
            n, first, base = place(u, dil)
            qb = qb_ref[pl.ds(base + blk, blk), :]
            kcat = kb_ref[pl.ds(base, 2 * blk), :]
            vext = jnp.concatenate([vb_ref[pl.ds(base, 2 * blk), :], ones], axis=1)
            bias = bias_ref[jnp.minimum(n, 1)]
            os, ms, ls = [], [], []
            for hm in head_masks:
                s = _nt(qb * hm, kcat) + bias
                m = jnp.max(s, axis=-1, keepdims=True)
                p = jnp.exp2(s - m)
                res = jnp.dot(p.astype(BF16), vext, preferred_element_type=F32)
                os.append(res[:, :LANES])
                ls.append(res[:, LANES:])
                ms.append(m)
            o_new = jnp.where(lo, os[0], os[1])
            m_new = jnp.where(lo, ms[0], ms[1])
            l_new = jnp.where(lo, ls[0], ls[1])
            dst = rows(first, dil)
            if dil == order[0]:
                acc_ref[dst, :] = o_new
                m_ref[dst, :] = m_new
                l_ref[dst, :] = l_new
                return carry
            m_old = m_ref[dst, :]
            m_tot = jnp.maximum(m_old, m_new)
            a_old = jnp.exp2(m_old - m_tot)
            a_new = jnp.exp2(m_new - m_tot)
            acc = a_old * acc_ref[dst, :] + a_new * o_new
            den = a_old * l_ref[dst, :] + a_new * l_new
            if dil == order[-1]:
                o_ref[0, pl.ds(pl.multiple_of(first, blk), blk), :] = (acc / den).astype(o_ref.dtype)
            else:
                acc_ref[dst, :] = acc
                l_ref[dst, :] = den
                m_ref[dst, :] = m_tot
            return carry

        lax.fori_loop(0, n_units, unit, 0, unroll=UNITS_PER_ITER)


def _att_prompt_call(qkv):
    b, s, d3 = qkv.shape
    d_att = d3 // 3
    nblk = d_att // LANES
    bias = _band_bias()
    _, buf_rows = _att_buf_rows(s)
    slab = pl.BlockSpec((1, s, LANES), lambda i, j: (i, 0, j))
    part = lambda p: pl.BlockSpec((1, s, LANES), lambda i, j: (i, 0, p * nblk + j))
    kern = functools.partial(_att_prompt_kernel, seq=s)
    return pl.pallas_call(
        kern,
        grid=(b, d_att // LANES),
        in_specs=[part(0), part(1), part(2), pl.BlockSpec(bias.shape, lambda i, j: (0, 0, 0))],
        out_specs=slab,
        out_shape=jax.ShapeDtypeStruct((b, s, d_att), BF16),
        scratch_shapes=[pltpu.VMEM((buf_rows, LANES), BF16)] * 3 + [pltpu.VMEM((s, LANES), F32)] * 6,
        compiler_params=_cparams("parallel", "parallel"),
        name="att_prompt",
    )(qkv, qkv, qkv, bias)


def _gla_consts(blk):
    t = np.arange(LANES)
    same = (t[:, None] // blk) == (t[None, :] // blk)
    tri = ((t[None, :] <= t[:, None]) & same).astype(np.int64)
    mats, lvl, right_rows = [tri], [], []
    h = blk // 2
    while h >= 1:
        pair = t // (2 * h)
        right = (t // h) % 2 == 1
        mid = pair * 2 * h + h - 1
        upto_mid = ((t[None, :] <= mid[:, None]) & same).astype(np.int64)
        mats.append(tri - upto_mid)
        lvl.append(pair[:, None] == pair[None, :])
        right_rows.append(np.broadcast_to(right[:, None], (LANES, LANES)))
        h //= 2
    lvl.append(np.eye(LANES, dtype=bool))
    if blk < LANES:
        mats.append(same.astype(np.int64))
    cmat = jnp.asarray(np.concatenate(mats, 0), BF16)
    lvl = jnp.asarray(np.concatenate(lvl, 0), F32)
    rmask = jnp.asarray(np.concatenate(right_rows, 0), F32)
    return cmat, lvl, rmask, len(right_rows)


def _gla_tile_tasks(q, k, la, cmat_ref, lvl_ref, rmask_ref, n_lvl):
    t = LANES
    n_pair = q.shape[1] // LANES
    st = {}
    pairs = [None] * n_pair

    def cumulate():
        hi, mid, lo3 = _split3(la)
        cm = cmat_ref[...]
        dot = functools.partial(jnp.dot, preferred_element_type=F32)
        g = dot(cm, hi) + dot(cm, mid) + dot(cm, lo3)
        b = g[0:t]
        if cm.shape[0] > (n_lvl + 1) * t:
            btot = g[(n_lvl + 1) * t:(n_lvl + 2) * t]
        else:
            btot = jnp.broadcast_to(b[t - 1:t, :], b.shape)
        st.update(g=g, b=b, btot=btot)

    def pair(j):
        g, b, btot = st["g"], st["b"], st["btot"]
        lane = lax.broadcasted_iota(jnp.int32, (t, LANES), 1)
        lo = lane < GLA_DK
        sl = slice(j * LANES, (j + 1) * LANES)
        qp, kp, bp = q[:, sl], k[:, sl], b[:, sl]
        a0 = jnp.zeros((t, t), F32)
        a1 = jnp.zeros((t, t), F32)
        for lv in range(n_lvl + 1):
            m = lvl_ref[lv * t:(lv + 1) * t, :]
            if lv < n_lvl:
                dl = g[(lv + 1) * t:(lv + 2) * t, sl]
                e = jnp.exp(-jnp.abs(dl))
                rm = rmask_ref[lv * t:(lv + 1) * t, :]
                x = qp * e * rm
                y = (kp * e * (1.0 - rm)).astype(BF16)
            else:
                x = qp
                y = kp.astype(BF16)
            x0 = jnp.where(lo, x, 0.0).astype(BF16)
            x1 = jnp.where(lo, 0.0, x).astype(BF16)
            a0 = a0 + m * _nt(x0, y)
            a1 = a1 + m * _nt(x1, y)
        q_in = qp * jnp.exp(bp)
        k_in = kp * jnp.exp(btot[:, sl] - bp)
        pairs[j] = (a0, a1, q_in, k_in, btot[:, sl], lo)

    return [cumulate] + [functools.partial(pair, j) for j in range(n_pair)], pairs


def _gla_prompt_tasks(res, cmat_ref, lvl_ref, rmask_ref, o_ref, o_col, s_ref, *, n_lvl, d_gk):
    t = LANES
    n_pair = d_gk // LANES
    holder = {}

    def start():
        steps, pairs = _gla_tile_tasks(res["qg"], res["kg"], res["la"],
                                       cmat_ref, lvl_ref, rmask_ref, n_lvl)
        holder.update(steps=steps, pairs=pairs)
        steps[0]()

    def levels(j):
        holder["steps"][1 + j]()

    def finish(j):
        a0, a1, q_in, k_in, btot, lo = holder["pairs"][j]
        v = res["vg"]
        row = lax.broadcasted_iota(jnp.int32, (t, t), 0)
        top = row < GLA_DK
        s_prev = s_ref[j]
        sb = s_prev.astype(BF16)
        kt = k_in.T.astype(BF16)
        us = []
        for hh, (a, sel) in enumerate(((a0, lo), (a1, ~lo))):
            vs = slice((2 * j + hh) * GLA_DV, (2 * j + hh + 1) * GLA_DV)
            vb = v[:, vs].astype(BF16)
            qm = jnp.where(sel, q_in, 0.0).astype(BF16)
            o = (jnp.dot(a.astype(BF16), vb, preferred_element_type=F32)
                 + jnp.dot(qm, sb, preferred_element_type=F32))
            o_ref[0, :, o_col + vs.start:o_col + vs.stop] = o.astype(o_ref.dtype)
            us.append(jnp.dot(kt, vb, preferred_element_type=F32))
        a_col = jnp.exp(btot.T)
        s_ref[j] = a_col * s_prev + jnp.where(top, us[0], us[1])

    tasks = [start]
    for j in range(n_pair):
        tasks += [functools.partial(levels, j), functools.partial(finish, j)]
    return tasks


def _gla_sample_kernel(q_ref, k_ref, la_ref, v_ref, s_in_ref, cmat_ref, lvl_ref, rmask_ref,
                       o_ref, s_out_ref, *, n_lvl, t_dec):
    steps, pairs = _gla_tile_tasks(q_ref[...], k_ref[...], la_ref[...],
                                   cmat_ref, lvl_ref, rmask_ref, n_lvl)
    for step in steps:
        step()
    t = LANES
    nb = t // t_dec
    row = lax.broadcasted_iota(jnp.int32, (t, t), 0)
    col = lax.broadcasted_iota(jnp.int32, (t, t), 1)
    top = row < GLA_DK
    for j, (a0, a1, q_in, k_in, btot, lo) in enumerate(pairs):
        s_prev = s_in_ref[:, j]
        sb = s_prev.astype(BF16)
        kt = k_in.T
        bt = btot.T
        vbs = []
        for hh, (a, sel) in enumerate(((a0, lo), (a1, ~lo))):
            vs = slice((2 * j + hh) * GLA_DV, (2 * j + hh + 1) * GLA_DV)
            vb = v_ref[:, vs].astype(BF16)
            vbs.append(vb)
            qm = jnp.where(sel, q_in, 0.0)
            o_inter = jnp.concatenate(
                [jnp.dot(qm[i * t_dec:(i + 1) * t_dec].astype(BF16), sb[i],
                         preferred_element_type=F32) for i in range(nb)], axis=0)
            o_ref[:, vs] = jnp.dot(a.astype(BF16), vb, preferred_element_type=F32) + o_inter
        for i in range(nb):
            mine = (col >= i * t_dec) & (col < (i + 1) * t_dec)
            kti = (kt * mine.astype(F32)).astype(BF16)
            u0 = jnp.dot(kti, vbs[0], preferred_element_type=F32)
            u1 = jnp.dot(kti, vbs[1], preferred_element_type=F32)
            a_col = jnp.exp(jnp.broadcast_to(bt[:, i * t_dec:i * t_dec + 1], (t, GLA_DV)))
            s_out_ref[i, j] = a_col * s_prev[i] + jnp.where(top, u0, u1)


def _gla_sample_call(qg, kg, la, vg, state, t_dec):
    rows, d_gk = qg.shape
    d_gv = vg.shape[1]
    n_pair = d_gk // LANES
    nseq = rows // t_dec
    t = LANES
    nb = t // t_dec
    cmat, lvl, rmask, n_lvl = _gla_consts(t_dec)
    s_in = state.reshape(nseq, n_pair, 2 * GLA_DK, GLA_DV)
    tok = lambda w: pl.BlockSpec((t, w), lambda i: (i, 0))
    full = lambda a: pl.BlockSpec(a.shape, lambda i: (0,) * a.ndim)
    st = pl.BlockSpec((nb, n_pair, t, GLA_DV), lambda i: (i, 0, 0, 0))
    kern = functools.partial(_gla_sample_kernel, n_lvl=n_lvl, t_dec=t_dec)
    o, s_out = pl.pallas_call(
        kern,
        grid=(rows // t,),
        in_specs=[tok(d_gk), tok(d_gk), tok(d_gk), tok(d_gv), st, full(cmat), full(lvl), full(rmask)],
        out_specs=[tok(d_gv), st],
        out_shape=[jax.ShapeDtypeStruct((rows, d_gv), F32),
                   jax.ShapeDtypeStruct(s_in.shape, F32)],
        compiler_params=_cparams("parallel"),
        name="gla_sample",
    )(qg, kg, la, vg, s_in, cmat, lvl, rmask)
    return o, s_out.reshape(state.shape)


def _sample_att_table(w, t_dec):
    idx = np.arange(w + LANES)[None, :]
    tq = np.arange(t_dec)[:, None]
    dist = w + tq - idx
    mult = np.zeros((t_dec, w + LANES), np.int64)
    for d in DILATIONS:
        mult += (dist >= 0) & (dist % d == 0) & (dist // d <= STEPS) & (idx < w + t_dec)
    logw = np.where(mult > 0, np.log(np.maximum(mult, 1)), NEG).astype(np.float32)
    return jnp.asarray(logw)


def _att_sample_tasks(qkv_ref, ck_ref, cv_ref, logw_ref, o_ref, ko_ref, vo_ref, *, w, t_dec, n_heads):
    lane = lax.broadcasted_iota(jnp.int32, (HEAD_DIM, LANES), 1)
    keep = lane < LANES - t_dec
    d_att = n_heads * HEAD_DIM
    zpad = jnp.zeros((LANES - t_dec, d_att), F32)
    q = qkv_ref[:, 0:d_att]
    knt = jnp.concatenate([qkv_ref[:, d_att:2 * d_att], zpad], axis=0).T
    vnt = jnp.concatenate([qkv_ref[:, 2 * d_att:3 * d_att], zpad], axis=0).T
    logw_c = logw_ref[:, 0:w]
    logw_n = logw_ref[:, w:w + LANES]
    outs = [None] * n_heads

    def head(h):
        hs = slice(h * HEAD_DIM, (h + 1) * HEAD_DIM)
        kt = ck_ref[0, h]
        vt = cv_ref[0, h]
        knh = knt[hs, :]
        vnh = vnt[hs, :]
        qh = q[:, hs].astype(BF16)
        s_c = jnp.dot(qh, kt.astype(BF16), preferred_element_type=F32) + logw_c
        s_n = jnp.dot(qh, knh.astype(BF16), preferred_element_type=F32) + logw_n
        m = jnp.maximum(jnp.max(s_c, axis=-1, keepdims=True), jnp.max(s_n, axis=-1, keepdims=True))
        p_c = jnp.exp(s_c - m)
        p_n = jnp.exp(s_n - m)
        l = jnp.sum(p_c, axis=-1, keepdims=True) + jnp.sum(p_n, axis=-1, keepdims=True)
        o = _nt(p_c.astype(BF16), vt.astype(BF16)) + _nt(p_n.astype(BF16), vnh.astype(BF16))
        outs[h] = o / l
        for src, new, dst in ((kt, knh, ko_ref), (vt, vnh, vo_ref)):
            sh = pltpu.roll(src, w - t_dec, axis=1)
            tail = jnp.where(keep, sh[:, w - LANES:], pltpu.roll(new, LANES - t_dec, axis=1))
            dst[0, h, :, 0:w - LANES] = sh[:, 0:w - LANES]
            dst[0, h, :, w - LANES:w] = tail

    def finish():
        o_ref[...] = jnp.concatenate(outs, axis=1)

    return [functools.partial(head, h) for h in range(n_heads)] + [finish]


def _stream_kernel(sqkv_ref, ck_ref, cv_ref, logw_ref,
                   x_ref, shsc_ref, gpre_ref, w_ref, walr_ref, wa2_ref, ba2_ref, tab_ref,
                   cmat_ref, lvl_ref, rmask_ref,
                   o_ref, ko_ref, vo_ref, pqkv_ref, gates_ref, kt_ref, vt_ref, s_out_ref,
                   s_ref, *, w, t_dec, n_heads, n_lvl, n_tiles, d_att, d_gk, d_gv):
    n = lax.rem(pl.program_id(0), n_tiles)

    @pl.when(n == 0)
    def _():
        s_ref[...] = jnp.zeros_like(s_ref)

    win_steps = _att_sample_tasks(sqkv_ref, ck_ref, cv_ref, logw_ref, o_ref, ko_ref, vo_ref,
                                  w=w, t_dec=t_dec, n_heads=n_heads)
    proj_steps, res = _proj_tasks(x_ref[0], shsc_ref[0], gpre_ref, w_ref, walr_ref, wa2_ref, ba2_ref,
                                  tab_ref, d_att=d_att, d_gk=d_gk, d_gv=d_gv)

    def emit():
        for i, name in enumerate(("q", "k", "v")):
            pqkv_ref[0, :, i * d_att:(i + 1) * d_att] = res[name]
        gates_ref[0, :, 0:d_att] = res["ga"].astype(gates_ref.dtype)
        gates_ref[0, :, d_att:d_att + d_gv] = res["gg"].astype(gates_ref.dtype)
        kt_ref[0] = res["k"].T
        vt_ref[0] = res["v"].T

    tile_steps = proj_steps + [emit] + _gla_prompt_tasks(
        res, cmat_ref, lvl_ref, rmask_ref, gates_ref, d_att + d_gv, s_ref, n_lvl=n_lvl, d_gk=d_gk)
    for step in win_steps + tile_steps:
        step()

    @pl.when(n == n_tiles - 1)
    def _():
        s_out_ref[0] = s_ref[...]


def _stream_call(sqkv, cache_k, cache_v, t_dec, x, mod_p, gpre, w_main, w_alr, wa2_pad, ba2, table,
                 *, d_att, d_gk, d_gv):
    nseq, w, n_heads, hd = cache_k.shape
    b, s, d = x.shape
    t = LANES
    n_tiles = s // t
    n_pair = d_gk // LANES
    first_win_tile = (s - w) // t
    assert nseq == b * n_tiles, "one 128-row prompt tile is paired with each sample sequence"
    ck = jnp.transpose(cache_k, (0, 2, 3, 1))
    cv = jnp.transpose(cache_v, (0, 2, 3, 1))
    logw = _sample_att_table(w, t_dec)
    cmat, lvl, rmask, n_lvl = _gla_consts(LANES)
    tok = lambda wd: pl.BlockSpec((t_dec, wd), lambda i: (i, 0))
    win = pl.BlockSpec((1, n_heads, hd, w), lambda i: (i, 0, 0, 0))
    once = lambda a: pl.BlockSpec(a.shape, lambda i: (0,) * a.ndim, pipeline_mode=pl.Buffered(1))
    ptile = lambda wd: pl.BlockSpec((1, t, wd), lambda i: (i // n_tiles, i % n_tiles, 0))
    pwin = pl.BlockSpec((1, d_att, t),
                        lambda i: (i // n_tiles, 0, jnp.maximum(i % n_tiles - first_win_tile, 0)))
    kern = functools.partial(_stream_kernel, w=w, t_dec=t_dec, n_heads=n_heads, n_lvl=n_lvl,
                             n_tiles=n_tiles, d_att=d_att, d_gk=d_gk, d_gv=d_gv)
    f32 = lambda *shape: jax.ShapeDtypeStruct(shape, F32)
    bf16 = lambda *shape: jax.ShapeDtypeStruct(shape, BF16)
    outs = pl.pallas_call(
        kern,
        grid=(nseq,),
        in_specs=[tok(3 * d_att), win, win, once(logw),
                  ptile(d), pl.BlockSpec((1, 1, 2 * d), lambda i: (i // n_tiles, 0, 0)),
                  once(gpre), once(w_main), once(w_alr), once(wa2_pad), once(ba2),
                  pl.BlockSpec((t, 3 * LANES), lambda i: (i % n_tiles, 0)),
                  once(cmat), once(lvl), once(rmask)],
        out_specs=[tok(d_att), win, win, ptile(3 * d_att), ptile(d_att + 2 * d_gv), pwin, pwin,
                   pl.BlockSpec((1, n_pair, t, GLA_DV), lambda i: (i // n_tiles, 0, 0, 0))],
        out_shape=[f32(nseq * t_dec, d_att), f32(*ck.shape), f32(*cv.shape),
                   f32(b, s, 3 * d_att), bf16(b, s, d_att + 2 * d_gv),
                   f32(b, d_att, w), f32(b, d_att, w), f32(b, n_pair, t, GLA_DV)],
        scratch_shapes=[pltpu.VMEM((n_pair, t, GLA_DV), F32)],
        compiler_params=pltpu.CompilerParams(dimension_semantics=("arbitrary",),
                                             vmem_limit_bytes=STREAM_VMEM_LIMIT),
        name="stream",
    )(sqkv, ck, cv, logw, x, mod_p, gpre, w_main, w_alr, wa2_pad, ba2, table, cmat, lvl, rmask)
    o, ko, vo, pqkv, gates, kt, vt, s_out = outs
    to_win = lambda a: jnp.transpose(a, (0, 3, 1, 2))
    return (o, to_win(ko), to_win(vo), pqkv, gates,
            to_win(kt.reshape(b, n_heads, hd, w)), to_win(vt.reshape(b, n_heads, hd, w)),
            s_out.reshape(b, 2 * n_pair, GLA_DK, GLA_DV))


def _finish_kernel(x_ref, gate_ref, att_ref, ga_ref, og_ref, gg_ref, ggla_ref, gpost_ref, wo_ref,
                   y_ref, *, d_att, d_gv):
    att = att_ref[0] * ga_ref[0]
    og = og_ref[0].astype(F32)
    gg = gg_ref[0].astype(F32)
    gl = []
    for h in range(d_gv // GLA_DV):
        sl = slice(h * GLA_DV, (h + 1) * GLA_DV)
        oh = og[:, sl]
        nh = oh * lax.rsqrt(jnp.mean(oh * oh, axis=-1, keepdims=True) + EPS) * ggla_ref[...]
        gl.append((nh * gg[:, sl]).astype(BF16))
    mix = jnp.concatenate([att.astype(BF16)] + gl, axis=1)
    y = jnp.dot(mix, wo_ref[...], preferred_element_type=F32)
    yn = y * lax.rsqrt(jnp.mean(y * y, axis=-1, keepdims=True) + EPS) * gpost_ref[...]
    y_ref[0] = x_ref[0] + gate_ref[0] * yn


def _finish_call(x3, mod3, att, ga, og, gg, g_gla, g_post, w_out, *, tm, d_att, d_gv):
    g, r, d = x3.shape
    per_row = mod3.shape[1] != 1
    if per_row:
        gate_spec = pl.BlockSpec((1, tm, d), lambda i, j: (i, j, 2))
    else:
        gate_spec = pl.BlockSpec((1, 1, d), lambda i, j: (i, 0, 2))
    tok = lambda w, c=0: pl.BlockSpec((1, tm, w), lambda i, j: (i, j, c))
    full = lambda a: pl.BlockSpec(a.shape, lambda i, j: (0,) * a.ndim)
    kern = functools.partial(_finish_kernel, d_att=d_att, d_gv=d_gv)
    return pl.pallas_call(
        kern,
        grid=(g, r // tm),
        in_specs=[tok(d), gate_spec, tok(d_att), tok(d_att, ga[1]), tok(d_gv, og[1]), tok(d_gv, gg[1]),
                  full(g_gla), full(g_post), full(w_out)],
        out_specs=tok(d),
        out_shape=jax.ShapeDtypeStruct((g, r, d), F32),
        compiler_params=_cparams("parallel", "arbitrary"),
        name="finish",
    )(x3, mod3, att, ga[0], og[0], gg[0], g_gla, g_post, w_out)


def kernel(x_prompt, x_sample, c_prompt, c_sample, cache_k_win, cache_v_win, state_gla,
           w_mod, b_mod, g_pre, g_post, w_in, w_a2, b_a2, g_gla, w_out):
    b, s, d = x_prompt.shape
    nseq, t_dec, _ = x_sample.shape
    w = cache_k_win.shape[1]
    d_att = cache_k_win.shape[2] * cache_k_win.shape[3]
    d_gk = w_a2.shape[1]
    d_gv = state_gla.shape[1] * state_gla.shape[3]
    d_main = 4 * d_att + 2 * d_gk + 2 * d_gv
    assert w_in.shape[1] == d_main + GLA_GATE_RANK and cache_k_win.shape[3] == HEAD_DIM
    assert state_gla.shape[2] == GLA_DK and state_gla.shape[3] == GLA_DV
    assert w >= STEPS * max(DILATIONS) and s % (ATT_BLOCK * max(DILATIONS)) == 0 and s >= w

    w_main = w_in[:, :d_main].astype(BF16)
    w_alr = jnp.pad(w_in[:, d_main:], ((0, 0), (0, LANES - GLA_GATE_RANK))).astype(BF16)
    wa2_pad = jnp.pad(w_a2, ((0, LANES - GLA_GATE_RANK), (0, 0)))
    wo = w_out.astype(BF16)
    gpre = g_pre.reshape(1, d)
    gpost = g_post.reshape(1, d)
    ggla = g_gla.reshape(1, GLA_DV)
    ba2 = b_a2.reshape(1, d_gk)

    n_c = b + nseq
    n_c_pad = -(-n_c // SUBLANES) * SUBLANES
    c_all = jnp.pad(jnp.concatenate([c_prompt, c_sample], axis=0), ((0, n_c_pad - n_c), (0, 0)))
    mod = _mod_call(c_all, w_mod, b_mod)
    mod_p = mod[:b].reshape(b, 1, 3 * d)

    dims = dict(d_att=d_att, d_gk=d_gk, d_gv=d_gv)

    rows = nseq * t_dec
    tms = min(512, rows)
    xs = x_sample.reshape(rows // tms, tms, d)
    mod_s = jnp.repeat(mod[b:b + nseq], t_dec, axis=0).reshape(rows // tms, tms, 3 * d)
    tab_s = _rope_tables(np.tile(PAST_LEN + np.arange(t_dec), tms // t_dec))
    outs = _proj_call(xs, mod_s, gpre, w_main, w_alr, wa2_pad, ba2, tab_s, tm=tms, **dims)
    qkv_s, ga_s, qg_s, kg_s, la_s, vg_s, gg_s = [o.reshape(rows, o.shape[2]) for o in outs]

    tab_p = _rope_tables(np.arange(s))
    (o_att_s, k_win_sample, v_win_sample, qkv, gates, k_win_prompt, v_win_prompt,
     gla_p) = _stream_call(qkv_s, cache_k_win, cache_v_win, t_dec,
                           x_prompt, mod_p, gpre, w_main, w_alr, wa2_pad, ba2, tab_p, **dims)

    att = _att_prompt_call(qkv)
    og_s, gla_s = _gla_sample_call(qg_s, kg_s, la_s, vg_s, state_gla, t_dec)

    assert d_att == d_gv
    y_prompt = _finish_call(x_prompt, mod_p, att, (gates, 0), (gates, 2), (gates, 1),
                            ggla, gpost, wo, tm=512, d_att=d_att, d_gv=d_gv)
    r3 = lambda a: a.reshape(rows // tms, tms, a.shape[1])
    y_sample = _finish_call(xs, mod_s, r3(o_att_s), (r3(ga_s), 0), (r3(og_s), 0), (r3(gg_s), 0),
                            ggla, gpost, wo, tm=tms, d_att=d_att, d_gv=d_gv).reshape(x_sample.shape)

    return (y_prompt, y_sample, k_win_prompt, v_win_prompt, gla_p,
            k_win_sample, v_win_sample, gla_s)
```

```python
import functools
import math

import numpy as np
import jax
import jax.numpy as jnp
from jax import lax
from jax.experimental import pallas as pl
from jax.experimental.pallas import tpu as pltpu

F32 = jnp.float32
BF16 = jnp.bfloat16

HEAD_DIM = 64
ROT_DIM = HEAD_DIM // 4
ROPE_THETA = 500000.0
DILATIONS = (1, 4, 16)
STEPS = 128
ATT_BLOCK = 128
UNITS_PER_ITER = 32
GATHERS_PER_ITER = 4
STAGED_DILATION = 4
PAST_LEN = 8192
GLA_DK = 64
GLA_DV = 128
GLA_GATE_RANK = 16
GLA_TAU = 16.0
EPS = 1e-6
NEG = -1e30
LOG2E = math.log2(math.e)

LANES = 128
SUBLANES = 8
VMEM_LIMIT = 48 * 1024 * 1024
STREAM_VMEM_LIMIT = 56 * 1024 * 1024


def _cparams(*sem):
    return pltpu.CompilerParams(dimension_semantics=sem, vmem_limit_bytes=VMEM_LIMIT)


def _nt(a, b):
    return lax.dot_general(a, b, (((1,), (1,)), ((), ())), preferred_element_type=F32)


def _split3(x):
    hi = x.astype(BF16)
    r1 = x - hi.astype(F32)
    mid = r1.astype(BF16)
    lo = (r1 - mid.astype(F32)).astype(BF16)
    return hi, mid, lo


def _mod_kernel(c_ref, w_ref, b_ref, o_ref):
    c = c_ref[...]
    a = c / (1.0 + jnp.exp(-c))
    a_hi, a_mid, _ = _split3(a)
    w_hi, w_mid, _ = _split3(w_ref[...])
    dot = functools.partial(jnp.dot, preferred_element_type=F32)
    o_ref[...] = dot(a_hi, w_hi) + (dot(a_hi, w_mid) + dot(a_mid, w_hi)) + b_ref[...]


def _mod_call(c, w_mod, b_mod):
    n, d = c.shape
    d3 = w_mod.shape[1]
    tn = 1024
    return pl.pallas_call(
        _mod_kernel,
        grid=(d3 // tn,),
        in_specs=[pl.BlockSpec((n, d), lambda j: (0, 0)),
                  pl.BlockSpec((d, tn), lambda j: (0, j)),
                  pl.BlockSpec((1, tn), lambda j: (0, j))],
        out_specs=pl.BlockSpec((n, tn), lambda j: (0, j)),
        out_shape=jax.ShapeDtypeStruct((n, d3), F32),
        compiler_params=_cparams("arbitrary"),
        name="mod",
    )(c, w_mod, b_mod.reshape(1, d3))


def _rope_tables(pos):
    f32 = np.float32
    half = ROT_DIM // 2
    inv = (f32(1.0) / (f32(ROPE_THETA) ** (np.arange(half, dtype=f32) / f32(half)))).astype(f32)
    ang = np.asarray(pos, f32)[:, None] * inv[None, :]
    cos, sin = np.cos(ang).astype(f32), np.sin(ang).astype(f32)
    n = ang.shape[0]
    ones = np.ones((n, HEAD_DIM - ROT_DIM), f32)
    zeros_h = np.zeros((n, half), f32)
    zeros_r = np.zeros((n, HEAD_DIM - ROT_DIM), f32)
    c = np.concatenate([cos, cos, ones], axis=1)
    s1 = np.concatenate([-sin, zeros_h, zeros_r], axis=1)
    s2 = np.concatenate([zeros_h, sin, zeros_r], axis=1)
    rep = LANES // HEAD_DIM
    return jnp.asarray(np.concatenate([np.tile(t, (1, rep)) for t in (c, s1, s2)], axis=1))


def _proj_tasks(x, shsc, gpre_ref, w_ref, walr_ref, wa2_ref, ba2_ref, tab_ref, *, d_att, d_gk, d_gv):
    d = x.shape[1]
    ms = jnp.mean(x * x, axis=-1, keepdims=True)
    h = x * lax.rsqrt(ms + EPS) * gpre_ref[...]
    h = h * (1.0 + shsc[:, d:]) + shsc[:, :d]
    hb = h.astype(BF16)
    reps = d_att // LANES
    half = ROT_DIM // 2
    res = {}

    def seg(a, n):
        return jnp.dot(hb, w_ref[:, a:a + n], preferred_element_type=F32)

    def rot(t):
        rc = jnp.concatenate([tab_ref[:, 0:LANES]] * reps, axis=1)
        rs1 = jnp.concatenate([tab_ref[:, LANES:2 * LANES]] * reps, axis=1)
        rs2 = jnp.concatenate([tab_ref[:, 2 * LANES:3 * LANES]] * reps, axis=1)
        up = pltpu.roll(t, d_att - half, axis=1)
        dn = pltpu.roll(t, half, axis=1)
        return t * rc + up * rs1 + dn * rs2

    def silu(t):
        return t / (1.0 + jnp.exp(-t))

    def log_decay(t):
        a1, a2, a3 = _split3(t)
        w1, w2, w3 = _split3(wa2_ref[...])
        dot = functools.partial(jnp.dot, preferred_element_type=F32)
        z = (dot(a1, w1) + dot(a1, w2) + dot(a2, w1)
             + dot(a1, w3) + dot(a2, w2) + dot(a3, w1)) + ba2_ref[...]
        logsig = jnp.minimum(z, 0.0) - jnp.log1p(jnp.exp(-jnp.abs(z)))
        return logsig * (1.0 / GLA_TAU)

    plan = (("q", d_att, lambda t: rot(t) * (HEAD_DIM ** -0.5)), ("k", d_att, rot),
            ("v", d_att, None), ("ga", d_att, silu),
            ("qg", d_gk, lambda t: t * (GLA_DK ** -0.5)), ("kg", d_gk, None),
            ("vg", d_gv, None), ("gg", d_gv, silu))
    steps, off = [], 0
    for name, width, post in plan:
        def step(name=name, width=width, post=post, off=off):
            t = seg(off, width)
            res[name] = t if post is None else post(t)
        steps.append(step)
        off += width

    def decay_step():
        res["la"] = log_decay(jnp.dot(hb, walr_ref[...], preferred_element_type=F32))

    return steps + [decay_step], res


def _proj_kernel(x_ref, shsc_ref, gpre_ref, w_ref, walr_ref, wa2_ref, ba2_ref, tab_ref,
                 qkv_ref, *out_refs, d_att, d_gk, d_gv):
    steps, res = _proj_tasks(x_ref[0], shsc_ref[0], gpre_ref, w_ref, walr_ref, wa2_ref, ba2_ref,
                             tab_ref, d_att=d_att, d_gk=d_gk, d_gv=d_gv)
    for step in steps:
        step()
    for i, name in enumerate(("q", "k", "v")):
        qkv_ref[0, :, i * d_att:(i + 1) * d_att] = res[name]
    for ref, name in zip(out_refs, ("ga", "qg", "kg", "la", "vg", "gg")):
        ref[0] = res[name]


def _proj_call(x3, mod3, gpre, w_main, w_alr, wa2_pad, ba2, table, *, d_att, d_gk, d_gv, tm):
    g, r, d = x3.shape
    grid = (g, r // tm)
    full = lambda a: pl.BlockSpec(a.shape, lambda i, j: (0,) * a.ndim)
    widths = (3 * d_att, d_att, d_gk, d_gk, d_gk, d_gv, d_gv)
    kern = functools.partial(_proj_kernel, d_att=d_att, d_gk=d_gk, d_gv=d_gv)
    return pl.pallas_call(
        kern,
        grid=grid,
        in_specs=[pl.BlockSpec((1, tm, d), lambda i, j: (i, j, 0)),
                  pl.BlockSpec((1, tm, 2 * d), lambda i, j: (i, j, 0)),
                  full(gpre), full(w_main), full(w_alr), full(wa2_pad), full(ba2),
                  pl.BlockSpec((tm, 3 * LANES), lambda i, j: (0, 0))],
        out_specs=[pl.BlockSpec((1, tm, w), lambda i, j: (i, j, 0)) for w in widths],
        out_shape=[jax.ShapeDtypeStruct((g, r, w), F32) for w in widths],
        compiler_params=_cparams("parallel", "arbitrary"),
        name="proj",
    )(x3, mod3, gpre, w_main, w_alr, wa2_pad, ba2, table)


def _band_bias():
    row = np.arange(ATT_BLOCK)[:, None]
    col = np.arange(2 * ATT_BLOCK)[None, :]
    dist = row + ATT_BLOCK - col
    band = (dist >= 0) & (dist <= STEPS)
    first = band & (col >= ATT_BLOCK)
    return jnp.asarray(np.where(np.stack([first, band]), 0.0, NEG), F32)


def _att_buf_rows(seq):
    offs, total = {}, 0
    for dil in DILATIONS:
        offs[dil] = total
        total += seq + dil * ATT_BLOCK
    return offs, total


def _att_prompt_kernel(q_ref, k_ref, v_ref, bias_ref, o_ref,
                       qb_ref, kb_ref, vb_ref, acc_ref, m_ref, l_ref, *stage_refs, seq):
    blk = ATT_BLOCK
    n_units = seq // blk
    offs, _ = _att_buf_rows(seq)
    lane = lax.broadcasted_iota(jnp.int32, (blk, LANES), 1)
    lo = lane < HEAD_DIM
    head_masks = (jnp.where(lo, 1.0, 0.0).astype(BF16), jnp.where(lo, 0.0, 1.0).astype(BF16))
    ones = jnp.ones((2 * blk, LANES), BF16)
    zeros = jnp.zeros((blk, LANES), BF16)

    def place(u, dil):
        r = lax.rem(u, dil)
        n = lax.div(u, dil)
        first = r + dil * blk * n
        base = offs[dil] + r * (seq // dil + blk) + blk * n
        return n, first, pl.multiple_of(base, blk)

    def rows(first, dil):
        return pl.ds(first, blk) if dil == 1 else pl.ds(first, blk, stride=dil)

    srcs = (q_ref, k_ref, v_ref)
    bufs = (qb_ref, kb_ref, vb_ref)
    scales = (LOG2E, None, None)
    parent = {d: max([p for p in DILATIONS if 1 < p < d and d % p == 0], default=None) for d in DILATIONS}
    assert set(parent.values()) <= {None, STAGED_DILATION}
    for dil in DILATIONS:
        for r in range(dil):
            z0 = offs[dil] + r * (seq // dil + blk)
            kb_ref[z0:z0 + blk, :] = zeros
            vb_ref[z0:z0 + blk, :] = zeros

        def gather(u, carry, dil=dil):
            n, first, base = place(u, dil)
            r = lax.rem(u, dil)
            par = parent[dil]
            for a in range(3):
                if par is None:
                    x = srcs[a][0, rows(first, dil), :]
                else:
                    step = dil // par
                    start = lax.rem(r, par) * (seq // par) + lax.div(r, par) + step * blk * n
                    x = stage_refs[a][pl.ds(start, blk, stride=step), :]
                if dil == STAGED_DILATION:
                    stage_refs[a][pl.ds(pl.multiple_of(r * (seq // dil) + blk * n, blk), blk), :] = x
                if scales[a] is not None:
                    x = x * scales[a]
                bufs[a][pl.ds(base + blk, blk), :] = x.astype(BF16)
            return carry

        lax.fori_loop(0, n_units, gather, 0, unroll=GATHERS_PER_ITER)

    order = tuple(reversed(DILATIONS))
    for dil in order:
        def unit(u, carry, dil=dil):
            n, first, base = place(u, dil)
            qb = qb_ref[pl.ds(base + blk, blk), :]
            kcat = kb_ref[pl.ds(base, 2 * blk), :]
            vext = jnp.concatenate([vb_ref[pl.ds(base, 2 * blk), :], ones], axis=1)
            bias = bias_ref[jnp.minimum(n, 1)]
            os, ms, ls = [], [], []
            for hm in head_masks:
                s = _nt(qb * hm, kcat) + bias
                m = jnp.max(s, axis=-1, keepdims=True)
                p = jnp.exp2(s - m)
                res = jnp.dot(p.astype(BF16), vext, preferred_element_type=F32)
                os.append(res[:, :LANES])
                ls.append(res[:, LANES:])
                ms.append(m)
            o_new = jnp.where(lo, os[0], os[1])
            m_new = jnp.where(lo, ms[0], ms[1])
            l_new = jnp.where(lo, ls[0], ls[1])
            dst = rows(first, dil)
            if dil == order[0]:
                acc_ref[dst, :] = o_new
                m_ref[dst, :] = m_new
                l_ref[dst, :] = l_new
                return carry
            m_old = m_ref[dst, :]
            m_tot = jnp.maximum(m_old, m_new)
            a_old = jnp.exp2(m_old - m_tot)
            a_new = jnp.exp2(m_new - m_tot)
            acc = a_old * acc_ref[dst, :] + a_new * o_new
            den = a_old * l_ref[dst, :] + a_new * l_new
            if dil == order[-1]:
                o_ref[0, pl.ds(pl.multiple_of(first, blk), blk), :] = (acc / den).astype(o_ref.dtype)
            else:
                acc_ref[dst, :] = acc
                l_ref[dst, :] = den
                m_ref[dst, :] = m_tot
            return carry

        lax.fori_loop(0, n_units, unit, 0, unroll=UNITS_PER_ITER)


def _att_prompt_call(qkv):
    b, s, d3 = qkv.shape
    d_att = d3 // 3
    nblk = d_att // LANES
    bias = _band_bias()
    _, buf_rows = _att_buf_rows(s)
    slab = pl.BlockSpec((1, s, LANES), lambda i, j: (i, 0, j))
    part = lambda p: pl.BlockSpec((1, s, LANES), lambda i, j: (i, 0, p * nblk + j))
    kern = functools.partial(_att_prompt_kernel, seq=s)
    return pl.pallas_call(
        kern,
        grid=(b, d_att // LANES),
        in_specs=[part(0), part(1), part(2), pl.BlockSpec(bias.shape, lambda i, j: (0, 0, 0))],
        out_specs=slab,
        out_shape=jax.ShapeDtypeStruct((b, s, d_att), BF16),
        scratch_shapes=[pltpu.VMEM((buf_rows, LANES), BF16)] * 3 + [pltpu.VMEM((s, LANES), F32)] * 6,
        compiler_params=_cparams("parallel", "parallel"),
        name="att_prompt",
    )(qkv, qkv, qkv, bias)


def _gla_consts(blk):
    t = np.arange(LANES)
    same = (t[:, None] // blk) == (t[None, :] // blk)
    tri = ((t[None, :] <= t[:, None]) & same).astype(np.int64)
    mats, lvl, right_rows = [tri], [], []
    h = blk // 2
    while h >= 1:
        pair = t // (2 * h)
        right = (t // h) % 2 == 1
        mid = pair * 2 * h + h - 1
        upto_mid = ((t[None, :] <= mid[:, None]) & same).astype(np.int64)
        mats.append(tri - upto_mid)
        lvl.append(pair[:, None] == pair[None, :])
        right_rows.append(np.broadcast_to(right[:, None], (LANES, LANES)))
        h //= 2
    lvl.append(np.eye(LANES, dtype=bool))
    if blk < LANES:
        mats.append(same.astype(np.int64))
    cmat = jnp.asarray(np.concatenate(mats, 0), BF16)
    lvl = jnp.asarray(np.concatenate(lvl, 0), F32)
    rmask = jnp.asarray(np.concatenate(right_rows, 0), F32)
    return cmat, lvl, rmask, len(right_rows)


def _gla_tile_tasks(q, k, la, cmat_ref, lvl_ref, rmask_ref, n_lvl):
    t = LANES
    n_pair = q.shape[1] // LANES
    st = {}
    pairs = [None] * n_pair

    def cumulate():
        hi, mid, lo3 = _split3(la)
        cm = cmat_ref[...]
        dot = functools.partial(jnp.dot, preferred_element_type=F32)
        g = dot(cm, hi) + dot(cm, mid) + dot(cm, lo3)
        b = g[0:t]
        if cm.shape[0] > (n_lvl + 1) * t:
            btot = g[(n_lvl + 1) * t:(n_lvl + 2) * t]
        else:
            btot = jnp.broadcast_to(b[t - 1:t, :], b.shape)
        st.update(g=g, b=b, btot=btot)

    def pair(j):
        g, b, btot = st["g"], st["b"], st["btot"]
        lane = lax.broadcasted_iota(jnp.int32, (t, LANES), 1)
        lo = lane < GLA_DK
        sl = slice(j * LANES, (j + 1) * LANES)
        qp, kp, bp = q[:, sl], k[:, sl], b[:, sl]
        a0 = jnp.zeros((t, t), F32)
        a1 = jnp.zeros((t, t), F32)
        for lv in range(n_lvl + 1):
            m = lvl_ref[lv * t:(lv + 1) * t, :]
            if lv < n_lvl:
                dl = g[(lv + 1) * t:(lv + 2) * t, sl]
                e = jnp.exp(-jnp.abs(dl))
                rm = rmask_ref[lv * t:(lv + 1) * t, :]
                x = qp * e * rm
                y = (kp * e * (1.0 - rm)).astype(BF16)
            else:
                x = qp
                y = kp.astype(BF16)
            x0 = jnp.where(lo, x, 0.0).astype(BF16)
            x1 = jnp.where(lo, 0.0, x).astype(BF16)
            a0 = a0 + m * _nt(x0, y)
            a1 = a1 + m * _nt(x1, y)
        q_in = qp * jnp.exp(bp)
        k_in = kp * jnp.exp(btot[:, sl] - bp)
        pairs[j] = (a0, a1, q_in, k_in, btot[:, sl], lo)

    return [cumulate] + [functools.partial(pair, j) for j in range(n_pair)], pairs


def _gla_prompt_tasks(res, cmat_ref, lvl_ref, rmask_ref, o_ref, o_col, s_ref, *, n_lvl, d_gk):
    t = LANES
    n_pair = d_gk // LANES
    holder = {}

    def start():
        steps, pairs = _gla_tile_tasks(res["qg"], res["kg"], res["la"],
                                       cmat_ref, lvl_ref, rmask_ref, n_lvl)
        holder.update(steps=steps, pairs=pairs)
        steps[0]()

    def levels(j):
        holder["steps"][1 + j]()

    def finish(j):
        a0, a1, q_in, k_in, btot, lo = holder["pairs"][j]
        v = res["vg"]
        row = lax.broadcasted_iota(jnp.int32, (t, t), 0)
        top = row < GLA_DK
        s_prev = s_ref[j]
        sb = s_prev.astype(BF16)
        kt = k_in.T.astype(BF16)
        us = []
        for hh, (a, sel) in enumerate(((a0, lo), (a1, ~lo))):
            vs = slice((2 * j + hh) * GLA_DV, (2 * j + hh + 1) * GLA_DV)
            vb = v[:, vs].astype(BF16)
            qm = jnp.where(sel, q_in, 0.0).astype(BF16)
            o = (jnp.dot(a.astype(BF16), vb, preferred_element_type=F32)
                 + jnp.dot(qm, sb, preferred_element_type=F32))
            o_ref[0, :, o_col + vs.start:o_col + vs.stop] = o.astype(o_ref.dtype)
            us.append(jnp.dot(kt, vb, preferred_element_type=F32))
        a_col = jnp.exp(btot.T)
        s_ref[j] = a_col * s_prev + jnp.where(top, us[0], us[1])

    tasks = [start]
    for j in range(n_pair):
        tasks += [functools.partial(levels, j), functools.partial(finish, j)]
    return tasks


def _gla_sample_kernel(q_ref, k_ref, la_ref, v_ref, s_in_ref, cmat_ref, lvl_ref, rmask_ref,
                       o_ref, s_out_ref, *, n_lvl, t_dec):
    steps, pairs = _gla_tile_tasks(q_ref[...], k_ref[...], la_ref[...],
                                   cmat_ref, lvl_ref, rmask_ref, n_lvl)
    for step in steps:
        step()
    t = LANES
    nb = t // t_dec
    row = lax.broadcasted_iota(jnp.int32, (t, t), 0)
    col = lax.broadcasted_iota(jnp.int32, (t, t), 1)
    top = row < GLA_DK
    for j, (a0, a1, q_in, k_in, btot, lo) in enumerate(pairs):
        s_prev = s_in_ref[:, j]
        sb = s_prev.astype(BF16)
        kt = k_in.T
        bt = btot.T
        vbs = []
        for hh, (a, sel) in enumerate(((a0, lo), (a1, ~lo))):
            vs = slice((2 * j + hh) * GLA_DV, (2 * j + hh + 1) * GLA_DV)
            vb = v_ref[:, vs].astype(BF16)
            vbs.append(vb)
            qm = jnp.where(sel, q_in, 0.0)
            o_inter = jnp.concatenate(
                [jnp.dot(qm[i * t_dec:(i + 1) * t_dec].astype(BF16), sb[i],
                         preferred_element_type=F32) for i in range(nb)], axis=0)
            o_ref[:, vs] = jnp.dot(a.astype(BF16), vb, preferred_element_type=F32) + o_inter
        for i in range(nb):
            mine = (col >= i * t_dec) & (col < (i + 1) * t_dec)
            kti = (kt * mine.astype(F32)).astype(BF16)
            u0 = jnp.dot(kti, vbs[0], preferred_element_type=F32)
            u1 = jnp.dot(kti, vbs[1], preferred_element_type=F32)
            a_col = jnp.exp(jnp.broadcast_to(bt[:, i * t_dec:i * t_dec + 1], (t, GLA_DV)))
            s_out_ref[i, j] = a_col * s_prev[i] + jnp.where(top, u0, u1)


def _gla_sample_call(qg, kg, la, vg, state, t_dec):
    rows, d_gk = qg.shape
    d_gv = vg.shape[1]
    n_pair = d_gk // LANES
    nseq = rows // t_dec
    t = LANES
    nb = t // t_dec
    cmat, lvl, rmask, n_lvl = _gla_consts(t_dec)
    s_in = state.reshape(nseq, n_pair, 2 * GLA_DK, GLA_DV)
    tok = lambda w: pl.BlockSpec((t, w), lambda i: (i, 0))
    full = lambda a: pl.BlockSpec(a.shape, lambda i: (0,) * a.ndim)
    st = pl.BlockSpec((nb, n_pair, t, GLA_DV), lambda i: (i, 0, 0, 0))
    kern = functools.partial(_gla_sample_kernel, n_lvl=n_lvl, t_dec=t_dec)
    o, s_out = pl.pallas_call(
        kern,
        grid=(rows // t,),
        in_specs=[tok(d_gk), tok(d_gk), tok(d_gk), tok(d_gv), st, full(cmat), full(lvl), full(rmask)],
        out_specs=[tok(d_gv), st],
        out_shape=[jax.ShapeDtypeStruct((rows, d_gv), F32),
                   jax.ShapeDtypeStruct(s_in.shape, F32)],
        compiler_params=_cparams("parallel"),
        name="gla_sample",
    )(qg, kg, la, vg, s_in, cmat, lvl, rmask)
    return o, s_out.reshape(state.shape)


def _sample_att_table(w, t_dec):
    idx = np.arange(w + LANES)[None, :]
    tq = np.arange(t_dec)[:, None]
    dist = w + tq - idx
    mult = np.zeros((t_dec, w + LANES), np.int64)
    for d in DILATIONS:
        mult += (dist >= 0) & (dist % d == 0) & (dist // d <= STEPS) & (idx < w + t_dec)
    logw = np.where(mult > 0, np.log(np.maximum(mult, 1)), NEG).astype(np.float32)
    return jnp.asarray(logw)


def _att_sample_tasks(qkv_ref, ck_ref, cv_ref, logw_ref, o_ref, ko_ref, vo_ref, *, w, t_dec, n_heads):
    lane = lax.broadcasted_iota(jnp.int32, (HEAD_DIM, LANES), 1)
    keep = lane < LANES - t_dec
    d_att = n_heads * HEAD_DIM
    zpad = jnp.zeros((LANES - t_dec, d_att), F32)
    q = qkv_ref[:, 0:d_att]
    knt = jnp.concatenate([qkv_ref[:, d_att:2 * d_att], zpad], axis=0).T
    vnt = jnp.concatenate([qkv_ref[:, 2 * d_att:3 * d_att], zpad], axis=0).T
    logw_c = logw_ref[:, 0:w]
    logw_n = logw_ref[:, w:w + LANES]
    outs = [None] * n_heads

    def head(h):
        hs = slice(h * HEAD_DIM, (h + 1) * HEAD_DIM)
        kt = ck_ref[0, h]
        vt = cv_ref[0, h]
        knh = knt[hs, :]
        vnh = vnt[hs, :]
        qh = q[:, hs].astype(BF16)
        s_c = jnp.dot(qh, kt.astype(BF16), preferred_element_type=F32) + logw_c
        s_n = jnp.dot(qh, knh.astype(BF16), preferred_element_type=F32) + logw_n
        m = jnp.maximum(jnp.max(s_c, axis=-1, keepdims=True), jnp.max(s_n, axis=-1, keepdims=True))
        p_c = jnp.exp(s_c - m)
        p_n = jnp.exp(s_n - m)
        l = jnp.sum(p_c, axis=-1, keepdims=True) + jnp.sum(p_n, axis=-1, keepdims=True)
        o = _nt(p_c.astype(BF16), vt.astype(BF16)) + _nt(p_n.astype(BF16), vnh.astype(BF16))
        outs[h] = o / l
        for src, new, dst in ((kt, knh, ko_ref), (vt, vnh, vo_ref)):
            sh = pltpu.roll(src, w - t_dec, axis=1)
            tail = jnp.where(keep, sh[:, w - LANES:], pltpu.roll(new, LANES - t_dec, axis=1))
            dst[0, h, :, 0:w - LANES] = sh[:, 0:w - LANES]
            dst[0, h, :, w - LANES:w] = tail

    def finish():
        o_ref[...] = jnp.concatenate(outs, axis=1)

    return [functools.partial(head, h) for h in range(n_heads)] + [finish]


def _stream_kernel(sqkv_ref, ck_ref, cv_ref, logw_ref,
                   x_ref, shsc_ref, gpre_ref, w_ref, walr_ref, wa2_ref, ba2_ref, tab_ref,
                   cmat_ref, lvl_ref, rmask_ref,
                   o_ref, ko_ref, vo_ref, pqkv_ref, gates_ref, kt_ref, vt_ref, s_out_ref,
                   s_ref, *, w, t_dec, n_heads, n_lvl, n_tiles, d_att, d_gk, d_gv):
    n = lax.rem(pl.program_id(0), n_tiles)

    @pl.when(n == 0)
    def _():
        s_ref[...] = jnp.zeros_like(s_ref)

    win_steps = _att_sample_tasks(sqkv_ref, ck_ref, cv_ref, logw_ref, o_ref, ko_ref, vo_ref,
                                  w=w, t_dec=t_dec, n_heads=n_heads)
    proj_steps, res = _proj_tasks(x_ref[0], shsc_ref[0], gpre_ref, w_ref, walr_ref, wa2_ref, ba2_ref,
                                  tab_ref, d_att=d_att, d_gk=d_gk, d_gv=d_gv)

    def emit():
        for i, name in enumerate(("q", "k", "v")):
            pqkv_ref[0, :, i * d_att:(i + 1) * d_att] = res[name]
        gates_ref[0, :, 0:d_att] = res["ga"].astype(gates_ref.dtype)
        gates_ref[0, :, d_att:d_att + d_gv] = res["gg"].astype(gates_ref.dtype)
        kt_ref[0] = res["k"].T
        vt_ref[0] = res["v"].T

    tile_steps = proj_steps + [emit] + _gla_prompt_tasks(
        res, cmat_ref, lvl_ref, rmask_ref, gates_ref, d_att + d_gv, s_ref, n_lvl=n_lvl, d_gk=d_gk)
    for step in win_steps + tile_steps:
        step()

    @pl.when(n == n_tiles - 1)
    def _():
        s_out_ref[0] = s_ref[...]


def _stream_call(sqkv, cache_k, cache_v, t_dec, x, mod_p, gpre, w_main, w_alr, wa2_pad, ba2, table,
                 *, d_att, d_gk, d_gv):
    nseq, w, n_heads, hd = cache_k.shape
    b, s, d = x.shape
    t = LANES
    n_tiles = s // t
    n_pair = d_gk // LANES
    first_win_tile = (s - w) // t
    assert nseq == b * n_tiles, "one 128-row prompt tile is paired with each sample sequence"
    ck = jnp.transpose(cache_k, (0, 2, 3, 1))
    cv = jnp.transpose(cache_v, (0, 2, 3, 1))
    logw = _sample_att_table(w, t_dec)
    cmat, lvl, rmask, n_lvl = _gla_consts(LANES)
    tok = lambda wd: pl.BlockSpec((t_dec, wd), lambda i: (i, 0))
    win = pl.BlockSpec((1, n_heads, hd, w), lambda i: (i, 0, 0, 0))
    once = lambda a: pl.BlockSpec(a.shape, lambda i: (0,) * a.ndim, pipeline_mode=pl.Buffered(1))
    ptile = lambda wd: pl.BlockSpec((1, t, wd), lambda i: (i // n_tiles, i % n_tiles, 0))
    pwin = pl.BlockSpec((1, d_att, t),
                        lambda i: (i // n_tiles, 0, jnp.maximum(i % n_tiles - first_win_tile, 0)))
    kern = functools.partial(_stream_kernel, w=w, t_dec=t_dec, n_heads=n_heads, n_lvl=n_lvl,
                             n_tiles=n_tiles, d_att=d_att, d_gk=d_gk, d_gv=d_gv)
    f32 = lambda *shape: jax.ShapeDtypeStruct(shape, F32)
    bf16 = lambda *shape: jax.ShapeDtypeStruct(shape, BF16)
    outs = pl.pallas_call(
        kern,
        grid=(nseq,),
        in_specs=[tok(3 * d_att), win, win, once(logw),
                  ptile(d), pl.BlockSpec((1, 1, 2 * d), lambda i: (i // n_tiles, 0, 0)),
                  once(gpre), once(w_main), once(w_alr), once(wa2_pad), once(ba2),
                  pl.BlockSpec((t, 3 * LANES), lambda i: (i % n_tiles, 0)),
                  once(cmat), once(lvl), once(rmask)],
        out_specs=[tok(d_att), win, win, ptile(3 * d_att), ptile(d_att + 2 * d_gv), pwin, pwin,
                   pl.BlockSpec((1, n_pair, t, GLA_DV), lambda i: (i // n_tiles, 0, 0, 0))],
        out_shape=[f32(nseq * t_dec, d_att), f32(*ck.shape), f32(*cv.shape),
                   f32(b, s, 3 * d_att), bf16(b, s, d_att + 2 * d_gv),
                   f32(b, d_att, w), f32(b, d_att, w), f32(b, n_pair, t, GLA_DV)],
        scratch_shapes=[pltpu.VMEM((n_pair, t, GLA_DV), F32)],
        compiler_params=pltpu.CompilerParams(dimension_semantics=("arbitrary",),
                                             vmem_limit_bytes=STREAM_VMEM_LIMIT),
        name="stream",
    )(sqkv, ck, cv, logw, x, mod_p, gpre, w_main, w_alr, wa2_pad, ba2, table, cmat, lvl, rmask)
    o, ko, vo, pqkv, gates, kt, vt, s_out = outs
    to_win = lambda a: jnp.transpose(a, (0, 3, 1, 2))
    return (o, to_win(ko), to_win(vo), pqkv, gates,
            to_win(kt.reshape(b, n_heads, hd, w)), to_win(vt.reshape(b, n_heads, hd, w)),
            s_out.reshape(b, 2 * n_pair, GLA_DK, GLA_DV))


def _finish_kernel(x_ref, gate_ref, att_ref, ga_ref, og_ref, gg_ref, ggla_ref, gpost_ref, wo_ref,
                   y_ref, *, d_att, d_gv):
    att = att_ref[0] * ga_ref[0]
    og = og_ref[0].astype(F32)
    gg = gg_ref[0].astype(F32)
    gl = []
    for h in range(d_gv // GLA_DV):
        sl = slice(h * GLA_DV, (h + 1) * GLA_DV)
        oh = og[:, sl]
        nh = oh * lax.rsqrt(jnp.mean(oh * oh, axis=-1, keepdims=True) + EPS) * ggla_ref[...]
        gl.append((nh * gg[:, sl]).astype(BF16))
    mix = jnp.concatenate([att.astype(BF16)] + gl, axis=1)
    y = jnp.dot(mix, wo_ref[...], preferred_element_type=F32)
    yn = y * lax.rsqrt(jnp.mean(y * y, axis=-1, keepdims=True) + EPS) * gpost_ref[...]
    y_ref[0] = x_ref[0] + gate_ref[0] * yn


def _finish_call(x3, mod3, att, ga, og, gg, g_gla, g_post, w_out, *, tm, d_att, d_gv):
    g, r, d = x3.shape
    per_row = mod3.shape[1] != 1
    if per_row:
        gate_spec = pl.BlockSpec((1, tm, d), lambda i, j: (i, j, 2))
    else:
        gate_spec = pl.BlockSpec((1, 1, d), lambda i, j: (i, 0, 2))
    tok = lambda w, c=0: pl.BlockSpec((1, tm, w), lambda i, j: (i, j, c))
    full = lambda a: pl.BlockSpec(a.shape, lambda i, j: (0,) * a.ndim)
    kern = functools.partial(_finish_kernel, d_att=d_att, d_gv=d_gv)
    return pl.pallas_call(
        kern,
        grid=(g, r // tm),
        in_specs=[tok(d), gate_spec, tok(d_att), tok(d_att, ga[1]), tok(d_gv, og[1]), tok(d_gv, gg[1]),
                  full(g_gla), full(g_post), full(w_out)],
        out_specs=tok(d),
        out_shape=jax.ShapeDtypeStruct((g, r, d), F32),
        compiler_params=_cparams("parallel", "arbitrary"),
        name="finish",
    )(x3, mod3, att, ga[0], og[0], gg[0], g_gla, g_post, w_out)


def kernel(x_prompt, x_sample, c_prompt, c_sample, cache_k_win, cache_v_win, state_gla,
           w_mod, b_mod, g_pre, g_post, w_in, w_a2, b_a2, g_gla, w_out):
    b, s, d = x_prompt.shape
    nseq, t_dec, _ = x_sample.shape
    w = cache_k_win.shape[1]
    d_att = cache_k_win.shape[2] * cache_k_win.shape[3]
    d_gk = w_a2.shape[1]
    d_gv = state_gla.shape[1] * state_gla.shape[3]
    d_main = 4 * d_att + 2 * d_gk + 2 * d_gv
    assert w_in.shape[1] == d_main + GLA_GATE_RANK and cache_k_win.shape[3] == HEAD_DIM
    assert state_gla.shape[2] == GLA_DK and state_gla.shape[3] == GLA_DV
    assert w >= STEPS * max(DILATIONS) and s % (ATT_BLOCK * max(DILATIONS)) == 0 and s >= w

    w_main = w_in[:, :d_main].astype(BF16)
    w_alr = jnp.pad(w_in[:, d_main:], ((0, 0), (0, LANES - GLA_GATE_RANK))).astype(BF16)
    wa2_pad = jnp.pad(w_a2, ((0, LANES - GLA_GATE_RANK), (0, 0)))
    wo = w_out.astype(BF16)
    gpre = g_pre.reshape(1, d)
    gpost = g_post.reshape(1, d)
    ggla = g_gla.reshape(1, GLA_DV)
    ba2 = b_a2.reshape(1, d_gk)

    n_c = b + nseq
    n_c_pad = -(-n_c // SUBLANES) * SUBLANES
    c_all = jnp.pad(jnp.concatenate([c_prompt, c_sample], axis=0), ((0, n_c_pad - n_c), (0, 0)))
    mod = _mod_call(c_all, w_mod, b_mod)
    mod_p = mod[:b].reshape(b, 1, 3 * d)

    dims = dict(d_att=d_att, d_gk=d_gk, d_gv=d_gv)

    rows = nseq * t_dec
    tms = min(512, rows)
    xs = x_sample.reshape(rows // tms, tms, d)
    mod_s = jnp.repeat(mod[b:b + nseq], t_dec, axis=0).reshape(rows // tms, tms, 3 * d)
    tab_s = _rope_tables(np.tile(PAST_LEN + np.arange(t_dec), tms // t_dec))
    outs = _proj_call(xs, mod_s, gpre, w_main, w_alr, wa2_pad, ba2, tab_s, tm=tms, **dims)
    qkv_s, ga_s, qg_s, kg_s, la_s, vg_s, gg_s = [o.reshape(rows, o.shape[2]) for o in outs]

    tab_p = _rope_tables(np.arange(s))
    (o_att_s, k_win_sample, v_win_sample, qkv, gates, k_win_prompt, v_win_prompt,
     gla_p) = _stream_call(qkv_s, cache_k_win, cache_v_win, t_dec,
                           x_prompt, mod_p, gpre, w_main, w_alr, wa2_pad, ba2, tab_p, **dims)

    att = _att_prompt_call(qkv)
    og_s, gla_s = _gla_sample_call(qg_s, kg_s, la_s, vg_s, state_gla, t_dec)

    assert d_att == d_gv
    y_prompt = _finish_call(x_prompt, mod_p, att, (gates, 0), (gates, 2), (gates, 1),
                            ggla, gpost, wo, tm=1024, d_att=d_att, d_gv=d_gv)
    r3 = lambda a: a.reshape(rows // tms, tms, a.shape[1])
    y_sample = _finish_call(xs, mod_s, r3(o_att_s), (r3(ga_s), 0), (r3(og_s), 0), (r3(gg_s), 0),
                            ggla, gpost, wo, tm=tms, d_att=d_att, d_gv=d_gv).reshape(x_sample.shape)

    return (y_prompt, y_sample, k_win_prompt, v_win_prompt, gla_p,
            k_win_sample, v_win_sample, gla_s)
```

```python
import functools
import math

import numpy as np
import jax
import jax.numpy as jnp
from jax import lax
from jax.experimental import pallas as pl
from jax.experimental.pallas import tpu as pltpu

F32 = jnp.float32
BF16 = jnp.bfloat16

HEAD_DIM = 64
ROT_DIM = HEAD_DIM // 4
ROPE_THETA = 500000.0
DILATIONS = (1, 4, 16)
STEPS = 128
ATT_BLOCK = 128
UNITS_PER_ITER = 32
GATHERS_PER_ITER = 4
STAGED_DILATION = 4
PAST_LEN = 8192
GLA_DK = 64
GLA_DV = 128
GLA_GATE_RANK = 16
GLA_TAU = 16.0
EPS = 1e-6
NEG = -1e30
LOG2E = math.log2(math.e)

LANES = 128
SUBLANES = 8
VMEM_LIMIT = 48 * 1024 * 1024
STREAM_VMEM_LIMIT = 56 * 1024 * 1024


def _cparams(*sem):
    return pltpu.CompilerParams(dimension_semantics=sem, vmem_limit_bytes=VMEM_LIMIT)


def _nt(a, b):
    return lax.dot_general(a, b, (((1,), (1,)), ((), ())), preferred_element_type=F32)


def _split3(x):
    hi = x.astype(BF16)
    r1 = x - hi.astype(F32)
    mid = r1.astype(BF16)
    lo = (r1 - mid.astype(F32)).astype(BF16)
    return hi, mid, lo


def _mod_kernel(c_ref, w_ref, b_ref, o_ref):
    c = c_ref[...]
    a = c / (1.0 + jnp.exp(-c))
    a_hi, a_mid, _ = _split3(a)
    w_hi, w_mid, _ = _split3(w_ref[...])
    dot = functools.partial(jnp.dot, preferred_element_type=F32)
    o_ref[...] = dot(a_hi, w_hi) + (dot(a_hi, w_mid) + dot(a_mid, w_hi)) + b_ref[...]


def _mod_call(c, w_mod, b_mod):
    n, d = c.shape
    d3 = w_mod.shape[1]
    tn = 1024
    return pl.pallas_call(
        _mod_kernel,
        grid=(d3 // tn,),
        in_specs=[pl.BlockSpec((n, d), lambda j: (0, 0)),
                  pl.BlockSpec((d, tn), lambda j: (0, j)),
                  pl.BlockSpec((1, tn), lambda j: (0, j))],
        out_specs=pl.BlockSpec((n, tn), lambda j: (0, j)),
        out_shape=jax.ShapeDtypeStruct((n, d3), F32),
        compiler_params=_cparams("arbitrary"),
        name="mod",
    )(c, w_mod, b_mod.reshape(1, d3))


def _rope_tables(pos):
    f32 = np.float32
    half = ROT_DIM // 2
    inv = (f32(1.0) / (f32(ROPE_THETA) ** (np.arange(half, dtype=f32) / f32(half)))).astype(f32)
    ang = np.asarray(pos, f32)[:, None] * inv[None, :]
    cos, sin = np.cos(ang).astype(f32), np.sin(ang).astype(f32)
    n = ang.shape[0]
    ones = np.ones((n, HEAD_DIM - ROT_DIM), f32)
    zeros_h = np.zeros((n, half), f32)
    zeros_r = np.zeros((n, HEAD_DIM - ROT_DIM), f32)
    c = np.concatenate([cos, cos, ones], axis=1)
    s1 = np.concatenate([-sin, zeros_h, zeros_r], axis=1)
    s2 = np.concatenate([zeros_h, sin, zeros_r], axis=1)
    rep = LANES // HEAD_DIM
    return jnp.asarray(np.concatenate([np.tile(t, (1, rep)) for t in (c, s1, s2)], axis=1))


def _proj_tasks(x, shsc, gpre_ref, w_ref, wa2_ref, ba2_ref, tab_ref, *, d_att, d_gk, d_gv):
    d = x.shape[1]
    ms = jnp.mean(x * x, axis=-1, keepdims=True)
    h = x * lax.rsqrt(ms + EPS) * gpre_ref[...]
    h = h * (1.0 + shsc[:, d:]) + shsc[:, :d]
    hb = h.astype(BF16)
    reps = d_att // LANES
    half = ROT_DIM // 2
    res = {}

    def seg(a, n):
        return jnp.dot(hb, w_ref[:, a:a + n], preferred_element_type=F32)

    def rot(t):
        rc = jnp.concatenate([tab_ref[:, 0:LANES]] * reps, axis=1)
        rs1 = jnp.concatenate([tab_ref[:, LANES:2 * LANES]] * reps, axis=1)
        rs2 = jnp.concatenate([tab_ref[:, 2 * LANES:3 * LANES]] * reps, axis=1)
        up = pltpu.roll(t, d_att - half, axis=1)
        dn = pltpu.roll(t, half, axis=1)
        return t * rc + up * rs1 + dn * rs2

    def silu(t):
        return t / (1.0 + jnp.exp(-t))

    def log_decay(t):
        a1, a2, a3 = _split3(t)
        w1, w2, w3 = _split3(wa2_ref[...])
        dot = functools.partial(jnp.dot, preferred_element_type=F32)
        z = (dot(a1, w1) + dot(a1, w2) + dot(a2, w1)
             + dot(a1, w3) + dot(a2, w2) + dot(a3, w1)) + ba2_ref[...]
        logsig = jnp.minimum(z, 0.0) - jnp.log1p(jnp.exp(-jnp.abs(z)))
        return logsig * (1.0 / GLA_TAU)

    plan = (("q", d_att, lambda t: rot(t) * (HEAD_DIM ** -0.5)), ("k", d_att, rot),
            ("v", d_att, None), ("ga", d_att, silu),
            ("qg", d_gk, lambda t: t * (GLA_DK ** -0.5)), ("kg", d_gk, None),
            ("vg", d_gv, None), ("gg", d_gv, silu), ("la", LANES, log_decay))
    steps, off = [], 0
    for name, width, post in plan:
        def step(name=name, width=width, post=post, off=off):
            t = seg(off, width)
            res[name] = t if post is None else post(t)
        steps.append(step)
        off += width
    return steps, res


def _proj_kernel(x_ref, shsc_ref, gpre_ref, w_ref, wa2_ref, ba2_ref, tab_ref,
                 qkv_ref, *out_refs, d_att, d_gk, d_gv):
    steps, res = _proj_tasks(x_ref[0], shsc_ref[0], gpre_ref, w_ref, wa2_ref, ba2_ref, tab_ref,
                             d_att=d_att, d_gk=d_gk, d_gv=d_gv)
    for step in steps:
        step()
    for i, name in enumerate(("q", "k", "v")):
        qkv_ref[0, :, i * d_att:(i + 1) * d_att] = res[name]
    for ref, name in zip(out_refs, ("ga", "qg", "kg", "la", "vg", "gg")):
        ref[0] = res[name]


def _proj_call(x3, mod3, gpre, w_pad, wa2_pad, ba2, table, *, d_att, d_gk, d_gv, tm):
    g, r, d = x3.shape
    grid = (g, r // tm)
    full = lambda a: pl.BlockSpec(a.shape, lambda i, j: (0,) * a.ndim)
    widths = (3 * d_att, d_att, d_gk, d_gk, d_gk, d_gv, d_gv)
    kern = functools.partial(_proj_kernel, d_att=d_att, d_gk=d_gk, d_gv=d_gv)
    return pl.pallas_call(
        kern,
        grid=grid,
        in_specs=[pl.BlockSpec((1, tm, d), lambda i, j: (i, j, 0)),
                  pl.BlockSpec((1, tm, 2 * d), lambda i, j: (i, j, 0)),
                  full(gpre), full(w_pad), full(wa2_pad), full(ba2),
                  pl.BlockSpec((tm, 3 * LANES), lambda i, j: (0, 0))],
        out_specs=[pl.BlockSpec((1, tm, w), lambda i, j: (i, j, 0)) for w in widths],
        out_shape=[jax.ShapeDtypeStruct((g, r, w), F32) for w in widths],
        compiler_params=_cparams("parallel", "arbitrary"),
        name="proj",
    )(x3, mod3, gpre, w_pad, wa2_pad, ba2, table)


def _band_bias():
    row = np.arange(ATT_BLOCK)[:, None]
    col = np.arange(2 * ATT_BLOCK)[None, :]
    dist = row + ATT_BLOCK - col
    band = (dist >= 0) & (dist <= STEPS)
    first = band & (col >= ATT_BLOCK)
    return jnp.asarray(np.where(np.stack([first, band]), 0.0, NEG), F32)


def _att_buf_rows(seq):
    offs, total = {}, 0
    for dil in DILATIONS:
        offs[dil] = total
        total += seq + dil * ATT_BLOCK
    return offs, total


def _att_prompt_kernel(q_ref, k_ref, v_ref, bias_ref, o_ref,
                       qb_ref, kb_ref, vb_ref, acc_ref, m_ref, l_ref, *stage_refs, seq):
    blk = ATT_BLOCK
    n_units = seq // blk
    offs, _ = _att_buf_rows(seq)
    lane = lax.broadcasted_iota(jnp.int32, (blk, LANES), 1)
    lo = lane < HEAD_DIM
    head_masks = (jnp.where(lo, 1.0, 0.0).astype(BF16), jnp.where(lo, 0.0, 1.0).astype(BF16))
    ones = jnp.ones((2 * blk, LANES), BF16)
    zeros = jnp.zeros((blk, LANES), BF16)

    def place(u, dil):
        r = lax.rem(u, dil)
        n = lax.div(u, dil)
        first = r + dil * blk * n
        base = offs[dil] + r * (seq // dil + blk) + blk * n
        return n, first, pl.multiple_of(base, blk)

    def rows(first, dil):
        return pl.ds(first, blk) if dil == 1 else pl.ds(first, blk, stride=dil)

    srcs = (q_ref, k_ref, v_ref)
    bufs = (qb_ref, kb_ref, vb_ref)
    scales = (LOG2E, None, None)
    parent = {d: max([p for p in DILATIONS if 1 < p < d and d % p == 0], default=None) for d in DILATIONS}
    assert set(parent.values()) <= {None, STAGED_DILATION}
    for dil in DILATIONS:
        for r in range(dil):
            z0 = offs[dil] + r * (seq // dil + blk)
            kb_ref[z0:z0 + blk, :] = zeros
            vb_ref[z0:z0 + blk, :] = zeros

        def gather(u, carry, dil=dil):
            n, first, base = place(u, dil)
            r = lax.rem(u, dil)
            par = parent[dil]
            for a in range(3):
                if par is None:
                    x = srcs[a][0, rows(first, dil), :]
                else:
                    step = dil // par
                    start = lax.rem(r, par) * (seq // par) + lax.div(r, par) + step * blk * n
                    x = stage_refs[a][pl.ds(start, blk, stride=step), :]
                if dil == STAGED_DILATION:
                    stage_refs[a][pl.ds(pl.multiple_of(r * (seq // dil) + blk * n, blk), blk), :] = x
                if scales[a] is not None:
                    x = x * scales[a]
                bufs[a][pl.ds(base + blk, blk), :] = x.astype(BF16)
            return carry

        lax.fori_loop(0, n_units, gather, 0, unroll=GATHERS_PER_ITER)

    order = tuple(reversed(DILATIONS))
    for dil in order:
        def unit(u, carry, dil=dil):
            n, first, base = place(u, dil)
            qb = qb_ref[pl.ds(base + blk, blk), :]
            kcat = kb_ref[pl.ds(base, 2 * blk), :]
            vext = jnp.concatenate([vb_ref[pl.ds(base, 2 * blk), :], ones], axis=1)
            bias = bias_ref[jnp.minimum(n, 1)]
            os, ms, ls = [], [], []
            for hm in head_masks:
                s = _nt(qb * hm, kcat) + bias
                m = jnp.max(s, axis=-1, keepdims=True)
                p = jnp.exp2(s - m)
                res = jnp.dot(p.astype(BF16), vext, preferred_element_type=F32)
                os.append(res[:, :LANES])
                ls.append(res[:, LANES:])
                ms.append(m)
            o_new = jnp.where(lo, os[0], os[1])
            m_new = jnp.where(lo, ms[0], ms[1])
            l_new = jnp.where(lo, ls[0], ls[1])
            dst = rows(first, dil)
            if dil == order[0]:
                acc_ref[dst, :] = o_new
                m_ref[dst, :] = m_new
                l_ref[dst, :] = l_new
                return carry
            m_old = m_ref[dst, :]
            m_tot = jnp.maximum(m_old, m_new)
            a_old = jnp.exp2(m_old - m_tot)
            a_new = jnp.exp2(m_new - m_tot)
            acc = a_old * acc_ref[dst, :] + a_new * o_new
            den = a_old * l_ref[dst, :] + a_new * l_new
            if dil == order[-1]:
                o_ref[0, pl.ds(pl.multiple_of(first, blk), blk), :] = (acc / den).astype(o_ref.dtype)
            else:
                acc_ref[dst, :] = acc
                l_ref[dst, :] = den
                m_ref[dst, :] = m_tot
            return carry

        lax.fori_loop(0, n_units, unit, 0, unroll=UNITS_PER_ITER)


def _att_prompt_call(qkv):
    b, s, d3 = qkv.shape
    d_att = d3 // 3
    nblk = d_att // LANES
    bias = _band_bias()
    _, buf_rows = _att_buf_rows(s)
    slab = pl.BlockSpec((1, s, LANES), lambda i, j: (i, 0, j))
    part = lambda p: pl.BlockSpec((1, s, LANES), lambda i, j: (i, 0, p * nblk + j))
    kern = functools.partial(_att_prompt_kernel, seq=s)
    return pl.pallas_call(
        kern,
        grid=(b, d_att // LANES),
        in_specs=[part(0), part(1), part(2), pl.BlockSpec(bias.shape, lambda i, j: (0, 0, 0))],
        out_specs=slab,
        out_shape=jax.ShapeDtypeStruct((b, s, d_att), BF16),
        scratch_shapes=[pltpu.VMEM((buf_rows, LANES), BF16)] * 3 + [pltpu.VMEM((s, LANES), F32)] * 6,
        compiler_params=_cparams("parallel", "parallel"),
        name="att_prompt",
    )(qkv, qkv, qkv, bias)


def _gla_consts(blk):
    t = np.arange(LANES)
    same = (t[:, None] // blk) == (t[None, :] // blk)
    tri = ((t[None, :] <= t[:, None]) & same).astype(np.int64)
    mats, lvl, right_rows = [tri], [], []
    h = blk // 2
    while h >= 1:
        pair = t // (2 * h)
        right = (t // h) % 2 == 1
        mid = pair * 2 * h + h - 1
        upto_mid = ((t[None, :] <= mid[:, None]) & same).astype(np.int64)
        mats.append(tri - upto_mid)
        lvl.append(pair[:, None] == pair[None, :])
        right_rows.append(np.broadcast_to(right[:, None], (LANES, LANES)))
        h //= 2
    lvl.append(np.eye(LANES, dtype=bool))
    if blk < LANES:
        mats.append(same.astype(np.int64))
    cmat = jnp.asarray(np.concatenate(mats, 0), BF16)
    lvl = jnp.asarray(np.concatenate(lvl, 0), F32)
    rmask = jnp.asarray(np.concatenate(right_rows, 0), F32)
    return cmat, lvl, rmask, len(right_rows)


def _gla_tile_tasks(q, k, la, cmat_ref, lvl_ref, rmask_ref, n_lvl):
    t = LANES
    n_pair = q.shape[1] // LANES
    st = {}
    pairs = [None] * n_pair

    def cumulate():
        hi, mid, lo3 = _split3(la)
        cm = cmat_ref[...]
        dot = functools.partial(jnp.dot, preferred_element_type=F32)
        g = dot(cm, hi) + dot(cm, mid) + dot(cm, lo3)
        b = g[0:t]
        if cm.shape[0] > (n_lvl + 1) * t:
            btot = g[(n_lvl + 1) * t:(n_lvl + 2) * t]
        else:
            btot = jnp.broadcast_to(b[t - 1:t, :], b.shape)
        st.update(g=g, b=b, btot=btot)

    def pair(j):
        g, b, btot = st["g"], st["b"], st["btot"]
        lane = lax.broadcasted_iota(jnp.int32, (t, LANES), 1)
        lo = lane < GLA_DK
        sl = slice(j * LANES, (j + 1) * LANES)
        qp, kp, bp = q[:, sl], k[:, sl], b[:, sl]
        a0 = jnp.zeros((t, t), F32)
        a1 = jnp.zeros((t, t), F32)
        for lv in range(n_lvl + 1):
            m = lvl_ref[lv * t:(lv + 1) * t, :]
            if lv < n_lvl:
                dl = g[(lv + 1) * t:(lv + 2) * t, sl]
                e = jnp.exp(-jnp.abs(dl))
                rm = rmask_ref[lv * t:(lv + 1) * t, :]
                x = qp * e * rm
                y = (kp * e * (1.0 - rm)).astype(BF16)
            else:
                x = qp
                y = kp.astype(BF16)
            x0 = jnp.where(lo, x, 0.0).astype(BF16)
            x1 = jnp.where(lo, 0.0, x).astype(BF16)
            a0 = a0 + m * _nt(x0, y)
            a1 = a1 + m * _nt(x1, y)
        q_in = qp * jnp.exp(bp)
        k_in = kp * jnp.exp(btot[:, sl] - bp)
        pairs[j] = (a0, a1, q_in, k_in, btot[:, sl], lo)

    return [cumulate] + [functools.partial(pair, j) for j in range(n_pair)], pairs


def _gla_prompt_tasks(res, cmat_ref, lvl_ref, rmask_ref, o_ref, o_col, s_ref, *, n_lvl, d_gk):
    t = LANES
    n_pair = d_gk // LANES
    holder = {}

    def start():
        steps, pairs = _gla_tile_tasks(res["qg"], res["kg"], res["la"],
                                       cmat_ref, lvl_ref, rmask_ref, n_lvl)
        holder.update(steps=steps, pairs=pairs)
        steps[0]()

    def levels(j):
        holder["steps"][1 + j]()

    def finish(j):
        a0, a1, q_in, k_in, btot, lo = holder["pairs"][j]
        v = res["vg"]
        row = lax.broadcasted_iota(jnp.int32, (t, t), 0)
        top = row < GLA_DK
        s_prev = s_ref[j]
        sb = s_prev.astype(BF16)
        kt = k_in.T.astype(BF16)
        us = []
        for hh, (a, sel) in enumerate(((a0, lo), (a1, ~lo))):
            vs = slice((2 * j + hh) * GLA_DV, (2 * j + hh + 1) * GLA_DV)
            vb = v[:, vs].astype(BF16)
            qm = jnp.where(sel, q_in, 0.0).astype(BF16)
            o = (jnp.dot(a.astype(BF16), vb, preferred_element_type=F32)
                 + jnp.dot(qm, sb, preferred_element_type=F32))
            o_ref[0, :, o_col + vs.start:o_col + vs.stop] = o.astype(o_ref.dtype)
            us.append(jnp.dot(kt, vb, preferred_element_type=F32))
        a_col = jnp.exp(btot.T)
        s_ref[j] = a_col * s_prev + jnp.where(top, us[0], us[1])

    tasks = [start]
    for j in range(n_pair):
        tasks += [functools.partial(levels, j), functools.partial(finish, j)]
    return tasks


def _gla_sample_kernel(q_ref, k_ref, la_ref, v_ref, s_in_ref, cmat_ref, lvl_ref, rmask_ref,
                       o_ref, s_out_ref, *, n_lvl, t_dec):
    steps, pairs = _gla_tile_tasks(q_ref[...], k_ref[...], la_ref[...],
                                   cmat_ref, lvl_ref, rmask_ref, n_lvl)
    for step in steps:
        step()
    t = LANES
    nb = t // t_dec
    row = lax.broadcasted_iota(jnp.int32, (t, t), 0)
    col = lax.broadcasted_iota(jnp.int32, (t, t), 1)
    top = row < GLA_DK
    for j, (a0, a1, q_in, k_in, btot, lo) in enumerate(pairs):
        s_prev = s_in_ref[:, j]
        sb = s_prev.astype(BF16)
        kt = k_in.T
        bt = btot.T
        vbs = []
        for hh, (a, sel) in enumerate(((a0, lo), (a1, ~lo))):
            vs = slice((2 * j + hh) * GLA_DV, (2 * j + hh + 1) * GLA_DV)
            vb = v_ref[:, vs].astype(BF16)
            vbs.append(vb)
            qm = jnp.where(sel, q_in, 0.0)
            o_inter = jnp.concatenate(
                [jnp.dot(qm[i * t_dec:(i + 1) * t_dec].astype(BF16), sb[i],
                         preferred_element_type=F32) for i in range(nb)], axis=0)
            o_ref[:, vs] = jnp.dot(a.astype(BF16), vb, preferred_element_type=F32) + o_inter
        for i in range(nb):
            mine = (col >= i * t_dec) & (col < (i + 1) * t_dec)
            kti = (kt * mine.astype(F32)).astype(BF16)
            u0 = jnp.dot(kti, vbs[0], preferred_element_type=F32)
            u1 = jnp.dot(kti, vbs[1], preferred_element_type=F32)
            a_col = jnp.exp(jnp.broadcast_to(bt[:, i * t_dec:i * t_dec + 1], (t, GLA_DV)))
            s_out_ref[i, j] = a_col * s_prev[i] + jnp.where(top, u0, u1)


def _gla_sample_call(qg, kg, la, vg, state, t_dec):
    rows, d_gk = qg.shape
    d_gv = vg.shape[1]
    n_pair = d_gk // LANES
    nseq = rows // t_dec
    t = LANES
    nb = t // t_dec
    cmat, lvl, rmask, n_lvl = _gla_consts(t_dec)
    s_in = state.reshape(nseq, n_pair, 2 * GLA_DK, GLA_DV)
    tok = lambda w: pl.BlockSpec((t, w), lambda i: (i, 0))
    full = lambda a: pl.BlockSpec(a.shape, lambda i: (0,) * a.ndim)
    st = pl.BlockSpec((nb, n_pair, t, GLA_DV), lambda i: (i, 0, 0, 0))
    kern = functools.partial(_gla_sample_kernel, n_lvl=n_lvl, t_dec=t_dec)
    o, s_out = pl.pallas_call(
        kern,
        grid=(rows // t,),
        in_specs=[tok(d_gk), tok(d_gk), tok(d_gk), tok(d_gv), st, full(cmat), full(lvl), full(rmask)],
        out_specs=[tok(d_gv), st],
        out_shape=[jax.ShapeDtypeStruct((rows, d_gv), F32),
                   jax.ShapeDtypeStruct(s_in.shape, F32)],
        compiler_params=_cparams("parallel"),
        name="gla_sample",
    )(qg, kg, la, vg, s_in, cmat, lvl, rmask)
    return o, s_out.reshape(state.shape)


def _sample_att_table(w, t_dec):
    idx = np.arange(w + LANES)[None, :]
    tq = np.arange(t_dec)[:, None]
    dist = w + tq - idx
    mult = np.zeros((t_dec, w + LANES), np.int64)
    for d in DILATIONS:
        mult += (dist >= 0) & (dist % d == 0) & (dist // d <= STEPS) & (idx < w + t_dec)
    logw = np.where(mult > 0, np.log(np.maximum(mult, 1)), NEG).astype(np.float32)
    return jnp.asarray(logw)


def _att_sample_tasks(qkv_ref, ck_ref, cv_ref, logw_ref, o_ref, ko_ref, vo_ref, *, w, t_dec, n_heads):
    lane = lax.broadcasted_iota(jnp.int32, (HEAD_DIM, LANES), 1)
    keep = lane < LANES - t_dec
    d_att = n_heads * HEAD_DIM
    zpad = jnp.zeros((LANES - t_dec, d_att), F32)
    q = qkv_ref[:, 0:d_att]
    knt = jnp.concatenate([qkv_ref[:, d_att:2 * d_att], zpad], axis=0).T
    vnt = jnp.concatenate([qkv_ref[:, 2 * d_att:3 * d_att], zpad], axis=0).T
    logw_c = logw_ref[:, 0:w]
    logw_n = logw_ref[:, w:w + LANES]
    outs = [None] * n_heads

    def head(h):
        hs = slice(h * HEAD_DIM, (h + 1) * HEAD_DIM)
        kt = ck_ref[0, h]
        vt = cv_ref[0, h]
        knh = knt[hs, :]
        vnh = vnt[hs, :]
        qh = q[:, hs].astype(BF16)
        s_c = jnp.dot(qh, kt.astype(BF16), preferred_element_type=F32) + logw_c
        s_n = jnp.dot(qh, knh.astype(BF16), preferred_element_type=F32) + logw_n
        m = jnp.maximum(jnp.max(s_c, axis=-1, keepdims=True), jnp.max(s_n, axis=-1, keepdims=True))
        p_c = jnp.exp(s_c - m)
        p_n = jnp.exp(s_n - m)
        l = jnp.sum(p_c, axis=-1, keepdims=True) + jnp.sum(p_n, axis=-1, keepdims=True)
        o = _nt(p_c.astype(BF16), vt.astype(BF16)) + _nt(p_n.astype(BF16), vnh.astype(BF16))
        outs[h] = o / l
        for src, new, dst in ((kt, knh, ko_ref), (vt, vnh, vo_ref)):
            sh = pltpu.roll(src, w - t_dec, axis=1)
            tail = jnp.where(keep, sh[:, w - LANES:], pltpu.roll(new, LANES - t_dec, axis=1))
            dst[0, h, :, 0:w - LANES] = sh[:, 0:w - LANES]
            dst[0, h, :, w - LANES:w] = tail

    def finish():
        o_ref[...] = jnp.concatenate(outs, axis=1)

    return [functools.partial(head, h) for h in range(n_heads)] + [finish]


def _stream_kernel(sqkv_ref, ck_ref, cv_ref, logw_ref,
                   x_ref, shsc_ref, gpre_ref, w_ref, wa2_ref, ba2_ref, tab_ref,
                   cmat_ref, lvl_ref, rmask_ref,
                   o_ref, ko_ref, vo_ref, pqkv_ref, gates_ref, kt_ref, vt_ref, s_out_ref,
                   s_ref, *, w, t_dec, n_heads, n_lvl, n_tiles, d_att, d_gk, d_gv):
    n = lax.rem(pl.program_id(0), n_tiles)

    @pl.when(n == 0)
    def _():
        s_ref[...] = jnp.zeros_like(s_ref)

    win_steps = _att_sample_tasks(sqkv_ref, ck_ref, cv_ref, logw_ref, o_ref, ko_ref, vo_ref,
                                  w=w, t_dec=t_dec, n_heads=n_heads)
    proj_steps, res = _proj_tasks(x_ref[0], shsc_ref[0], gpre_ref, w_ref, wa2_ref, ba2_ref, tab_ref,
                                  d_att=d_att, d_gk=d_gk, d_gv=d_gv)

    def emit():
        for i, name in enumerate(("q", "k", "v")):
            pqkv_ref[0, :, i * d_att:(i + 1) * d_att] = res[name]
        gates_ref[0, :, 0:d_att] = res["ga"].astype(gates_ref.dtype)
        gates_ref[0, :, d_att:d_att + d_gv] = res["gg"].astype(gates_ref.dtype)
        kt_ref[0] = res["k"].T
        vt_ref[0] = res["v"].T

    tile_steps = proj_steps + [emit] + _gla_prompt_tasks(
        res, cmat_ref, lvl_ref, rmask_ref, gates_ref, d_att + d_gv, s_ref, n_lvl=n_lvl, d_gk=d_gk)
    for step in win_steps + tile_steps:
        step()

    @pl.when(n == n_tiles - 1)
    def _():
        s_out_ref[0] = s_ref[...]


def _stream_call(sqkv, cache_k, cache_v, t_dec, x, mod_p, gpre, w_pad, wa2_pad, ba2, table,
                 *, d_att, d_gk, d_gv):
    nseq, w, n_heads, hd = cache_k.shape
    b, s, d = x.shape
    t = LANES
    n_tiles = s // t
    n_pair = d_gk // LANES
    first_win_tile = (s - w) // t
    assert nseq == b * n_tiles, "one 128-row prompt tile is paired with each sample sequence"
    ck = jnp.transpose(cache_k, (0, 2, 3, 1))
    cv = jnp.transpose(cache_v, (0, 2, 3, 1))
    logw = _sample_att_table(w, t_dec)
    cmat, lvl, rmask, n_lvl = _gla_consts(LANES)
    tok = lambda wd: pl.BlockSpec((t_dec, wd), lambda i: (i, 0))
    win = pl.BlockSpec((1, n_heads, hd, w), lambda i: (i, 0, 0, 0))
    once = lambda a: pl.BlockSpec(a.shape, lambda i: (0,) * a.ndim, pipeline_mode=pl.Buffered(1))
    ptile = lambda wd: pl.BlockSpec((1, t, wd), lambda i: (i // n_tiles, i % n_tiles, 0))
    pwin = pl.BlockSpec((1, d_att, t),
                        lambda i: (i // n_tiles, 0, jnp.maximum(i % n_tiles - first_win_tile, 0)))
    kern = functools.partial(_stream_kernel, w=w, t_dec=t_dec, n_heads=n_heads, n_lvl=n_lvl,
                             n_tiles=n_tiles, d_att=d_att, d_gk=d_gk, d_gv=d_gv)
    f32 = lambda *shape: jax.ShapeDtypeStruct(shape, F32)
    bf16 = lambda *shape: jax.ShapeDtypeStruct(shape, BF16)
    outs = pl.pallas_call(
        kern,
        grid=(nseq,),
        in_specs=[tok(3 * d_att), win, win, once(logw),
                  ptile(d), pl.BlockSpec((1, 1, 2 * d), lambda i: (i // n_tiles, 0, 0)),
                  once(gpre), once(w_pad), once(wa2_pad), once(ba2),
                  pl.BlockSpec((t, 3 * LANES), lambda i: (i % n_tiles, 0)),
                  once(cmat), once(lvl), once(rmask)],
        out_specs=[tok(d_att), win, win, ptile(3 * d_att), ptile(d_att + 2 * d_gv), pwin, pwin,
                   pl.BlockSpec((1, n_pair, t, GLA_DV), lambda i: (i // n_tiles, 0, 0, 0))],
        out_shape=[f32(nseq * t_dec, d_att), f32(*ck.shape), f32(*cv.shape),
                   f32(b, s, 3 * d_att), bf16(b, s, d_att + 2 * d_gv),
                   f32(b, d_att, w), f32(b, d_att, w), f32(b, n_pair, t, GLA_DV)],
        scratch_shapes=[pltpu.VMEM((n_pair, t, GLA_DV), F32)],
        compiler_params=pltpu.CompilerParams(dimension_semantics=("arbitrary",),
                                             vmem_limit_bytes=STREAM_VMEM_LIMIT),
        name="stream",
    )(sqkv, ck, cv, logw, x, mod_p, gpre, w_pad, wa2_pad, ba2, table, cmat, lvl, rmask)
    o, ko, vo, pqkv, gates, kt, vt, s_out = outs
    to_win = lambda a: jnp.transpose(a, (0, 3, 1, 2))
    return (o, to_win(ko), to_win(vo), pqkv, gates,
            to_win(kt.reshape(b, n_heads, hd, w)), to_win(vt.reshape(b, n_heads, hd, w)),
            s_out.reshape(b, 2 * n_pair, GLA_DK, GLA_DV))


def _finish_kernel(x_ref, gate_ref, att_ref, ga_ref, og_ref, gg_ref, ggla_ref, gpost_ref, wo_ref,
                   y_ref, *, d_att, d_gv):
    att = att_ref[0] * ga_ref[0]
    og = og_ref[0].astype(F32)
    gg = gg_ref[0].astype(F32)
    gl = []
    for h in range(d_gv // GLA_DV):
        sl = slice(h * GLA_DV, (h + 1) * GLA_DV)
        oh = og[:, sl]
        nh = oh * lax.rsqrt(jnp.mean(oh * oh, axis=-1, keepdims=True) + EPS) * ggla_ref[...]
        gl.append((nh * gg[:, sl]).astype(BF16))
    mix = jnp.concatenate([att.astype(BF16)] + gl, axis=1)
    y = jnp.dot(mix, wo_ref[...], preferred_element_type=F32)
    yn = y * lax.rsqrt(jnp.mean(y * y, axis=-1, keepdims=True) + EPS) * gpost_ref[...]
    y_ref[0] = x_ref[0] + gate_ref[0] * yn


def _finish_call(x3, mod3, att, ga, og, gg, g_gla, g_post, w_out, *, tm, d_att, d_gv):
    g, r, d = x3.shape
    per_row = mod3.shape[1] != 1
    if per_row:
        gate_spec = pl.BlockSpec((1, tm, d), lambda i, j: (i, j, 2))
    else:
        gate_spec = pl.BlockSpec((1, 1, d), lambda i, j: (i, 0, 2))
    tok = lambda w, c=0: pl.BlockSpec((1, tm, w), lambda i, j: (i, j, c))
    full = lambda a: pl.BlockSpec(a.shape, lambda i, j: (0,) * a.ndim)
    kern = functools.partial(_finish_kernel, d_att=d_att, d_gv=d_gv)
    return pl.pallas_call(
        kern,
        grid=(g, r // tm),
        in_specs=[tok(d), gate_spec, tok(d_att), tok(d_att, ga[1]), tok(d_gv, og[1]), tok(d_gv, gg[1]),
                  full(g_gla), full(g_post), full(w_out)],
        out_specs=tok(d),
        out_shape=jax.ShapeDtypeStruct((g, r, d), F32),
        compiler_params=_cparams("parallel", "arbitrary"),
        name="finish",
    )(x3, mod3, att, ga[0], og[0], gg[0], g_gla, g_post, w_out)


def kernel(x_prompt, x_sample, c_prompt, c_sample, cache_k_win, cache_v_win, state_gla,
           w_mod, b_mod, g_pre, g_post, w_in, w_a2, b_a2, g_gla, w_out):
    b, s, d = x_prompt.shape
    nseq, t_dec, _ = x_sample.shape
    w = cache_k_win.shape[1]
    d_att = cache_k_win.shape[2] * cache_k_win.shape[3]
    d_gk = w_a2.shape[1]
    d_gv = state_gla.shape[1] * state_gla.shape[3]
    d_main = 4 * d_att + 2 * d_gk + 2 * d_gv
    assert w_in.shape[1] == d_main + GLA_GATE_RANK and cache_k_win.shape[3] == HEAD_DIM
    assert state_gla.shape[2] == GLA_DK and state_gla.shape[3] == GLA_DV
    assert w >= STEPS * max(DILATIONS) and s % (ATT_BLOCK * max(DILATIONS)) == 0 and s >= w

    w_pad = jnp.pad(w_in.astype(BF16), ((0, 0), (0, LANES - GLA_GATE_RANK)))
    wa2_pad = jnp.pad(w_a2, ((0, LANES - GLA_GATE_RANK), (0, 0)))
    wo = w_out.astype(BF16)
    gpre = g_pre.reshape(1, d)
    gpost = g_post.reshape(1, d)
    ggla = g_gla.reshape(1, GLA_DV)
    ba2 = b_a2.reshape(1, d_gk)

    n_c = b + nseq
    n_c_pad = -(-n_c // SUBLANES) * SUBLANES
    c_all = jnp.pad(jnp.concatenate([c_prompt, c_sample], axis=0), ((0, n_c_pad - n_c), (0, 0)))
    mod = _mod_call(c_all, w_mod, b_mod)
    mod_p = mod[:b].reshape(b, 1, 3 * d)

    dims = dict(d_att=d_att, d_gk=d_gk, d_gv=d_gv)

    rows = nseq * t_dec
    tms = min(512, rows)
    xs = x_sample.reshape(rows // tms, tms, d)
    mod_s = jnp.repeat(mod[b:b + nseq], t_dec, axis=0).reshape(rows // tms, tms, 3 * d)
    tab_s = _rope_tables(np.tile(PAST_LEN + np.arange(t_dec), tms // t_dec))
    outs = _proj_call(xs, mod_s, gpre, w_pad, wa2_pad, ba2, tab_s, tm=tms, **dims)
    qkv_s, ga_s, qg_s, kg_s, la_s, vg_s, gg_s = [o.reshape(rows, o.shape[2]) for o in outs]

    tab_p = _rope_tables(np.arange(s))
    (o_att_s, k_win_sample, v_win_sample, qkv, gates, k_win_prompt, v_win_prompt,
     gla_p) = _stream_call(qkv_s, cache_k_win, cache_v_win, t_dec,
                           x_prompt, mod_p, gpre, w_pad, wa2_pad, ba2, tab_p, **dims)

    att = _att_prompt_call(qkv)
    og_s, gla_s = _gla_sample_call(qg_s, kg_s, la_s, vg_s, state_gla, t_dec)

    assert d_att == d_gv
    y_prompt = _finish_call(x_prompt, mod_p, att, (gates, 0), (gates, 2), (gates, 1),
                            ggla, gpost, wo, tm=1024, d_att=d_att, d_gv=d_gv)
    r3 = lambda a: a.reshape(rows // tms, tms, a.shape[1])
    y_sample = _finish_call(xs, mod_s, r3(o_att_s), (r3(ga_s), 0), (r3(og_s), 0), (r3(gg_s), 0),
                            ggla, gpost, wo, tm=tms, d_att=d_att, d_gv=d_gv).reshape(x_sample.shape)

    return (y_prompt, y_sample, k_win_prompt, v_win_prompt, gla_p,
            k_win_sample, v_win_sample, gla_s)
```
